```python
import jax, jax.numpy as jnp
from jax import lax
import numpy as np

D_MODEL = 1024
BATCH = 4
SEQ = 8192
DEPTH = 2

GRID_W = 64
CTX_LEN = 256
MIX_W = D_MODEL
LRU_W = MIX_W // 2
LRU_HEADS = 8
LRU_HD = LRU_W // LRU_HEADS
CONV_W = MIX_W // 4
FFT_W = MIX_W - LRU_W - CONV_W
FFT_GROUPS = 4
FFT_GD = FFT_W // FFT_GROUPS
LRU_CONV_K = 4
LRU_CONV_LEFT = 1
SHORT_CONV_K = 3
SHORT_CONV_LEFT = 1
LRU_C = 8.0
IN_COLS = 2 * LRU_W + 3 * CONV_W + FFT_W
N_EXPERTS = 16
EXPERT_FF = D_MODEL
CAPACITY_FACTOR = 2
EPS = 1e-6

kernel_name = 'hybrid_lru_conv_fourier_ec_moe_dit'


def rmsnorm(x, g):
    xf = x.astype(jnp.float32)
    y = xf * lax.rsqrt(jnp.mean(xf * xf, axis=-1, keepdims=True) + EPS)
    return (y * g.astype(jnp.float32)).astype(x.dtype)


def modulate(h, shift, scale):
    return h * (1 + scale) + shift


def shift_conv(u, w, left, axis):
    k_w = w.shape[0]
    n = u.shape[axis]
    pad = [(0, 0)] * u.ndim
    pad[axis] = (left, k_w - 1 - left)
    up = jnp.pad(u, pad)
    out = w[0] * lax.slice_in_dim(up, 0, n, axis=axis)
    for k in range(1, k_w):
        out = out + w[k] * lax.slice_in_dim(up, k, k + n, axis=axis)
    return out


def conv_seq(u, w, left):
    return shift_conv(u, w, left, axis=1)


def conv_grid(u, w, left):
    b, n, ch = u.shape
    rows = n // GRID_W
    return shift_conv(u.reshape(b, rows, GRID_W, ch), w, left, axis=2).reshape(b, n, ch)


def rglru_coeffs(u, w_r, b_r, w_i, b_i, lam):
    b, n, _ = u.shape
    uh = u.astype(jnp.float32).reshape(b, n, LRU_HEADS, LRU_HD)
    r = jax.nn.sigmoid(jnp.einsum('blhi,hij->blhj', uh, w_r.astype(jnp.float32)).reshape(b, n, LRU_W) + b_r.astype(jnp.float32))
    i = jax.nn.sigmoid(jnp.einsum('blhi,hij->blhj', uh, w_i.astype(jnp.float32)).reshape(b, n, LRU_W) + b_i.astype(jnp.float32))
    log_a = -LRU_C * r * jax.nn.softplus(-lam.astype(jnp.float32))
    a = jnp.exp(log_a)
    inp = jnp.sqrt(-jnp.expm1(2.0 * log_a)) * (i * uh.reshape(b, n, LRU_W))
    return a, inp


def _combine(left, right):
    a1, b1 = left
    a2, b2 = right
    return a1 * a2, a2 * b1 + b2


def linear_scan(a, b, h0):
    b = b.at[:, 0].add(a[:, 0] * h0)
    _, h = lax.associative_scan(_combine, (a, b), axis=1)
    return h


def lru_states(u, w_r, b_r, w_i, b_i, lam, h0_f, h0_b):
    a, b = rglru_coeffs(u, w_r[0], b_r[0], w_i[0], b_i[0], lam[0])
    h_f = linear_scan(a, b, h0_f)
    a, b = rglru_coeffs(u, w_r[1], b_r[1], w_i[1], b_i[1], lam[1])
    h_b = jnp.flip(linear_scan(jnp.flip(a, 1), jnp.flip(b, 1), h0_b), 1)
    return h_f, h_b


def fourier_mix(u):
    b, n, w = u.shape
    uf = u.astype(jnp.float32).reshape(b, n, FFT_GROUPS, FFT_GD)
    y = jnp.real(jnp.fft.fftn(uf, axes=(1, 3), norm='ortho'))
    return y.reshape(b, n, w).astype(u.dtype)


def mixer_out(p, h_f, h_b, conv_fn, sc_w, g_out, w_out):
    o = 2 * LRU_W
    lru_g = p[..., LRU_W:o]
    sc_b = p[..., o:o + CONV_W]
    sc_c = p[..., o + CONV_W:o + 2 * CONV_W]
    sc_x = p[..., o + 2 * CONV_W:o + 3 * CONV_W]
    fft_x = p[..., o + 3 * CONV_W:]
    y_lru = ((h_f + h_b) * jax.nn.gelu(lru_g.astype(jnp.float32))).astype(p.dtype)
    y_sc = sc_b * conv_fn(sc_c * sc_x, sc_w, SHORT_CONV_LEFT)
    y_fft = fourier_mix(fft_x)
    y = jnp.concatenate([
        rmsnorm(y_lru, g_out[:LRU_W]),
        rmsnorm(y_sc, g_out[LRU_W:LRU_W + CONV_W]),
        rmsnorm(y_fft, g_out[LRU_W + CONV_W:]),
    ], axis=-1)
    return y @ w_out


def expert_choice_ffn(h, w_router, w_gate, w_up, w_down):
    b, n, d = h.shape
    cap = CAPACITY_FACTOR * n // N_EXPERTS
    logits = jnp.einsum('bld,de->ble', h, w_router).astype(jnp.float32)
    aff = jax.nn.softmax(logits, axis=-1)
    g, idx = lax.top_k(jnp.swapaxes(aff, 1, 2), cap)
    xs = jax.vmap(lambda hb, ib: hb[ib])(h, idx)
    hid = jax.nn.silu(jnp.einsum('becd,edf->becf', xs, w_gate)) * jnp.einsum('becd,edf->becf', xs, w_up)
    ys = jnp.einsum('becf,efd->becd', hid, w_down) * g[..., None].astype(hid.dtype)
    return jax.vmap(lambda ib, yb: jnp.zeros((n, d), yb.dtype).at[ib.reshape(-1)].add(yb.reshape(-1, d)))(idx, ys)


def setup_inputs(seed: int = 0) -> dict:
    key = jax.random.key(seed)
    ks = jax.random.split(key, 26)
    f32 = jnp.float32
    nrm = lambda k, shape, s: jax.random.normal(k, shape, f32) * s
    u = jax.random.uniform(ks[14], (DEPTH, 2, LRU_W), f32, 0.9, 0.999)
    a0 = u ** (1.0 / LRU_C)
    lam = jnp.log(a0) - jnp.log1p(-a0)
    return {
        'x': nrm(ks[0], (BATCH, SEQ, D_MODEL), 1.0),
        'c': nrm(ks[1], (BATCH, D_MODEL), 1.0),
        'ctx': nrm(ks[2], (BATCH, CTX_LEN, D_MODEL), 1.0),
        'c_ctx': nrm(ks[3], (D_MODEL,), 1.0),
        'w_ada': nrm(ks[4], (DEPTH, D_MODEL, 6 * D_MODEL), 0.5 * D_MODEL ** -0.5),
        'b_ada': nrm(ks[5], (DEPTH, 6 * D_MODEL), 0.02),
        'g_norm1': 1.0 + nrm(ks[6], (DEPTH, D_MODEL), 0.05),
        'w_in': nrm(ks[7], (DEPTH, D_MODEL, IN_COLS), D_MODEL ** -0.5),
        'lru_conv_w': nrm(ks[8], (DEPTH, LRU_CONV_K, LRU_W), LRU_CONV_K ** -0.5),
        'lru_conv_b': nrm(ks[9], (DEPTH, LRU_W), 0.02),
        'lru_wr': nrm(ks[10], (DEPTH, 2, LRU_HEADS, LRU_HD, LRU_HD), LRU_HD ** -0.5),
        'lru_br': nrm(ks[11], (DEPTH, 2, LRU_W), 0.02),
        'lru_wi': nrm(ks[12], (DEPTH, 2, LRU_HEADS, LRU_HD, LRU_HD), LRU_HD ** -0.5),
        'lru_bi': nrm(ks[13], (DEPTH, 2, LRU_W), 0.02),
        'lru_lam': lam,
        'sc_conv_w': nrm(ks[15], (DEPTH, SHORT_CONV_K, CONV_W), SHORT_CONV_K ** -0.5),
        'g_out': 1.0 + nrm(ks[16], (DEPTH, MIX_W), 0.05),
        'w_out': nrm(ks[17], (DEPTH, MIX_W, D_MODEL), MIX_W ** -0.5),
        'g_norm2': 1.0 + nrm(ks[18], (DEPTH, D_MODEL), 0.05),
        'w_router': nrm(ks[19], (DEPTH, D_MODEL, N_EXPERTS), D_MODEL ** -0.5),
        'w_gate_e': nrm(ks[20], (DEPTH, N_EXPERTS, D_MODEL, EXPERT_FF), D_MODEL ** -0.5),
        'w_up_e': nrm(ks[21], (DEPTH, N_EXPERTS, D_MODEL, EXPERT_FF), D_MODEL ** -0.5),
        'w_down_e': nrm(ks[22], (DEPTH, N_EXPERTS, EXPERT_FF, D_MODEL), EXPERT_FF ** -0.5),
        'g_final': 1.0 + nrm(ks[23], (D_MODEL,), 0.05),
    }


def reference(x, c, ctx, c_ctx, w_ada, b_ada, g_norm1, w_in, lru_conv_w, lru_conv_b, lru_wr, lru_br,
              lru_wi, lru_bi, lru_lam, sc_conv_w, g_out, w_out, g_norm2, w_router, w_gate_e, w_up_e,
              w_down_e, g_final):
    b = x.shape[0]
    for l in range(DEPTH):
        last = l == DEPTH - 1
        mx = jnp.split((jax.nn.silu(c) @ w_ada[l] + b_ada[l])[:, None, :], 6, axis=-1)
        mc = jnp.split(jax.nn.silu(c_ctx) @ w_ada[l] + b_ada[l], 6, axis=-1)
        lru_p = (lru_wr[l], lru_br[l], lru_wi[l], lru_bi[l], lru_lam[l])

        hc = modulate(rmsnorm(ctx, g_norm1[l]), mc[0], mc[1])
        pc = hc @ (w_in[l][:, :LRU_W] if last else w_in[l])
        uc = conv_seq(pc[..., :LRU_W], lru_conv_w[l], LRU_CONV_LEFT) + lru_conv_b[l]
        h0 = jnp.zeros((b, LRU_W), jnp.float32)
        hf_c, hb_c = lru_states(uc, *lru_p, h0, h0)
        state_f = hf_c[:, -1]
        state_b = hb_c[:, 0]

        hx = modulate(rmsnorm(x, g_norm1[l]), mx[0], mx[1])
        px = hx @ w_in[l]
        ux = conv_grid(px[..., :LRU_W], lru_conv_w[l], LRU_CONV_LEFT) + lru_conv_b[l]
        hf_x, hb_x = lru_states(ux, *lru_p, state_f, state_b)
        x = x + mx[2] * mixer_out(px, hf_x, hb_x, conv_grid, sc_conv_w[l], g_out[l], w_out[l])

        if not last:
            ctx = ctx + mc[2] * mixer_out(pc, hf_c, hb_c, conv_seq, sc_conv_w[l], g_out[l], w_out[l])
            hc2 = modulate(rmsnorm(ctx, g_norm2[l]), mc[3], mc[4])
            ctx = ctx + mc[5] * expert_choice_ffn(hc2, w_router[l], w_gate_e[l], w_up_e[l], w_down_e[l])

        hx2 = modulate(rmsnorm(x, g_norm2[l]), mx[3], mx[4])
        x = x + mx[5] * expert_choice_ffn(hx2, w_router[l], w_gate_e[l], w_up_e[l], w_down_e[l])
    return rmsnorm(x, g_final)
```

```python
import functools
import math

import numpy as np
import jax
import jax.numpy as jnp
from jax import lax
from jax.experimental import pallas as pl
from jax.experimental.pallas import tpu as pltpu

F32 = jnp.float32
BF16 = jnp.bfloat16
HI = lax.Precision.HIGHEST

GRID_W = 64
LRU_HEADS = 8
LRU_C = 8.0
N_EXPERTS = 16
CAPACITY_FACTOR = 2
EPS = 1e-6
FFT_GROUPS = 4

LANES = 128
SUBLANES = 8
GATE_GROUP = 256
VMEM_LIMIT = 56 * 1024 * 1024


def _cparams(sem):
    return pltpu.CompilerParams(dimension_semantics=sem, vmem_limit_bytes=VMEM_LIMIT)


def _rms(x, g):
    return x * lax.rsqrt(jnp.mean(x * x, axis=-1, keepdims=True) + EPS) * g


def _dot(a, b):
    return jnp.dot(a, b, preferred_element_type=F32)


def _dot_hi(a, b):
    return jnp.dot(a, b, preferred_element_type=F32, precision=HI)


def _dot_nt(a, b, precision=None):
    return lax.dot_general(a, b, (((1,), (1,)), ((), ())), preferred_element_type=F32, precision=precision)


def _ada_kernel(c_ref, w_ref, b_ref, o_ref):
    c = c_ref[...]
    o_ref[...] = _dot_hi(c * jax.nn.sigmoid(c), w_ref[...]) + b_ref[...]


def _ada_call(cs, w_ada, b_ada):
    depth, d, six_d = w_ada.shape
    nblk = six_d // d
    return pl.pallas_call(
        _ada_kernel,
        grid=(depth, nblk),
        in_specs=[
            pl.BlockSpec((SUBLANES, d), lambda l, j: (0, 0)),
            pl.BlockSpec((None, d, d), lambda l, j: (l, 0, j)),
            pl.BlockSpec((None, 1, d), lambda l, j: (l, 0, j)),
        ],
        out_specs=pl.BlockSpec((None, SUBLANES, d), lambda l, j: (l, 0, j)),
        out_shape=jax.ShapeDtypeStruct((depth, SUBLANES, six_d), F32),
        compiler_params=_cparams(("parallel", "parallel")),
        name="ada_mod",
    )(cs, w_ada, b_ada.reshape(depth, 1, six_d))


def _shift_rows(u, d, pos, row_w):
    n = u.shape[0]
    if d == 0:
        return u
    rolled = pltpu.roll(u, (-d) % n, axis=0)
    valid = (pos + d >= 0) & (pos + d < row_w)
    return jnp.where(valid, rolled, 0.0)


def _conv_rows(u, w_ref, left, pos, row_w):
    out = None
    for k in range(w_ref.shape[0]):
        term = w_ref[k:k + 1, :] * _shift_rows(u, k - left, pos, row_w)
        out = term if out is None else out + term
    return out


def _inproj_kernel(x_ref, sh_ref, sc_ref, g1_ref, win_ref, cw_ref, cb_ref, wg_ref, bg_ref, lam_ref,
                   scw_ref, gsc_ref, dft_ref,
                   af_ref, bf_ref, ab_ref, bb_ref, gl_ref, ysc_ref, ucs_ref, *, row_w, lru_w, conv_w):
    x = x_ref[0]
    tm = x.shape[0]
    h = _rms(x, g1_ref[...]) * (1.0 + sc_ref[0]) + sh_ref[0]
    p = _dot(h.astype(BF16), win_ref[...])
    pos = lax.broadcasted_iota(jnp.int32, (tm, 1), 0) % row_w

    u = _conv_rows(p[:, :lru_w], cw_ref, 1, pos, row_w) + cb_ref[...]
    lam = lam_ref[...]
    nl = -lam
    softplus = jnp.maximum(nl, 0.0) + jnp.log(1.0 + jnp.exp(-jnp.abs(nl)))
    out_refs = ((af_ref, bf_ref), (ab_ref, bb_ref))
    for g in range(lru_w // GATE_GROUP):
        cs = slice(g * GATE_GROUP, (g + 1) * GATE_GROUP)
        ug = u[:, cs]
        z = _dot(ug.astype(BF16), wg_ref[g]) + bg_ref[g]
        for d in range(2):
            r = jax.nn.sigmoid(z[:, (2 * d) * GATE_GROUP:(2 * d + 1) * GATE_GROUP])
            i = jax.nn.sigmoid(z[:, (2 * d + 1) * GATE_GROUP:(2 * d + 2) * GATE_GROUP])
            sp = softplus[:, d * lru_w + g * GATE_GROUP: d * lru_w + (g + 1) * GATE_GROUP]
            a = jnp.exp(-LRU_C * r * sp)
            out_refs[d][0][0, :, cs] = a
            out_refs[d][1][0, :, cs] = jnp.sqrt(1.0 - a * a) * (i * ug)

    gl_ref[0] = jax.nn.gelu(p[:, lru_w:2 * lru_w], approximate=True)

    o = 2 * lru_w
    sc_b = p[:, o:o + conv_w]
    sc_c = p[:, o + conv_w:o + 2 * conv_w]
    sc_x = p[:, o + 2 * conv_w:o + 3 * conv_w]
    y_sc = sc_b * _conv_rows(sc_c * sc_x, scw_ref, 1, pos, row_w)
    ysc_ref[0] = _rms(y_sc, gsc_ref[...])

    ucs_ref[0] = _dot_hi(p[:, o + 3 * conv_w:], dft_ref[...])


def _inproj_call(x, shift, scale, g1, w_in, cw, cb, wg, bg, lam, scw, gsc, dftc, *, row_w, tm):
    bt, n, d = x.shape
    in_cols = w_in.shape[1]
    lru_w = cw.shape[1]
    conv_w = scw.shape[1]
    fft_w = in_cols - 2 * lru_w - 3 * conv_w
    ng = lru_w // GATE_GROUP
    tok = lambda w: pl.BlockSpec((1, tm, w), lambda b, i: (b, i, 0))
    vec = lambda w: pl.BlockSpec((1, 1, w), lambda b, i: (b, 0, 0))
    full = lambda *s: pl.BlockSpec(s, lambda b, i: (0,) * len(s))
    shp = lambda w: jax.ShapeDtypeStruct((bt, n, w), F32)
    return pl.pallas_call(
        functools.partial(_inproj_kernel, row_w=row_w, lru_w=lru_w, conv_w=conv_w),
        grid=(bt, n // tm),
        in_specs=[tok(d), vec(d), vec(d), full(1, d), full(d, in_cols), full(*cw.shape), full(1, lru_w),
                  full(ng, GATE_GROUP, 4 * GATE_GROUP), full(ng, 1, 4 * GATE_GROUP), full(1, 2 * lru_w),
                  full(*scw.shape), full(1, conv_w), full(fft_w, 2 * fft_w)],
        out_specs=[tok(lru_w)] * 5 + [tok(conv_w), tok(2 * fft_w)],
        out_shape=[shp(lru_w)] * 5 + [shp(conv_w), shp(2 * fft_w)],
        compiler_params=_cparams(("parallel", "parallel")),
        name="inproj_local",
    )(x, shift, scale, g1, w_in, cw, cb, wg, bg, lam, scw, gsc, dftc)


def _scan_kernel(*refs, reverse, add_other):
    if add_other:
        a_ref, b_ref, h0_ref, other_ref, h_ref, hl_ref, carry_ref = refs
    else:
        a_ref, b_ref, h0_ref, h_ref, hl_ref, carry_ref = refs
        other_ref = None

    @pl.when(pl.program_id(1) == 0)
    def _():
        carry_ref[...] = jnp.broadcast_to(h0_ref[0], carry_ref.shape)

    a = a_ref[0]
    b = b_ref[0]
    tb = a.shape[0]
    row = lax.broadcasted_iota(jnp.int32, (tb, 1), 0) % SUBLANES
    for s in (1, 2, 4):
        if reverse:
            a_sh = pltpu.roll(a, tb - s, axis=0)
            b_sh = pltpu.roll(b, tb - s, axis=0)
            valid = row < SUBLANES - s
        else:
            a_sh = pltpu.roll(a, s, axis=0)
            b_sh = pltpu.roll(b, s, axis=0)
            valid = row >= s
        b = jnp.where(valid, b + a * b_sh, b)
        a = jnp.where(valid, a * a_sh, a)
    carry = carry_ref[...]
    ntile = tb // SUBLANES
    order = range(ntile - 1, -1, -1) if reverse else range(ntile)
    edge = 0 if reverse else SUBLANES - 1
    for t in order:
        rs = slice(t * SUBLANES, (t + 1) * SUBLANES)
        h = b[rs] + a[rs] * carry
        if other_ref is not None:
            h_ref[0, rs, :] = h + other_ref[0, rs, :]
        else:
            h_ref[0, rs, :] = h
        carry = jnp.broadcast_to(h[edge:edge + 1], h.shape)
    carry_ref[...] = carry
    hl_ref[0] = carry[0:1]


def _scan_call(a, b, h0, other, *, reverse, tb):
    bt, n, w = a.shape
    nt = n // tb
    blk = (lambda bi, i: (bi, nt - 1 - i, 0)) if reverse else (lambda bi, i: (bi, i, 0))
    tok = pl.BlockSpec((1, tb, w), blk)
    st = pl.BlockSpec((1, 1, w), lambda bi, i: (bi, 0, 0))
    ins = [a, b, h0] + ([other] if other is not None else [])
    in_specs = [tok, tok, st] + ([tok] if other is not None else [])
    return pl.pallas_call(
        functools.partial(_scan_kernel, reverse=reverse, add_other=other is not None),
        grid=(bt, nt),
        in_specs=in_specs,
        out_specs=[tok, st],
        out_shape=[jax.ShapeDtypeStruct((bt, n, w), F32), jax.ShapeDtypeStruct((bt, 1, w), F32)],
        scratch_shapes=[pltpu.VMEM((SUBLANES, w), F32)],
        compiler_params=_cparams(("parallel", "arbitrary")),
        name="lru_scan_bwd" if reverse else "lru_scan_fwd",
    )(*ins)


def _cos_sin(n):
    k = np.arange(n, dtype=np.float64)
    ang = 2.0 * np.pi * np.outer(k, k) / n
    return np.cos(ang), np.sin(ang)


def _dft1_kernel(x_ref, m_ref, tc_ref, ts_ref, o_ref, *, n2, jn, fw):
    r = _dot_hi(m_ref[...], x_ref[0])
    for j in range(jn):
        base = j * 2 * fw
        c_uc = r[:n2, base:base + fw]
        c_us = r[:n2, base + fw:base + 2 * fw]
        s_uc = r[n2:, base:base + fw]
        s_us = r[n2:, base + fw:base + 2 * fw]
        br = c_uc - s_us
        bi = -(c_us + s_uc)
        tc = tc_ref[0, :, j:j + 1]
        ts = ts_ref[0, :, j:j + 1]
        o_ref[0, j, :, :fw] = br * tc + bi * ts
        o_ref[0, j, :, fw:] = bi * tc - br * ts


def _dft2_kernel(p_ref, m_ref, g_ref, o_ref, *, n1, kn, fw):
    r = _dot_hi(m_ref[...], p_ref[0])
    for k in range(kn):
        base = k * 2 * fw
        y = r[:n1, base:base + fw] + r[n1:, base + fw:base + 2 * fw]
        o_ref[0, :, k * fw:(k + 1) * fw] = _rms(y, g_ref[...])


def _dft_direct_kernel(x_ref, m_ref, g_ref, o_ref, *, n, fw):
    x = x_ref[0]
    y = _dot_hi(m_ref[:n], x[:, :fw]) - _dot_hi(m_ref[n:], x[:, fw:])
    o_ref[0] = _rms(y, g_ref[...])


def _fourier_call(ucs, g_fft):
    bt, n, fw2 = ucs.shape
    fw = fw2 // 2
    gd = fw // FFT_GROUPS
    scale = 1.0 / math.sqrt(n * gd)
    if n <= 512:
        c, s = _cos_sin(n)
        m = jnp.asarray(np.concatenate([c, s], 0) * scale, F32)
        return pl.pallas_call(
            functools.partial(_dft_direct_kernel, n=n, fw=fw),
            grid=(bt,),
            in_specs=[pl.BlockSpec((1, n, fw2), lambda b: (b, 0, 0)),
                      pl.BlockSpec((2 * n, n), lambda b: (0, 0)),
                      pl.BlockSpec((1, fw), lambda b: (0, 0))],
            out_specs=pl.BlockSpec((1, n, fw), lambda b: (b, 0, 0)),
            out_shape=jax.ShapeDtypeStruct((bt, n, fw), F32),
            compiler_params=_cparams(("parallel",)),
            name="dft_direct",
        )(ucs, m, g_fft)

    n1 = LANES
    n2 = n // n1
    jn = 16
    c2, s2 = _cos_sin(n2)
    m1 = jnp.asarray(np.concatenate([c2, s2], 0), F32)
    ang = 2.0 * np.pi * np.outer(np.arange(n2), np.arange(n1)) / n
    tw = lambda f: jnp.asarray(f(ang).reshape(n2, n1 // jn, jn).transpose(1, 0, 2), F32)
    p = pl.pallas_call(
        functools.partial(_dft1_kernel, n2=n2, jn=jn, fw=fw),
        grid=(bt, n1 // jn),
        in_specs=[pl.BlockSpec((1, n2, jn * fw2), lambda b, i: (b, 0, i)),
                  pl.BlockSpec((2 * n2, n2), lambda b, i: (0, 0)),
                  pl.BlockSpec((1, n2, jn), lambda b, i: (i, 0, 0)),
                  pl.BlockSpec((1, n2, jn), lambda b, i: (i, 0, 0))],
        out_specs=pl.BlockSpec((1, jn, n2, fw2), lambda b, i: (b, i, 0, 0)),
        out_shape=jax.ShapeDtypeStruct((bt, n1, n2, fw2), F32),
        compiler_params=_cparams(("parallel", "parallel")),
        name="dft_stage1",
    )(ucs.reshape(bt, n2, n1 * fw2), m1, tw(np.cos), tw(np.sin))

    kn = min(8, n2)
    c1, s1 = _cos_sin(n1)
    m2 = jnp.asarray(np.concatenate([c1, s1], 0) * scale, F32)
    y = pl.pallas_call(
        functools.partial(_dft2_kernel, n1=n1, kn=kn, fw=fw),
        grid=(bt, n2 // kn),
        in_specs=[pl.BlockSpec((1, n1, kn * fw2), lambda b, i: (b, 0, i)),
                  pl.BlockSpec((2 * n1, n1), lambda b, i: (0, 0)),
                  pl.BlockSpec((1, fw), lambda b, i: (0, 0))],
        out_specs=pl.BlockSpec((1, n1, kn * fw), lambda b, i: (b, 0, i)),
        out_shape=jax.ShapeDtypeStruct((bt, n1, n2 * fw), F32),
        compiler_params=_cparams(("parallel", "parallel")),
        name="dft_stage2",
    )(p.reshape(bt, n1, n2 * fw2), m2, g_fft)
    return y.reshape(bt, n, fw)


def _outproj_kernel(hs_ref, gl_ref, ysc_ref, yfft_ref, x_ref, wout_ref, glru_ref, gate_ref,
                    sh_ref, sc_ref, g2_ref, wr_ref, xn_ref, h2_ref, aff_ref):
    y_lru = _rms(hs_ref[0] * gl_ref[0], glru_ref[...])
    y = jnp.concatenate([y_lru.astype(BF16), ysc_ref[0].astype(BF16), yfft_ref[0].astype(BF16)], axis=-1)
    xn = x_ref[0] + gate_ref[0] * _dot(y, wout_ref[...])
    xn_ref[0] = xn
    h2 = _rms(xn, g2_ref[...]) * (1.0 + sc_ref[0]) + sh_ref[0]
    h2_ref[0] = h2
    logits = _dot_nt(wr_ref[...], h2, precision=HI)
    m = jnp.max(logits, axis=0, keepdims=True)
    e = jnp.exp(logits - m)
    aff_ref[0] = e / jnp.sum(e, axis=0, keepdims=True)


def _outproj_call(hs, gl, ysc, yfft, x, w_out, g_lru, gate, shift, scale, g2, w_router_t, *, tm):
    bt, n, d = x.shape
    ne = w_router_t.shape[0]
    tok = lambda w: pl.BlockSpec((1, tm, w), lambda b, i: (b, i, 0))
    vec = lambda w: pl.BlockSpec((1, 1, w), lambda b, i: (b, 0, 0))
    full = lambda *s: pl.BlockSpec(s, lambda b, i: (0,) * len(s))
    return pl.pallas_call(
        _outproj_kernel,
        grid=(bt, n // tm),
        in_specs=[tok(hs.shape[2]), tok(gl.shape[2]), tok(ysc.shape[2]), tok(yfft.shape[2]), tok(d),
                  full(*w_out.shape), full(1, hs.shape[2]), vec(d), vec(d), vec(d), full(1, d), full(ne, d)],
        out_specs=[tok(d), tok(d), pl.BlockSpec((1, ne, tm), lambda b, i: (b, 0, i))],
        out_shape=[jax.ShapeDtypeStruct((bt, n, d), F32), jax.ShapeDtypeStruct((bt, n, d), F32),
                   jax.ShapeDtypeStruct((bt, ne, n), F32)],
        compiler_params=_cparams(("parallel", "parallel")),
        name="outproj_router",
    )(hs, gl, ysc, yfft, x, w_out, g_lru, gate, shift, scale, g2, w_router_t)


def _select_kernel(aff_ref, idx_ref, g_ref, *, cap):
    aff = aff_ref[0]
    ne, nb, _ = aff.shape

    def count(mask):
        c = jnp.sum(mask.astype(jnp.int32), axis=2, keepdims=True)
        return jnp.sum(c, axis=1, keepdims=True)

    def bit_step(i, t):
        cand = t | (jnp.int32(1) << (30 - i))
        return jnp.where(count(aff >= lax.bitcast_convert_type(cand, F32)) >= cap, cand, t)

    bits = lax.fori_loop(0, 31, bit_step, jnp.zeros((ne, 1, 1), jnp.int32))
    lo = lax.bitcast_convert_type(bits, F32)
    hi = lax.bitcast_convert_type(bits + 1, F32)

    def mid_step(i, lh):
        lo, hi = lh
        mid = (lo + hi) * 0.5
        ok = count(aff >= mid) >= cap
        return jnp.where(ok, mid, lo), jnp.where(ok, hi, mid)

    lo, hi = lax.fori_loop(0, 30, mid_step, (lo, hi))
    gt = aff >= hi
    eq = (aff >= lo) & (aff < hi)
    need = (cap - count(gt)).astype(F32)

    lane = lax.broadcasted_iota(jnp.int32, (LANES, LANES), 0)
    lane_t = lax.broadcasted_iota(jnp.int32, (LANES, LANES), 1)
    tri_incl = (lane <= lane_t).astype(BF16)
    blk = lax.broadcasted_iota(jnp.int32, (nb, nb), 0)
    blk_t = lax.broadcasted_iota(jnp.int32, (nb, nb), 1)
    tri_blk = (blk <= blk_t).astype(BF16)
    ones_row = jnp.ones((SUBLANES, LANES), BF16)
    kvals = lax.broadcasted_iota(jnp.int32, (SUBLANES, nb), 1).astype(BF16)
    slot = lax.broadcasted_iota(jnp.int32, (cap, 1), 0).astype(F32)

    def block_prefix(mask_bf16):
        lc = _dot(mask_bf16, tri_incl)
        cnt_row = _dot_nt(ones_row, mask_bf16)
        inc_row = _dot(cnt_row.astype(BF16), tri_blk)
        return lc, inc_row - cnt_row, inc_row

    for e in range(ne):
        eq_e = eq[e].astype(BF16)
        lc_eq, off_eq, _ = block_prefix(eq_e)
        tri_strict = (blk_t < blk).astype(BF16)
        cnt_col = lc_eq[:, LANES - 1:LANES]
        before = _dot(tri_strict, jnp.broadcast_to(cnt_col, (nb, LANES)).astype(BF16))[:, :1]
        rank = lc_eq - eq[e].astype(F32) + before
        sel = gt[e] | (eq[e] & (rank < need[e]))
        sel_bf = sel.astype(BF16)

        lc, off_row, inc_row = block_prefix(sel_bf)
        off1 = off_row[0:1]
        inc1 = inc_row[0:1]
        onehot = ((slot >= off1) & (slot < inc1))
        oh_bf = onehot.astype(BF16)
        offk = jnp.sum(jnp.where(onehot, off1, 0.0), axis=1, keepdims=True)
        jl = slot - offk
        m = _dot(oh_bf, lc.astype(BF16))
        below = (m <= jl)
        kb_row = _dot_nt(kvals, oh_bf)
        r_row = _dot_nt(ones_row, below.astype(BF16))
        idx_ref[0, e:e + 1, :] = (kb_row[0:1] * float(LANES) + r_row[0:1]).astype(jnp.int32)

        msel = _dot(oh_bf, sel_bf)
        hit = (m == jl + 1.0) & (msel > 0.5)
        aff_rows = _dot_hi(onehot.astype(F32), aff[e])
        g_ref[0, e] = jnp.sum(jnp.where(hit, aff_rows, 0.0), axis=1, keepdims=True)


def _select_call(aff_t, cap):
    bt, ne, n = aff_t.shape
    nb = n // LANES
    return pl.pallas_call(
        functools.partial(_select_kernel, cap=cap),
        grid=(bt,),
        in_specs=[pl.BlockSpec((1, ne, nb, LANES), lambda b: (b, 0, 0, 0))],
        out_specs=[pl.BlockSpec((1, ne, cap), lambda b: (b, 0, 0)),
                   pl.BlockSpec((1, ne, cap, 1), lambda b: (b, 0, 0, 0))],
        out_shape=[jax.ShapeDtypeStruct((bt, ne, cap), jnp.int32),
                   jax.ShapeDtypeStruct((bt, ne, cap, 1), F32)],
        compiler_params=_cparams(("parallel",)),
        name="expert_select",
    )(aff_t.reshape(bt, ne, nb, LANES))


def _ffn_kernel(idx_ref, idxn_ref, h_hbm, g_ref, gate2_ref, wg_ref, wu_ref, wd_ref, o_ref,
                xs_ref, wgb_ref, wub_ref, wdb_ref, sem, *, cap, chunk):
    e = pl.program_id(0)
    b = pl.program_id(1)
    nb = pl.num_programs(1)
    step = e * nb + b
    slot = step % 2

    def gather(ids_ref, bi, dst_slot):
        def start(j, carry):
            pltpu.make_async_copy(h_hbm.at[bi, pl.ds(ids_ref[0, 0, 0, j], 1), :],
                                  xs_ref.at[dst_slot, pl.ds(j, 1), :], sem.at[dst_slot]).start()
            return carry
        lax.fori_loop(0, cap, start, 0, unroll=8)

    @pl.when(step == 0)
    def _():
        gather(idx_ref, b, slot)

    @pl.when(step + 1 < pl.num_programs(0) * nb)
    def _():
        gather(idxn_ref, (b + 1) % nb, 1 - slot)

    @pl.when(b == 0)
    def _():
        wgb_ref[...] = wg_ref[0, 0].astype(BF16)
        wub_ref[...] = wu_ref[0, 0].astype(BF16)
        wdb_ref[...] = wd_ref[0, 0].astype(BF16)

    pltpu.make_async_copy(h_hbm.at[b, pl.ds(0, cap), :], xs_ref.at[slot], sem.at[slot]).wait()

    for c in range(cap // chunk):
        rs = slice(c * chunk, (c + 1) * chunk)
        xb = xs_ref[slot, rs, :].astype(BF16)
        gate = _dot(xb, wgb_ref[...])
        up = _dot(xb, wub_ref[...])
        hid = (gate * jax.nn.sigmoid(gate)) * up
        y = _dot(hid.astype(BF16), wdb_ref[...])
        o_ref[0, 0, rs, :] = (y * g_ref[0, 0, rs, :]) * gate2_ref[0]


def _ffn_call(idx, h2, g, gate2, wg, wu, wd, layer):
    bt, ne, cap = idx.shape
    d = h2.shape[2]
    ff = wg.shape[3]
    chunk = min(cap, 256)
    wspec = lambda s: pl.BlockSpec((1, 1) + s, lambda e, b: (layer, e, 0, 0))
    idx4 = idx.reshape(bt, ne, 1, cap)
    nxt = lambda e, b: ((b + 1) % bt, jnp.minimum(e + (b + 1) // bt, ne - 1), 0, 0)
    return pl.pallas_call(
        functools.partial(_ffn_kernel, cap=cap, chunk=chunk),
        grid=(ne, bt),
        in_specs=[pl.BlockSpec((1, 1, 1, cap), lambda e, b: (b, e, 0, 0), memory_space=pltpu.SMEM),
                  pl.BlockSpec((1, 1, 1, cap), nxt, memory_space=pltpu.SMEM),
                  pl.BlockSpec(memory_space=pl.ANY),
                  pl.BlockSpec((1, 1, cap, 1), lambda e, b: (b, e, 0, 0)),
                  pl.BlockSpec((1, 1, d), lambda e, b: (b, 0, 0)),
                  wspec((d, ff)), wspec((d, ff)), wspec((ff, d))],
        out_specs=pl.BlockSpec((1, 1, cap, d), lambda e, b: (b, e, 0, 0)),
        out_shape=jax.ShapeDtypeStruct((bt, ne, cap, d), F32),
        scratch_shapes=[pltpu.VMEM((2, cap, d), F32), pltpu.VMEM((d, ff), BF16), pltpu.VMEM((d, ff), BF16),
                        pltpu.VMEM((ff, d), BF16), pltpu.SemaphoreType.DMA((2,))],
        compiler_params=_cparams(("arbitrary", "arbitrary")),
        name="expert_ffn",
    )(idx4, idx4, h2, g, gate2, wg, wu, wd)


def _combine_kernel(idx_ref, x_hbm, z_ref, gfin_ref, o_hbm, acc_ref, sem, *, cap, final_norm, norm_chunk):
    b = pl.program_id(0)
    e = pl.program_id(1)

    @pl.when(e == 0)
    def _():
        cp = pltpu.make_async_copy(x_hbm.at[b], acc_ref, sem)
        cp.start()
        cp.wait()

    unroll = 4

    def body(i, carry):
        j0 = i * unroll
        toks = [idx_ref[0, 0, 0, j0 + u] for u in range(unroll)]
        vals = [acc_ref[pl.ds(toks[u], 1), :] + z_ref[0, 0, pl.ds(j0 + u, 1), :] for u in range(unroll)]
        for u in range(unroll):
            acc_ref[pl.ds(toks[u], 1), :] = vals[u]
        return carry

    lax.fori_loop(0, cap // unroll, body, 0)

    @pl.when(e == pl.num_programs(1) - 1)
    def _():
        if final_norm:
            def norm(i, carry):
                rs = pl.ds(pl.multiple_of(i * norm_chunk, norm_chunk), norm_chunk)
                acc_ref[rs, :] = _rms(acc_ref[rs, :], gfin_ref[...])
                return carry
            lax.fori_loop(0, acc_ref.shape[0] // norm_chunk, norm, 0)
        cp = pltpu.make_async_copy(acc_ref, o_hbm.at[b], sem)
        cp.start()
        cp.wait()


def _combine_call(idx, x, z, g_final, *, final_norm):
    bt, ne, cap = idx.shape
    _, n, d = x.shape
    return pl.pallas_call(
        functools.partial(_combine_kernel, cap=cap, final_norm=final_norm, norm_chunk=min(n, 256)),
        grid=(bt, ne),
        in_specs=[pl.BlockSpec((1, 1, 1, cap), lambda b, e: (b, e, 0, 0), memory_space=pltpu.SMEM),
                  pl.BlockSpec(memory_space=pl.ANY),
                  pl.BlockSpec((1, 1, cap, d), lambda b, e: (b, e, 0, 0)),
                  pl.BlockSpec((1, d), lambda b, e: (0, 0))],
        out_specs=pl.BlockSpec(memory_space=pl.ANY),
        out_shape=jax.ShapeDtypeStruct((bt, n, d), F32),
        scratch_shapes=[pltpu.VMEM((n, d), F32), pltpu.SemaphoreType.DMA(())],
        compiler_params=_cparams(("arbitrary", "arbitrary")),
        name="expert_combine",
    )(idx.reshape(bt, ne, 1, cap), x, z, g_final)


def _block_diag(w):
    h, hd, _ = w.shape
    eye = jnp.eye(h, dtype=w.dtype)
    return (w[:, :, None, :] * eye[:, None, :, None]).reshape(h * hd, h * hd)


def _gate_weights(wr, br, wi, bi):
    dense = [_block_diag(wr[0]), _block_diag(wi[0]), _block_diag(wr[1]), _block_diag(wi[1])]
    bias = [br[0], bi[0], br[1], bi[1]]
    lru_w = dense[0].shape[0]
    wg, bg = [], []
    for g in range(lru_w // GATE_GROUP):
        cs = slice(g * GATE_GROUP, (g + 1) * GATE_GROUP)
        wg.append(jnp.concatenate([m[cs, cs] for m in dense], axis=1))
        bg.append(jnp.concatenate([v[cs] for v in bias])[None, :])
    return jnp.stack(wg).astype(BF16), jnp.stack(bg)


def _channel_dft(fw):
    gd = fw // FFT_GROUPS
    c, s = _cos_sin(gd)
    eye = np.eye(FFT_GROUPS)
    return jnp.asarray(np.concatenate([np.kron(eye, c), np.kron(eye, s)], axis=1), F32)


def _mixer(x, lp, shift, scale, h0_f, h0_b, *, row_w, tm, tb, need_out):
    af, bf, ab, bb, gl, ysc, ucs = _inproj_call(
        x, shift, scale, lp["g1"], lp["w_in"], lp["cw"], lp["cb"], lp["wg"], lp["bg"], lp["lam"],
        lp["scw"], lp["g_sc"], lp["dftc"], row_w=row_w, tm=tm)
    hf, sf = _scan_call(af, bf, h0_f, None, reverse=False, tb=tb)
    hs, sb = _scan_call(ab, bb, h0_b, hf, reverse=True, tb=tb)
    if not need_out:
        return sf, sb, None
    yfft = _fourier_call(ucs, lp["g_fft"])
    return sf, sb, (hs, gl, ysc, yfft)


def _moe(x, h2, aff_t, gate2, lp, g_final, *, final_norm):
    bt, n, d = x.shape
    cap = CAPACITY_FACTOR * n // N_EXPERTS
    pad = (-n) % (SUBLANES * LANES)
    if pad:
        aff_t = jnp.pad(aff_t, ((0, 0), (0, 0), (0, pad)), constant_values=-1.0)
    idx, g = _select_call(aff_t, cap)
    z = _ffn_call(idx, h2, g, gate2, lp["wge"], lp["wue"], lp["wde"], lp["layer"])
    return _combine_call(idx, x, z, g_final, final_norm=final_norm)


def kernel(x, c, ctx, c_ctx, w_ada, b_ada, g_norm1, w_in, lru_conv_w, lru_conv_b, lru_wr, lru_br, lru_wi,
           lru_bi, lru_lam, sc_conv_w, g_out, w_out, g_norm2, w_router, w_gate_e, w_up_e, w_down_e, g_final):
    depth = w_ada.shape[0]
    bsz, seq, d = x.shape
    ctx_len = ctx.shape[1]
    lru_w = lru_conv_w.shape[2]
    conv_w = sc_conv_w.shape[2]
    fft_w = w_in.shape[2] - 2 * lru_w - 3 * conv_w

    cs = jnp.concatenate([c, c_ctx[None, :], jnp.zeros((SUBLANES - bsz - 1, d), F32)], axis=0)
    mods = _ada_call(cs, w_ada, b_ada)
    dftc = _channel_dft(fft_w)
    gfin = g_final[None, :]
    zero_state = jnp.zeros((bsz, 1, lru_w), F32)

    for l in range(depth):
        last = l == depth - 1
        mx = [mods[l, :bsz, None, k * d:(k + 1) * d] for k in range(6)]
        mc = [jnp.broadcast_to(mods[l, bsz, k * d:(k + 1) * d], (bsz, 1, d)) for k in range(6)]
        wg, bg = _gate_weights(lru_wr[l], lru_br[l], lru_wi[l], lru_bi[l])
        lp = dict(
            g1=g_norm1[l][None, :], w_in=w_in[l].astype(BF16), cw=lru_conv_w[l], cb=lru_conv_b[l][None, :],
            wg=wg, bg=bg, lam=lru_lam[l].reshape(1, 2 * lru_w), scw=sc_conv_w[l],
            g_sc=g_out[l][None, lru_w:lru_w + conv_w], g_fft=g_out[l][None, lru_w + conv_w:], dftc=dftc,
            wge=w_gate_e, wue=w_up_e, wde=w_down_e, layer=l)
        g_lru = g_out[l][None, :lru_w]
        w_out_l = w_out[l].astype(BF16)
        w_router_t = w_router[l].T

        sf, sb, parts = _mixer(ctx, lp, mc[0], mc[1], zero_state, zero_state,
                               row_w=ctx_len, tm=ctx_len, tb=ctx_len, need_out=not last)
        if not last:
            ctx, hc2, aff_c = _outproj_call(*parts, ctx, w_out_l, g_lru, mc[2], mc[3], mc[4],
                                            g_norm2[l][None, :], w_router_t, tm=ctx_len)
            ctx = _moe(ctx, hc2, aff_c, mc[5], lp, gfin, final_norm=False)

        _, _, parts = _mixer(x, lp, mx[0], mx[1], sf, sb, row_w=GRID_W, tm=512, tb=512, need_out=True)
        x, hx2, aff_x = _outproj_call(*parts, x, w_out_l, g_lru, mx[2], mx[3], mx[4],
                                      g_norm2[l][None, :], w_router_t, tm=512)
        x = _moe(x, hx2, aff_x, mx[5], lp, gfin, final_norm=last)
    return x
```

```python
import functools
import math

import numpy as np
import jax
import jax.numpy as jnp
from jax import lax
from jax.experimental import pallas as pl
from jax.experimental.pallas import tpu as pltpu

F32 = jnp.float32
BF16 = jnp.bfloat16
HI = lax.Precision.HIGHEST

GRID_W = 64
LRU_HEADS = 8
LRU_C = 8.0
N_EXPERTS = 16
CAPACITY_FACTOR = 2
EPS = 1e-6
FFT_GROUPS = 4

LANES = 128
SUBLANES = 8
GATE_GROUP = 256
VMEM_LIMIT = 56 * 1024 * 1024


def _cparams(sem):
    return pltpu.CompilerParams(dimension_semantics=sem, vmem_limit_bytes=VMEM_LIMIT)


def _rms(x, g):
    return x * lax.rsqrt(jnp.mean(x * x, axis=-1, keepdims=True) + EPS) * g


def _dot(a, b):
    return jnp.dot(a, b, preferred_element_type=F32)


def _dot_hi(a, b):
    return jnp.dot(a, b, preferred_element_type=F32, precision=HI)


def _split_const(m):
    m = jnp.asarray(m, F32)
    hi = m.astype(BF16)
    return jnp.stack([hi, (m - hi.astype(F32)).astype(BF16)])


def _split(x):
    hi = x.astype(BF16)
    return hi, (x - hi.astype(F32)).astype(BF16)


def _dot3_const_lhs(m_ref, x):
    x_hi, x_lo = _split(x)
    return _dot(m_ref[0], x_hi) + (_dot(m_ref[1], x_hi) + _dot(m_ref[0], x_lo))


def _dot3_const_rhs(x, m_ref):
    x_hi, x_lo = _split(x)
    return _dot(x_hi, m_ref[0]) + (_dot(x_hi, m_ref[1]) + _dot(x_lo, m_ref[0]))


def _dot_nt(a, b, precision=None):
    return lax.dot_general(a, b, (((1,), (1,)), ((), ())), preferred_element_type=F32, precision=precision)


def _ada_kernel(c_ref, w_ref, b_ref, o_ref):
    c = c_ref[...]
    o_ref[...] = _dot_hi(c * jax.nn.sigmoid(c), w_ref[...]) + b_ref[...]


def _ada_call(cs, w_ada, b_ada):
    depth, d, six_d = w_ada.shape
    nblk = six_d // d
    return pl.pallas_call(
        _ada_kernel,
        grid=(depth, nblk),
        in_specs=[
            pl.BlockSpec((SUBLANES, d), lambda l, j: (0, 0)),
            pl.BlockSpec((None, d, d), lambda l, j: (l, 0, j)),
            pl.BlockSpec((None, 1, d), lambda l, j: (l, 0, j)),
        ],
        out_specs=pl.BlockSpec((None, SUBLANES, d), lambda l, j: (l, 0, j)),
        out_shape=jax.ShapeDtypeStruct((depth, SUBLANES, six_d), F32),
        compiler_params=_cparams(("parallel", "parallel")),
        name="ada_mod",
    )(cs, w_ada, b_ada.reshape(depth, 1, six_d))


def _shift_rows(u, d, pos, row_w):
    n = u.shape[0]
    if d == 0:
        return u
    rolled = pltpu.roll(u, (-d) % n, axis=0)
    valid = (pos + d >= 0) & (pos + d < row_w)
    return jnp.where(valid, rolled, 0.0)


def _conv_rows(u, w_ref, left, pos, row_w):
    out = None
    for k in range(w_ref.shape[0]):
        term = w_ref[k:k + 1, :] * _shift_rows(u, k - left, pos, row_w)
        out = term if out is None else out + term
    return out


def _inproj_kernel(x_ref, sh_ref, sc_ref, g1_ref, win_ref, cw_ref, cb_ref, wg_ref, bg_ref, lam_ref,
                   scw_ref, gsc_ref, dft_ref,
                   af_ref, bf_ref, ab_ref, bb_ref, gl_ref, ysc_ref, ucs_ref, *, row_w, lru_w, conv_w):
    x = x_ref[0]
    tm = x.shape[0]
    h = _rms(x, g1_ref[...]) * (1.0 + sc_ref[0]) + sh_ref[0]
    p = _dot(h.astype(BF16), win_ref[...])
    pos = lax.broadcasted_iota(jnp.int32, (tm, 1), 0) % row_w

    u = _conv_rows(p[:, :lru_w], cw_ref, 1, pos, row_w) + cb_ref[...]
    lam = lam_ref[...]
    nl = -lam
    softplus = jnp.maximum(nl, 0.0) + jnp.log(1.0 + jnp.exp(-jnp.abs(nl)))
    out_refs = ((af_ref, bf_ref), (ab_ref, bb_ref))
    for g in range(lru_w // GATE_GROUP):
        cs = slice(g * GATE_GROUP, (g + 1) * GATE_GROUP)
        ug = u[:, cs]
        z = _dot(ug.astype(BF16), wg_ref[g]) + bg_ref[g]
        for d in range(2):
            r = jax.nn.sigmoid(z[:, (2 * d) * GATE_GROUP:(2 * d + 1) * GATE_GROUP])
            i = jax.nn.sigmoid(z[:, (2 * d + 1) * GATE_GROUP:(2 * d + 2) * GATE_GROUP])
            sp = softplus[:, d * lru_w + g * GATE_GROUP: d * lru_w + (g + 1) * GATE_GROUP]
            a = jnp.exp(-LRU_C * r * sp)
            out_refs[d][0][0, :, cs] = a
            out_refs[d][1][0, :, cs] = jnp.sqrt(1.0 - a * a) * (i * ug)

    gl_ref[0] = jax.nn.gelu(p[:, lru_w:2 * lru_w], approximate=True)

    o = 2 * lru_w
    sc_b = p[:, o:o + conv_w]
    sc_c = p[:, o + conv_w:o + 2 * conv_w]
    sc_x = p[:, o + 2 * conv_w:o + 3 * conv_w]
    y_sc = sc_b * _conv_rows(sc_c * sc_x, scw_ref, 1, pos, row_w)
    ysc_ref[0] = _rms(y_sc, gsc_ref[...])

    ucs_ref[0] = _dot3_const_rhs(p[:, o + 3 * conv_w:], dft_ref)


def _inproj_call(x, shift, scale, g1, w_in, cw, cb, wg, bg, lam, scw, gsc, dftc, *, row_w, tm):
    bt, n, d = x.shape
    in_cols = w_in.shape[1]
    lru_w = cw.shape[1]
    conv_w = scw.shape[1]
    fft_w = in_cols - 2 * lru_w - 3 * conv_w
    ng = lru_w // GATE_GROUP
    tok = lambda w: pl.BlockSpec((1, tm, w), lambda b, i: (b, i, 0))
    vec = lambda w: pl.BlockSpec((1, 1, w), lambda b, i: (b, 0, 0))
    full = lambda *s: pl.BlockSpec(s, lambda b, i: (0,) * len(s))
    shp = lambda w: jax.ShapeDtypeStruct((bt, n, w), F32)
    return pl.pallas_call(
        functools.partial(_inproj_kernel, row_w=row_w, lru_w=lru_w, conv_w=conv_w),
        grid=(bt, n // tm),
        in_specs=[tok(d), vec(d), vec(d), full(1, d), full(d, in_cols), full(*cw.shape), full(1, lru_w),
                  full(ng, GATE_GROUP, 4 * GATE_GROUP), full(ng, 1, 4 * GATE_GROUP), full(1, 2 * lru_w),
                  full(*scw.shape), full(1, conv_w), full(2, fft_w, 2 * fft_w)],
        out_specs=[tok(lru_w)] * 5 + [tok(conv_w), tok(2 * fft_w)],
        out_shape=[shp(lru_w)] * 5 + [shp(conv_w), shp(2 * fft_w)],
        compiler_params=_cparams(("parallel", "parallel")),
        name="inproj_local",
    )(x, shift, scale, g1, w_in, cw, cb, wg, bg, lam, scw, gsc, dftc)


def _scan_kernel(*refs, reverse, add_other):
    if add_other:
        a_ref, b_ref, h0_ref, other_ref, h_ref, hl_ref, carry_ref = refs
    else:
        a_ref, b_ref, h0_ref, h_ref, hl_ref, carry_ref = refs
        other_ref = None

    @pl.when(pl.program_id(1) == 0)
    def _():
        carry_ref[...] = jnp.broadcast_to(h0_ref[0], carry_ref.shape)

    tb, w = a_ref.shape[1], a_ref.shape[2]
    ntile = tb // SUBLANES
    a = a_ref[0].reshape(ntile, SUBLANES, w)
    b = b_ref[0].reshape(ntile, SUBLANES, w)
    row = lax.broadcasted_iota(jnp.int32, (1, SUBLANES, 1), 1)
    for s in (1, 2, 4):
        if reverse:
            a_sh = pltpu.roll(a, SUBLANES - s, axis=1)
            b_sh = pltpu.roll(b, SUBLANES - s, axis=1)
            valid = row < SUBLANES - s
        else:
            a_sh = pltpu.roll(a, s, axis=1)
            b_sh = pltpu.roll(b, s, axis=1)
            valid = row >= s
        b = jnp.where(valid, b + a * b_sh, b)
        a = jnp.where(valid, a * a_sh, a)
    carry = carry_ref[...]
    order = range(ntile - 1, -1, -1) if reverse else range(ntile)
    edge = 0 if reverse else SUBLANES - 1
    for t in order:
        rs = slice(t * SUBLANES, (t + 1) * SUBLANES)
        h = b[t] + a[t] * carry
        if other_ref is not None:
            h_ref[0, rs, :] = h + other_ref[0, rs, :]
        else:
            h_ref[0, rs, :] = h
        carry = jnp.broadcast_to(h[edge:edge + 1], h.shape)
    carry_ref[...] = carry
    hl_ref[0] = carry[0:1]


def _scan_call(a, b, h0, other, *, reverse, tb):
    bt, n, w = a.shape
    nt = n // tb
    blk = (lambda bi, i: (bi, nt - 1 - i, 0)) if reverse else (lambda bi, i: (bi, i, 0))
    tok = pl.BlockSpec((1, tb, w), blk)
    st = pl.BlockSpec((1, 1, w), lambda bi, i: (bi, 0, 0))
    ins = [a, b, h0] + ([other] if other is not None else [])
    in_specs = [tok, tok, st] + ([tok] if other is not None else [])
    return pl.pallas_call(
        functools.partial(_scan_kernel, reverse=reverse, add_other=other is not None),
        grid=(bt, nt),
        in_specs=in_specs,
        out_specs=[tok, st],
        out_shape=[jax.ShapeDtypeStruct((bt, n, w), F32), jax.ShapeDtypeStruct((bt, 1, w), F32)],
        scratch_shapes=[pltpu.VMEM((SUBLANES, w), F32)],
        compiler_params=_cparams(("parallel", "arbitrary")),
        name="lru_scan_bwd" if reverse else "lru_scan_fwd",
    )(*ins)


def _cos_sin(n):
    k = np.arange(n, dtype=np.float64)
    ang = 2.0 * np.pi * np.outer(k, k) / n
    return np.cos(ang), np.sin(ang)


def _dft1_kernel(x_ref, m_ref, tc_ref, ts_ref, o_ref, *, n2, jn, fw):
    r = _dot3_const_lhs(m_ref, x_ref[0])
    for j in range(jn):
        base = j * 2 * fw
        c_uc = r[:n2, base:base + fw]
        c_us = r[:n2, base + fw:base + 2 * fw]
        s_uc = r[n2:, base:base + fw]
        s_us = r[n2:, base + fw:base + 2 * fw]
        br = c_uc - s_us
        bi = -(c_us + s_uc)
        tc = tc_ref[0, :, j:j + 1]
        ts = ts_ref[0, :, j:j + 1]
        o_ref[0, j, :, :fw] = br * tc + bi * ts
        o_ref[0, j, :, fw:] = bi * tc - br * ts


def _dft2_kernel(p_ref, m_ref, g_ref, o_ref, *, n1, kn, fw):
    r = _dot3_const_lhs(m_ref, p_ref[0])
    for k in range(kn):
        base = k * 2 * fw
        y = r[:n1, base:base + fw] + r[n1:, base + fw:base + 2 * fw]
        o_ref[0, :, k * fw:(k + 1) * fw] = _rms(y, g_ref[...])


def _dft_direct_kernel(x_ref, m_ref, g_ref, o_ref, *, n, fw):
    x = x_ref[0]
    r = _dot3_const_lhs(m_ref, x)
    y = r[:n, :fw] - r[n:, fw:]
    o_ref[0] = _rms(y, g_ref[...])


def _fourier_call(ucs, g_fft):
    bt, n, fw2 = ucs.shape
    fw = fw2 // 2
    gd = fw // FFT_GROUPS
    scale = 1.0 / math.sqrt(n * gd)
    if n <= 512:
        c, s = _cos_sin(n)
        m = _split_const(np.concatenate([c, s], 0) * scale)
        return pl.pallas_call(
            functools.partial(_dft_direct_kernel, n=n, fw=fw),
            grid=(bt,),
            in_specs=[pl.BlockSpec((1, n, fw2), lambda b: (b, 0, 0)),
                      pl.BlockSpec((2, 2 * n, n), lambda b: (0, 0, 0)),
                      pl.BlockSpec((1, fw), lambda b: (0, 0))],
            out_specs=pl.BlockSpec((1, n, fw), lambda b: (b, 0, 0)),
            out_shape=jax.ShapeDtypeStruct((bt, n, fw), F32),
            compiler_params=_cparams(("parallel",)),
            name="dft_direct",
        )(ucs, m, g_fft)

    n1 = LANES
    n2 = n // n1
    jn = 16
    c2, s2 = _cos_sin(n2)
    m1 = _split_const(np.concatenate([c2, s2], 0))
    ang = 2.0 * np.pi * np.outer(np.arange(n2), np.arange(n1)) / n
    tw = lambda f: jnp.asarray(f(ang).reshape(n2, n1 // jn, jn).transpose(1, 0, 2), F32)
    p = pl.pallas_call(
        functools.partial(_dft1_kernel, n2=n2, jn=jn, fw=fw),
        grid=(bt, n1 // jn),
        in_specs=[pl.BlockSpec((1, n2, jn * fw2), lambda b, i: (b, 0, i)),
                  pl.BlockSpec((2, 2 * n2, n2), lambda b, i: (0, 0, 0)),
                  pl.BlockSpec((1, n2, jn), lambda b, i: (i, 0, 0)),
                  pl.BlockSpec((1, n2, jn), lambda b, i: (i, 0, 0))],
        out_specs=pl.BlockSpec((1, jn, n2, fw2), lambda b, i: (b, i, 0, 0)),
        out_shape=jax.ShapeDtypeStruct((bt, n1, n2, fw2), F32),
        compiler_params=_cparams(("parallel", "parallel")),
        name="dft_stage1",
    )(ucs.reshape(bt, n2, n1 * fw2), m1, tw(np.cos), tw(np.sin))

    kn = min(8, n2)
    c1, s1 = _cos_sin(n1)
    m2 = _split_const(np.concatenate([c1, s1], 0) * scale)
    y = pl.pallas_call(
        functools.partial(_dft2_kernel, n1=n1, kn=kn, fw=fw),
        grid=(bt, n2 // kn),
        in_specs=[pl.BlockSpec((1, n1, kn * fw2), lambda b, i: (b, 0, i)),
                  pl.BlockSpec((2, 2 * n1, n1), lambda b, i: (0, 0, 0)),
                  pl.BlockSpec((1, fw), lambda b, i: (0, 0))],
        out_specs=pl.BlockSpec((1, n1, kn * fw), lambda b, i: (b, 0, i)),
        out_shape=jax.ShapeDtypeStruct((bt, n1, n2 * fw), F32),
        compiler_params=_cparams(("parallel", "parallel")),
        name="dft_stage2",
    )(p.reshape(bt, n1, n2 * fw2), m2, g_fft)
    return y.reshape(bt, n, fw)


def _outproj_kernel(hs_ref, gl_ref, ysc_ref, yfft_ref, x_ref, wout_ref, glru_ref, gate_ref,
                    sh_ref, sc_ref, g2_ref, wr_ref, xn_ref, h2_ref, aff_ref):
    y_lru = _rms(hs_ref[0] * gl_ref[0], glru_ref[...])
    y = jnp.concatenate([y_lru.astype(BF16), ysc_ref[0].astype(BF16), yfft_ref[0].astype(BF16)], axis=-1)
    xn = x_ref[0] + gate_ref[0] * _dot(y, wout_ref[...])
    xn_ref[0] = xn
    h2 = _rms(xn, g2_ref[...]) * (1.0 + sc_ref[0]) + sh_ref[0]
    h2_ref[0] = h2
    logits = _dot_nt(wr_ref[...], h2, precision=HI)
    m = jnp.max(logits, axis=0, keepdims=True)
    e = jnp.exp(logits - m)
    aff_ref[0] = e / jnp.sum(e, axis=0, keepdims=True)


def _outproj_call(hs, gl, ysc, yfft, x, w_out, g_lru, gate, shift, scale, g2, w_router_t, *, tm):
    bt, n, d = x.shape
    ne = w_router_t.shape[0]
    tok = lambda w: pl.BlockSpec((1, tm, w), lambda b, i: (b, i, 0))
    vec = lambda w: pl.BlockSpec((1, 1, w), lambda b, i: (b, 0, 0))
    full = lambda *s: pl.BlockSpec(s, lambda b, i: (0,) * len(s))
    return pl.pallas_call(
        _outproj_kernel,
        grid=(bt, n // tm),
        in_specs=[tok(hs.shape[2]), tok(gl.shape[2]), tok(ysc.shape[2]), tok(yfft.shape[2]), tok(d),
                  full(*w_out.shape), full(1, hs.shape[2]), vec(d), vec(d), vec(d), full(1, d), full(ne, d)],
        out_specs=[tok(d), tok(d), pl.BlockSpec((1, ne, tm), lambda b, i: (b, 0, i))],
        out_shape=[jax.ShapeDtypeStruct((bt, n, d), F32), jax.ShapeDtypeStruct((bt, n, d), F32),
                   jax.ShapeDtypeStruct((bt, ne, n), F32)],
        compiler_params=_cparams(("parallel", "parallel")),
        name="outproj_router",
    )(hs, gl, ysc, yfft, x, w_out, g_lru, gate, shift, scale, g2, w_router_t)


def _select_kernel(aff_ref, idx_ref, g_ref, *, cap):
    aff = aff_ref[0]
    ne, nb, _ = aff.shape

    def count(mask):
        c = jnp.sum(mask.astype(jnp.int32), axis=1, keepdims=True)
        return jnp.sum(c, axis=2, keepdims=True)

    def bit_step(i, t):
        cand = t | (jnp.int32(1) << (30 - i))
        return jnp.where(count(aff >= lax.bitcast_convert_type(cand, F32)) >= cap, cand, t)

    bits = lax.fori_loop(0, 31, bit_step, jnp.zeros((ne, 1, 1), jnp.int32))
    lo = lax.bitcast_convert_type(bits, F32)
    hi = lax.bitcast_convert_type(bits + 1, F32)

    def mid_step(i, lh):
        lo, hi = lh
        mid = (lo + hi) * 0.5
        ok = count(aff >= mid) >= cap
        return jnp.where(ok, mid, lo), jnp.where(ok, hi, mid)

    lo, hi = lax.fori_loop(0, 30, mid_step, (lo, hi))
    gt = aff >= hi
    eq = (aff >= lo) & (aff < hi)
    need = (cap - count(gt)).astype(F32)

    lane = lax.broadcasted_iota(jnp.int32, (LANES, LANES), 0)
    lane_t = lax.broadcasted_iota(jnp.int32, (LANES, LANES), 1)
    tri_incl = (lane <= lane_t).astype(BF16)
    blk = lax.broadcasted_iota(jnp.int32, (nb, nb), 0)
    blk_t = lax.broadcasted_iota(jnp.int32, (nb, nb), 1)
    tri_blk = (blk <= blk_t).astype(BF16)
    ones_row = jnp.ones((SUBLANES, LANES), BF16)
    kvals = lax.broadcasted_iota(jnp.int32, (SUBLANES, nb), 1).astype(BF16)
    slot = lax.broadcasted_iota(jnp.int32, (cap, 1), 0).astype(F32)

    def block_prefix(mask_bf16):
        lc = _dot(mask_bf16, tri_incl)
        cnt_row = _dot_nt(ones_row, mask_bf16)
        inc_row = _dot(cnt_row.astype(BF16), tri_blk)
        return lc, inc_row - cnt_row, inc_row

    for e in range(ne):
        eq_e = eq[e].astype(BF16)
        lc_eq, off_eq, _ = block_prefix(eq_e)
        tri_strict = (blk_t < blk).astype(BF16)
        cnt_col = lc_eq[:, LANES - 1:LANES]
        before = _dot(tri_strict, jnp.broadcast_to(cnt_col, (nb, LANES)).astype(BF16))[:, :1]
        rank = lc_eq - eq[e].astype(F32) + before
        sel = gt[e] | (eq[e] & (rank < need[e]))
        sel_bf = sel.astype(BF16)

        lc, off_row, inc_row = block_prefix(sel_bf)
        off1 = off_row[0:1]
        inc1 = inc_row[0:1]
        onehot = ((slot >= off1) & (slot < inc1))
        oh_bf = onehot.astype(BF16)
        offk = jnp.sum(jnp.where(onehot, off1, 0.0), axis=1, keepdims=True)
        jl = slot - offk
        m = _dot(oh_bf, lc.astype(BF16))
        below = (m <= jl)
        kb_row = _dot_nt(kvals, oh_bf)
        r_row = _dot_nt(ones_row, below.astype(BF16))
        idx_ref[0, e:e + 1, :] = (kb_row[0:1] * float(LANES) + r_row[0:1]).astype(jnp.int32)

        msel = _dot(oh_bf, sel_bf)
        hit = (m == jl + 1.0) & (msel > 0.5)
        aff_rows = _dot_hi(onehot.astype(F32), aff[e])
        g_ref[0, e] = jnp.sum(jnp.where(hit, aff_rows, 0.0), axis=1, keepdims=True)


def _select_call(aff_t, cap):
    bt, ne, n = aff_t.shape
    nb = n // LANES
    return pl.pallas_call(
        functools.partial(_select_kernel, cap=cap),
        grid=(bt,),
        in_specs=[pl.BlockSpec((1, ne, nb, LANES), lambda b: (b, 0, 0, 0))],
        out_specs=[pl.BlockSpec((1, ne, cap), lambda b: (b, 0, 0)),
                   pl.BlockSpec((1, ne, cap, 1), lambda b: (b, 0, 0, 0))],
        out_shape=[jax.ShapeDtypeStruct((bt, ne, cap), jnp.int32),
                   jax.ShapeDtypeStruct((bt, ne, cap, 1), F32)],
        compiler_params=_cparams(("parallel",)),
        name="expert_select",
    )(aff_t.reshape(bt, ne, nb, LANES))


def _ffn_kernel(idx_ref, idxn_ref, h_hbm, g_ref, gate2_ref, wg_ref, wu_ref, wd_ref, o_ref,
                xs_ref, wgb_ref, wub_ref, wdb_ref, sem, *, cap, chunk):
    e = pl.program_id(0)
    b = pl.program_id(1)
    nb = pl.num_programs(1)
    step = e * nb + b
    slot = step % 2

    def row_copy(ids_ref, bi, dst_slot, j):
        return pltpu.make_async_copy(h_hbm.at[bi, pl.ds(ids_ref[0, 0, 0, j], 1), :],
                                     xs_ref.at[dst_slot, pl.ds(j, 1), :], sem.at[dst_slot])

    def wait_slot(s):
        pltpu.make_async_copy(h_hbm.at[b, pl.ds(0, cap), :], xs_ref.at[s], sem.at[s]).wait()

    @pl.when(step == 0)
    def _():
        def start(j, carry):
            row_copy(idx_ref, b, slot, j).start()
            return carry
        lax.fori_loop(0, cap, start, 0, unroll=8)

    @pl.when(b == 0)
    def _():
        wgb_ref[...] = wg_ref[0, 0].astype(BF16)
        wub_ref[...] = wu_ref[0, 0].astype(BF16)
        wdb_ref[...] = wd_ref[0, 0].astype(BF16)

    wait_slot(slot)

    bn = (b + 1) % nb
    for c in range(cap // chunk):
        rs = slice(c * chunk, (c + 1) * chunk)
        for j in range(c * chunk, (c + 1) * chunk):
            row_copy(idxn_ref, bn, 1 - slot, j).start()
        xb = xs_ref[slot, rs, :].astype(BF16)
        gate = _dot(xb, wgb_ref[...])
        up = _dot(xb, wub_ref[...])
        hid = (gate * jax.nn.sigmoid(gate)) * up
        y = _dot(hid.astype(BF16), wdb_ref[...])
        o_ref[0, 0, rs, :] = (y * g_ref[0, 0, rs, :]) * gate2_ref[0]

    @pl.when(step == pl.num_programs(0) * nb - 1)
    def _():
        wait_slot(1 - slot)


def _ffn_call(idx, h2, g, gate2, wg, wu, wd, layer):
    bt, ne, cap = idx.shape
    d = h2.shape[2]
    ff = wg.shape[3]
    chunk = min(cap, 256)
    wspec = lambda s: pl.BlockSpec((1, 1) + s, lambda e, b: (layer, e, 0, 0))
    idx4 = idx.reshape(bt, ne, 1, cap)
    nxt = lambda e, b: ((b + 1) % bt, jnp.minimum(e + (b + 1) // bt, ne - 1), 0, 0)
    return pl.pallas_call(
        functools.partial(_ffn_kernel, cap=cap, chunk=chunk),
        grid=(ne, bt),
        in_specs=[pl.BlockSpec((1, 1, 1, cap), lambda e, b: (b, e, 0, 0), memory_space=pltpu.SMEM),
                  pl.BlockSpec((1, 1, 1, cap), nxt, memory_space=pltpu.SMEM),
                  pl.BlockSpec(memory_space=pl.ANY),
                  pl.BlockSpec((1, 1, cap, 1), lambda e, b: (b, e, 0, 0)),
                  pl.BlockSpec((1, 1, d), lambda e, b: (b, 0, 0)),
                  wspec((d, ff)), wspec((d, ff)), wspec((ff, d))],
        out_specs=pl.BlockSpec((1, 1, cap, d), lambda e, b: (b, e, 0, 0)),
        out_shape=jax.ShapeDtypeStruct((bt, ne, cap, d), F32),
        scratch_shapes=[pltpu.VMEM((2, cap, d), F32), pltpu.VMEM((d, ff), BF16), pltpu.VMEM((d, ff), BF16),
                        pltpu.VMEM((ff, d), BF16), pltpu.SemaphoreType.DMA((2,))],
        compiler_params=_cparams(("arbitrary", "arbitrary")),
        name="expert_ffn",
    )(idx4, idx4, h2, g, gate2, wg, wu, wd)


def _combine_kernel(idx_ref, x_hbm, z_ref, gfin_ref, o_hbm, acc_ref, sem, *, cap, final_norm, norm_chunk):
    b = pl.program_id(0)
    e = pl.program_id(1)

    @pl.when(e == 0)
    def _():
        cp = pltpu.make_async_copy(x_hbm.at[b], acc_ref, sem)
        cp.start()
        cp.wait()

    group = 4
    for j0 in range(0, cap, group):
        toks = [idx_ref[0, 0, 0, j0 + u] for u in range(group)]
        vals = [acc_ref[pl.ds(toks[u], 1), :] + z_ref[0, 0, j0 + u:j0 + u + 1, :] for u in range(group)]
        for u in range(group):
            acc_ref[pl.ds(toks[u], 1), :] = vals[u]

    @pl.when(e == pl.num_programs(1) - 1)
    def _():
        if final_norm:
            def norm(i, carry):
                rs = pl.ds(pl.multiple_of(i * norm_chunk, norm_chunk), norm_chunk)
                acc_ref[rs, :] = _rms(acc_ref[rs, :], gfin_ref[...])
                return carry
            lax.fori_loop(0, acc_ref.shape[0] // norm_chunk, norm, 0)
        cp = pltpu.make_async_copy(acc_ref, o_hbm.at[b], sem)
        cp.start()
        cp.wait()


def _combine_call(idx, x, z, g_final, *, final_norm):
    bt, ne, cap = idx.shape
    _, n, d = x.shape
    return pl.pallas_call(
        functools.partial(_combine_kernel, cap=cap, final_norm=final_norm, norm_chunk=min(n, 256)),
        grid=(bt, ne),
        in_specs=[pl.BlockSpec((1, 1, 1, cap), lambda b, e: (b, e, 0, 0), memory_space=pltpu.SMEM),
                  pl.BlockSpec(memory_space=pl.ANY),
                  pl.BlockSpec((1, 1, cap, d), lambda b, e: (b, e, 0, 0)),
                  pl.BlockSpec((1, d), lambda b, e: (0, 0))],
        out_specs=pl.BlockSpec(memory_space=pl.ANY),
        out_shape=jax.ShapeDtypeStruct((bt, n, d), F32),
        scratch_shapes=[pltpu.VMEM((n, d), F32), pltpu.SemaphoreType.DMA(())],
        compiler_params=_cparams(("arbitrary", "arbitrary")),
        name="expert_combine",
    )(idx.reshape(bt, ne, 1, cap), x, z, g_final)


def _block_diag(w):
    h, hd, _ = w.shape
    eye = jnp.eye(h, dtype=w.dtype)
    return (w[:, :, None, :] * eye[:, None, :, None]).reshape(h * hd, h * hd)


def _gate_weights(wr, br, wi, bi):
    dense = [_block_diag(wr[0]), _block_diag(wi[0]), _block_diag(wr[1]), _block_diag(wi[1])]
    bias = [br[0], bi[0], br[1], bi[1]]
    lru_w = dense[0].shape[0]
    wg, bg = [], []
    for g in range(lru_w // GATE_GROUP):
        cs = slice(g * GATE_GROUP, (g + 1) * GATE_GROUP)
        wg.append(jnp.concatenate([m[cs, cs] for m in dense], axis=1))
        bg.append(jnp.concatenate([v[cs] for v in bias])[None, :])
    return jnp.stack(wg).astype(BF16), jnp.stack(bg)


def _channel_dft(fw):
    gd = fw // FFT_GROUPS
    c, s = _cos_sin(gd)
    eye = np.eye(FFT_GROUPS)
    return _split_const(np.concatenate([np.kron(eye, c), np.kron(eye, s)], axis=1))


def _mixer(x, lp, shift, scale, h0_f, h0_b, *, row_w, tm, tb, need_out):
    af, bf, ab, bb, gl, ysc, ucs = _inproj_call(
        x, shift, scale, lp["g1"], lp["w_in"], lp["cw"], lp["cb"], lp["wg"], lp["bg"], lp["lam"],
        lp["scw"], lp["g_sc"], lp["dftc"], row_w=row_w, tm=tm)
    hf, sf = _scan_call(af, bf, h0_f, None, reverse=False, tb=tb)
    hs, sb = _scan_call(ab, bb, h0_b, hf, reverse=True, tb=tb)
    if not need_out:
        return sf, sb, None
    yfft = _fourier_call(ucs, lp["g_fft"])
    return sf, sb, (hs, gl, ysc, yfft)


def _moe(x, h2, aff_t, gate2, lp, g_final, *, final_norm):
    bt, n, d = x.shape
    cap = CAPACITY_FACTOR * n // N_EXPERTS
    pad = (-n) % (SUBLANES * LANES)
    if pad:
        aff_t = jnp.pad(aff_t, ((0, 0), (0, 0), (0, pad)), constant_values=-1.0)
    idx, g = _select_call(aff_t, cap)
    z = _ffn_call(idx, h2, g, gate2, lp["wge"], lp["wue"], lp["wde"], lp["layer"])
    return _combine_call(idx, x, z, g_final, final_norm=final_norm)


def kernel(x, c, ctx, c_ctx, w_ada, b_ada, g_norm1, w_in, lru_conv_w, lru_conv_b, lru_wr, lru_br, lru_wi,
           lru_bi, lru_lam, sc_conv_w, g_out, w_out, g_norm2, w_router, w_gate_e, w_up_e, w_down_e, g_final):
    depth = w_ada.shape[0]
    bsz, seq, d = x.shape
    ctx_len = ctx.shape[1]
    lru_w = lru_conv_w.shape[2]
    conv_w = sc_conv_w.shape[2]
    fft_w = w_in.shape[2] - 2 * lru_w - 3 * conv_w

    cs = jnp.concatenate([c, c_ctx[None, :], jnp.zeros((SUBLANES - bsz - 1, d), F32)], axis=0)
    mods = _ada_call(cs, w_ada, b_ada)
    dftc = _channel_dft(fft_w)
    gfin = g_final[None, :]
    zero_state = jnp.zeros((bsz, 1, lru_w), F32)

    for l in range(depth):
        last = l == depth - 1
        mx = [mods[l, :bsz, None, k * d:(k + 1) * d] for k in range(6)]
        mc = [jnp.broadcast_to(mods[l, bsz, k * d:(k + 1) * d], (bsz, 1, d)) for k in range(6)]
        wg, bg = _gate_weights(lru_wr[l], lru_br[l], lru_wi[l], lru_bi[l])
        lp = dict(
            g1=g_norm1[l][None, :], w_in=w_in[l].astype(BF16), cw=lru_conv_w[l], cb=lru_conv_b[l][None, :],
            wg=wg, bg=bg, lam=lru_lam[l].reshape(1, 2 * lru_w), scw=sc_conv_w[l],
            g_sc=g_out[l][None, lru_w:lru_w + conv_w], g_fft=g_out[l][None, lru_w + conv_w:], dftc=dftc,
            wge=w_gate_e, wue=w_up_e, wde=w_down_e, layer=l)
        g_lru = g_out[l][None, :lru_w]
        w_out_l = w_out[l].astype(BF16)
        w_router_t = w_router[l].T

        sf, sb, parts = _mixer(ctx, lp, mc[0], mc[1], zero_state, zero_state,
                               row_w=ctx_len, tm=ctx_len, tb=ctx_len, need_out=not last)
        if not last:
            ctx, hc2, aff_c = _outproj_call(*parts, ctx, w_out_l, g_lru, mc[2], mc[3], mc[4],
                                            g_norm2[l][None, :], w_router_t, tm=ctx_len)
            ctx = _moe(ctx, hc2, aff_c, mc[5], lp, gfin, final_norm=False)

        _, _, parts = _mixer(x, lp, mx[0], mx[1], sf, sb, row_w=GRID_W, tm=512, tb=512, need_out=True)
        x, hx2, aff_x = _outproj_call(*parts, x, w_out_l, g_lru, mx[2], mx[3], mx[4],
                                      g_norm2[l][None, :], w_router_t, tm=512)
        x = _moe(x, hx2, aff_x, mx[5], lp, gfin, final_norm=last)
    return x
```

```python
import functools
import math

import numpy as np
import jax
import jax.numpy as jnp
from jax import lax
from jax.experimental import pallas as pl
from jax.experimental.pallas import tpu as pltpu

F32 = jnp.float32
BF16 = jnp.bfloat16
HI = lax.Precision.HIGHEST

GRID_W = 64
LRU_HEADS = 8
LRU_C = 8.0
N_EXPERTS = 16
CAPACITY_FACTOR = 2
EPS = 1e-6
FFT_GROUPS = 4

LANES = 128
SUBLANES = 8
GATE_GROUP = 256
VMEM_LIMIT = 56 * 1024 * 1024


def _cparams(sem):
    return pltpu.CompilerParams(dimension_semantics=sem, vmem_limit_bytes=VMEM_LIMIT)


def _rms(x, g):
    return x * lax.rsqrt(jnp.mean(x * x, axis=-1, keepdims=True) + EPS) * g


def _dot(a, b):
    return jnp.dot(a, b, preferred_element_type=F32)


def _dot_hi(a, b):
    return jnp.dot(a, b, preferred_element_type=F32, precision=HI)


def _split_const(m):
    m = jnp.asarray(m, F32)
    hi = m.astype(BF16)
    return jnp.stack([hi, (m - hi.astype(F32)).astype(BF16)])


def _split(x):
    hi = x.astype(BF16)
    return hi, (x - hi.astype(F32)).astype(BF16)


def _dot3_const_lhs(m_ref, x):
    x_hi, x_lo = _split(x)
    return _dot(m_ref[0], x_hi) + (_dot(m_ref[1], x_hi) + _dot(m_ref[0], x_lo))


def _dot3_const_rhs(x, m_ref):
    x_hi, x_lo = _split(x)
    return _dot(x_hi, m_ref[0]) + (_dot(x_hi, m_ref[1]) + _dot(x_lo, m_ref[0]))


def _dot_nt(a, b, precision=None):
    return lax.dot_general(a, b, (((1,), (1,)), ((), ())), preferred_element_type=F32, precision=precision)


def _ada_kernel(c_ref, w_ref, b_ref, o_ref):
    c = c_ref[...]
    o_ref[...] = _dot_hi(c * jax.nn.sigmoid(c), w_ref[...]) + b_ref[...]


def _ada_call(cs, w_ada, b_ada):
    depth, d, six_d = w_ada.shape
    nblk = six_d // d
    return pl.pallas_call(
        _ada_kernel,
        grid=(depth, nblk),
        in_specs=[
            pl.BlockSpec((SUBLANES, d), lambda l, j: (0, 0)),
            pl.BlockSpec((None, d, d), lambda l, j: (l, 0, j)),
            pl.BlockSpec((None, 1, d), lambda l, j: (l, 0, j)),
        ],
        out_specs=pl.BlockSpec((None, SUBLANES, d), lambda l, j: (l, 0, j)),
        out_shape=jax.ShapeDtypeStruct((depth, SUBLANES, six_d), F32),
        compiler_params=_cparams(("parallel", "parallel")),
        name="ada_mod",
    )(cs, w_ada, b_ada.reshape(depth, 1, six_d))


def _shift_rows(u, d, pos, row_w):
    n = u.shape[0]
    if d == 0:
        return u
    rolled = pltpu.roll(u, (-d) % n, axis=0)
    valid = (pos + d >= 0) & (pos + d < row_w)
    return jnp.where(valid, rolled, 0.0)


def _conv_rows(u, w_ref, left, pos, row_w):
    out = None
    for k in range(w_ref.shape[0]):
        term = w_ref[k:k + 1, :] * _shift_rows(u, k - left, pos, row_w)
        out = term if out is None else out + term
    return out


def _inproj_kernel(x_ref, sh_ref, sc_ref, g1_ref, win_ref, cw_ref, cb_ref, wg_ref, bg_ref, lam_ref,
                   scw_ref, gsc_ref, dft_ref,
                   af_ref, bf_ref, ab_ref, bb_ref, gl_ref, ysc_ref, ucs_ref, winb_ref, *, row_w, lru_w, conv_w):
    @pl.when((pl.program_id(0) == 0) & (pl.program_id(1) == 0))
    def _():
        winb_ref[...] = win_ref[...].astype(BF16)

    x = x_ref[0]
    tm = x.shape[0]
    h = _rms(x, g1_ref[...]) * (1.0 + sc_ref[0]) + sh_ref[0]
    p = _dot(h.astype(BF16), winb_ref[...])
    pos = lax.broadcasted_iota(jnp.int32, (tm, 1), 0) % row_w

    u = _conv_rows(p[:, :lru_w], cw_ref, 1, pos, row_w) + cb_ref[...]
    lam = lam_ref[...]
    nl = -lam
    softplus = jnp.maximum(nl, 0.0) + jnp.log(1.0 + jnp.exp(-jnp.abs(nl)))
    out_refs = ((af_ref, bf_ref), (ab_ref, bb_ref))
    for g in range(lru_w // GATE_GROUP):
        cs = slice(g * GATE_GROUP, (g + 1) * GATE_GROUP)
        ug = u[:, cs]
        z = _dot(ug.astype(BF16), wg_ref[g]) + bg_ref[g]
        for d in range(2):
            r = jax.nn.sigmoid(z[:, (2 * d) * GATE_GROUP:(2 * d + 1) * GATE_GROUP])
            i = jax.nn.sigmoid(z[:, (2 * d + 1) * GATE_GROUP:(2 * d + 2) * GATE_GROUP])
            sp = softplus[:, d * lru_w + g * GATE_GROUP: d * lru_w + (g + 1) * GATE_GROUP]
            a = jnp.exp(-LRU_C * r * sp)
            out_refs[d][0][0, :, cs] = a
            out_refs[d][1][0, :, cs] = jnp.sqrt(1.0 - a * a) * (i * ug)

    gl_ref[0] = jax.nn.gelu(p[:, lru_w:2 * lru_w], approximate=True)

    o = 2 * lru_w
    sc_b = p[:, o:o + conv_w]
    sc_c = p[:, o + conv_w:o + 2 * conv_w]
    sc_x = p[:, o + 2 * conv_w:o + 3 * conv_w]
    y_sc = sc_b * _conv_rows(sc_c * sc_x, scw_ref, 1, pos, row_w)
    ysc_ref[0] = _rms(y_sc, gsc_ref[...])

    ucs_ref[0] = _dot3_const_rhs(p[:, o + 3 * conv_w:], dft_ref)


def _inproj_call(x, shift, scale, g1, w_in, layer, cw, cb, wg, bg, lam, scw, gsc, dftc, *, row_w, tm):
    bt, n, d = x.shape
    in_cols = w_in.shape[2]
    lru_w = cw.shape[1]
    conv_w = scw.shape[1]
    fft_w = in_cols - 2 * lru_w - 3 * conv_w
    ng = lru_w // GATE_GROUP
    tok = lambda w: pl.BlockSpec((1, tm, w), lambda b, i: (b, i, 0))
    vec = lambda w: pl.BlockSpec((1, 1, w), lambda b, i: (b, 0, 0))
    full = lambda *s: pl.BlockSpec(s, lambda b, i: (0,) * len(s))
    shp = lambda w: jax.ShapeDtypeStruct((bt, n, w), F32)
    return pl.pallas_call(
        functools.partial(_inproj_kernel, row_w=row_w, lru_w=lru_w, conv_w=conv_w),
        grid=(bt, n // tm),
        in_specs=[tok(d), vec(d), vec(d), full(1, d),
                  pl.BlockSpec((None, d, in_cols), lambda b, i: (layer, 0, 0), pipeline_mode=pl.Buffered(1)),
                  full(*cw.shape), full(1, lru_w),
                  full(ng, GATE_GROUP, 4 * GATE_GROUP), full(ng, 1, 4 * GATE_GROUP), full(1, 2 * lru_w),
                  full(*scw.shape), full(1, conv_w), full(2, fft_w, 2 * fft_w)],
        out_specs=[tok(lru_w)] * 5 + [tok(conv_w), tok(2 * fft_w)],
        out_shape=[shp(lru_w)] * 5 + [shp(conv_w), shp(2 * fft_w)],
        scratch_shapes=[pltpu.VMEM((d, in_cols), BF16)],
        compiler_params=_cparams(("arbitrary", "arbitrary")),
        name="inproj_local",
    )(x, shift, scale, g1, w_in, cw, cb, wg, bg, lam, scw, gsc, dftc)


def _scan_kernel(*refs, reverse, add_other):
    if add_other:
        a_ref, b_ref, h0_ref, other_ref, h_ref, hl_ref, carry_ref = refs
    else:
        a_ref, b_ref, h0_ref, h_ref, hl_ref, carry_ref = refs
        other_ref = None

    @pl.when(pl.program_id(1) == 0)
    def _():
        carry_ref[...] = jnp.broadcast_to(h0_ref[0], carry_ref.shape)

    tb, w = a_ref.shape[1], a_ref.shape[2]
    ntile = tb // SUBLANES
    a = a_ref[0].reshape(ntile, SUBLANES, w)
    b = b_ref[0].reshape(ntile, SUBLANES, w)
    row = lax.broadcasted_iota(jnp.int32, (1, SUBLANES, 1), 1)
    for s in (1, 2, 4):
        if reverse:
            a_sh = pltpu.roll(a, SUBLANES - s, axis=1)
            b_sh = pltpu.roll(b, SUBLANES - s, axis=1)
            valid = row < SUBLANES - s
        else:
            a_sh = pltpu.roll(a, s, axis=1)
            b_sh = pltpu.roll(b, s, axis=1)
            valid = row >= s
        b = jnp.where(valid, b + a * b_sh, b)
        a = jnp.where(valid, a * a_sh, a)
    carry = carry_ref[...]
    order = range(ntile - 1, -1, -1) if reverse else range(ntile)
    edge = 0 if reverse else SUBLANES - 1
    for t in order:
        rs = slice(t * SUBLANES, (t + 1) * SUBLANES)
        h = b[t] + a[t] * carry
        if other_ref is not None:
            h_ref[0, rs, :] = h + other_ref[0, rs, :]
        else:
            h_ref[0, rs, :] = h
        carry = jnp.broadcast_to(h[edge:edge + 1], h.shape)
    carry_ref[...] = carry
    hl_ref[0] = carry[0:1]


def _scan_call(a, b, h0, other, *, reverse, tb):
    bt, n, w = a.shape
    nt = n // tb
    blk = (lambda bi, i: (bi, nt - 1 - i, 0)) if reverse else (lambda bi, i: (bi, i, 0))
    tok = pl.BlockSpec((1, tb, w), blk)
    st = pl.BlockSpec((1, 1, w), lambda bi, i: (bi, 0, 0))
    ins = [a, b, h0] + ([other] if other is not None else [])
    in_specs = [tok, tok, st] + ([tok] if other is not None else [])
    return pl.pallas_call(
        functools.partial(_scan_kernel, reverse=reverse, add_other=other is not None),
        grid=(bt, nt),
        in_specs=in_specs,
        out_specs=[tok, st],
        out_shape=[jax.ShapeDtypeStruct((bt, n, w), F32), jax.ShapeDtypeStruct((bt, 1, w), F32)],
        scratch_shapes=[pltpu.VMEM((SUBLANES, w), F32)],
        compiler_params=_cparams(("parallel", "arbitrary")),
        name="lru_scan_bwd" if reverse else "lru_scan_fwd",
    )(*ins)


def _cos_sin(n):
    k = np.arange(n, dtype=np.float64)
    ang = 2.0 * np.pi * np.outer(k, k) / n
    return np.cos(ang), np.sin(ang)


def _slab_copies(src_hbm, dst_ref, sem, b, first, count, width, slot):
    return [pltpu.make_async_copy(src_hbm.at[b, :, first + j, :], dst_ref.at[slot, :, pl.ds(j * width, width)],
                                  sem.at[slot]) for j in range(count)]


def _prefetch_slabs(src_hbm, dst_ref, sem, count, width):
    b, i = pl.program_id(0), pl.program_id(1)
    ni = pl.num_programs(1)
    step = b * ni + i
    slot = step % 2

    @pl.when(step == 0)
    def _():
        for cp in _slab_copies(src_hbm, dst_ref, sem, b, i * count, count, width, slot):
            cp.start()

    @pl.when(step + 1 < pl.num_programs(0) * ni)
    def _():
        wrap = i + 1 == ni
        bn = jnp.where(wrap, b + 1, b)
        nxt = jnp.where(wrap, 0, i + 1)
        for cp in _slab_copies(src_hbm, dst_ref, sem, bn, nxt * count, count, width, 1 - slot):
            cp.start()

    for cp in _slab_copies(src_hbm, dst_ref, sem, b, i * count, count, width, slot):
        cp.wait()
    return step, slot


def _dft1_kernel(x_hbm, m_ref, tc_ref, ts_ref, o_ref, xs_ref, sem, *, n2, jn, fw):
    _, slot = _prefetch_slabs(x_hbm, xs_ref, sem, jn, 2 * fw)
    r = _dot3_const_lhs(m_ref, xs_ref[slot])
    for j in range(jn):
        base = j * 2 * fw
        c_uc = r[:n2, base:base + fw]
        c_us = r[:n2, base + fw:base + 2 * fw]
        s_uc = r[n2:, base:base + fw]
        s_us = r[n2:, base + fw:base + 2 * fw]
        br = c_uc - s_us
        bi = -(c_us + s_uc)
        tc = tc_ref[0, :, j:j + 1]
        ts = ts_ref[0, :, j:j + 1]
        o_ref[0, j, :, :fw] = br * tc + bi * ts
        o_ref[0, j, :, fw:] = bi * tc - br * ts


def _dft2_kernel(p_hbm, m_ref, g_ref, y_hbm, ps_ref, ys_ref, sem_in, sem_out, *, n1, kn, fw):
    step, slot = _prefetch_slabs(p_hbm, ps_ref, sem_in, kn, 2 * fw)
    b, i = pl.program_id(0), pl.program_id(1)
    last = pl.num_programs(0) * pl.num_programs(1) - 1

    def out_copies(s):
        return [pltpu.make_async_copy(ys_ref.at[s, :, pl.ds(k * fw, fw)], y_hbm.at[b, :, i * kn + k, :],
                                      sem_out.at[s]) for k in range(kn)]

    r = _dot3_const_lhs(m_ref, ps_ref[slot])

    @pl.when(step >= 2)
    def _():
        for cp in out_copies(slot):
            cp.wait()

    for k in range(kn):
        base = k * 2 * fw
        y = r[:n1, base:base + fw] + r[n1:, base + fw:base + 2 * fw]
        ys_ref[slot, :, k * fw:(k + 1) * fw] = _rms(y, g_ref[...])
    for cp in out_copies(slot):
        cp.start()

    @pl.when(step == last)
    def _():
        for cp in out_copies(slot):
            cp.wait()

    @pl.when((step == last) & (step >= 1))
    def _():
        for cp in out_copies(1 - slot):
            cp.wait()


def _dft_direct_kernel(x_ref, m_ref, g_ref, o_ref, *, n, fw):
    x = x_ref[0]
    r = _dot3_const_lhs(m_ref, x)
    y = r[:n, :fw] - r[n:, fw:]
    o_ref[0] = _rms(y, g_ref[...])


def _fourier_call(ucs, g_fft):
    bt, n, fw2 = ucs.shape
    fw = fw2 // 2
    gd = fw // FFT_GROUPS
    scale = 1.0 / math.sqrt(n * gd)
    if n <= 512:
        c, s = _cos_sin(n)
        m = _split_const(np.concatenate([c, s], 0) * scale)
        return pl.pallas_call(
            functools.partial(_dft_direct_kernel, n=n, fw=fw),
            grid=(bt,),
            in_specs=[pl.BlockSpec((1, n, fw2), lambda b: (b, 0, 0)),
                      pl.BlockSpec((2, 2 * n, n), lambda b: (0, 0, 0)),
                      pl.BlockSpec((1, fw), lambda b: (0, 0))],
            out_specs=pl.BlockSpec((1, n, fw), lambda b: (b, 0, 0)),
            out_shape=jax.ShapeDtypeStruct((bt, n, fw), F32),
            compiler_params=_cparams(("parallel",)),
            name="dft_direct",
        )(ucs, m, g_fft)

    n1 = LANES
    n2 = n // n1
    jn = 16
    c2, s2 = _cos_sin(n2)
    m1 = _split_const(np.concatenate([c2, s2], 0))
    ang = 2.0 * np.pi * np.outer(np.arange(n2), np.arange(n1)) / n
    tw = lambda f: jnp.asarray(f(ang).reshape(n2, n1 // jn, jn).transpose(1, 0, 2), F32)
    p = pl.pallas_call(
        functools.partial(_dft1_kernel, n2=n2, jn=jn, fw=fw),
        grid=(bt, n1 // jn),
        in_specs=[pl.BlockSpec(memory_space=pl.ANY),
                  pl.BlockSpec((2, 2 * n2, n2), lambda b, i: (0, 0, 0)),
                  pl.BlockSpec((1, n2, jn), lambda b, i: (i, 0, 0)),
                  pl.BlockSpec((1, n2, jn), lambda b, i: (i, 0, 0))],
        out_specs=pl.BlockSpec((1, jn, n2, fw2), lambda b, i: (b, i, 0, 0)),
        out_shape=jax.ShapeDtypeStruct((bt, n1, n2, fw2), F32),
        scratch_shapes=[pltpu.VMEM((2, n2, jn * fw2), F32), pltpu.SemaphoreType.DMA((2,))],
        compiler_params=_cparams(("arbitrary", "arbitrary")),
        name="dft_stage1",
    )(ucs.reshape(bt, n2, n1, fw2), m1, tw(np.cos), tw(np.sin))

    kn = min(8, n2)
    c1, s1 = _cos_sin(n1)
    m2 = _split_const(np.concatenate([c1, s1], 0) * scale)
    y = pl.pallas_call(
        functools.partial(_dft2_kernel, n1=n1, kn=kn, fw=fw),
        grid=(bt, n2 // kn),
        in_specs=[pl.BlockSpec(memory_space=pl.ANY),
                  pl.BlockSpec((2, 2 * n1, n1), lambda b, i: (0, 0, 0)),
                  pl.BlockSpec((1, fw), lambda b, i: (0, 0))],
        out_specs=pl.BlockSpec(memory_space=pl.ANY),
        out_shape=jax.ShapeDtypeStruct((bt, n1, n2, fw), F32),
        scratch_shapes=[pltpu.VMEM((2, n1, kn * fw2), F32), pltpu.VMEM((2, n1, kn * fw), F32),
                        pltpu.SemaphoreType.DMA((2,)), pltpu.SemaphoreType.DMA((2,))],
        compiler_params=_cparams(("arbitrary", "arbitrary")),
        name="dft_stage2",
    )(p, m2, g_fft)
    return y.reshape(bt, n, fw)


def _outproj_kernel(hs_ref, gl_ref, ysc_ref, yfft_ref, x_ref, wout_ref, glru_ref, gate_ref,
                    sh_ref, sc_ref, g2_ref, wr_ref, xn_ref, h2_ref, aff_ref, woutb_ref):
    @pl.when((pl.program_id(0) == 0) & (pl.program_id(1) == 0))
    def _():
        woutb_ref[...] = wout_ref[...].astype(BF16)

    y_lru = _rms(hs_ref[0] * gl_ref[0], glru_ref[...])
    y = jnp.concatenate([y_lru.astype(BF16), ysc_ref[0].astype(BF16), yfft_ref[0].astype(BF16)], axis=-1)
    xn = x_ref[0] + gate_ref[0] * _dot(y, woutb_ref[...])
    xn_ref[0] = xn
    h2 = _rms(xn, g2_ref[...]) * (1.0 + sc_ref[0]) + sh_ref[0]
    h2_ref[0] = h2
    logits = _dot_nt(wr_ref[...], h2, precision=HI)
    m = jnp.max(logits, axis=0, keepdims=True)
    e = jnp.exp(logits - m)
    aff_ref[0] = e / jnp.sum(e, axis=0, keepdims=True)


def _outproj_call(hs, gl, ysc, yfft, x, w_out, layer, g_lru, gate, shift, scale, g2, w_router_t, *, tm):
    bt, n, d = x.shape
    ne = w_router_t.shape[0]
    tok = lambda w: pl.BlockSpec((1, tm, w), lambda b, i: (b, i, 0))
    vec = lambda w: pl.BlockSpec((1, 1, w), lambda b, i: (b, 0, 0))
    full = lambda *s: pl.BlockSpec(s, lambda b, i: (0,) * len(s))
    return pl.pallas_call(
        _outproj_kernel,
        grid=(bt, n // tm),
        in_specs=[tok(hs.shape[2]), tok(gl.shape[2]), tok(ysc.shape[2]), tok(yfft.shape[2]), tok(d),
                  pl.BlockSpec((None,) + w_out.shape[1:], lambda b, i: (layer, 0, 0), pipeline_mode=pl.Buffered(1)),
                  full(1, hs.shape[2]), vec(d), vec(d), vec(d), full(1, d), full(ne, d)],
        out_specs=[tok(d), tok(d), pl.BlockSpec((1, ne, tm), lambda b, i: (b, 0, i))],
        out_shape=[jax.ShapeDtypeStruct((bt, n, d), F32), jax.ShapeDtypeStruct((bt, n, d), F32),
                   jax.ShapeDtypeStruct((bt, ne, n), F32)],
        scratch_shapes=[pltpu.VMEM(w_out.shape[1:], BF16)],
        compiler_params=_cparams(("arbitrary", "arbitrary")),
        name="outproj_router",
    )(hs, gl, ysc, yfft, x, w_out, g_lru, gate, shift, scale, g2, w_router_t)


def _select_kernel(aff_ref, idx_ref, g_ref, *, cap):
    aff = aff_ref[0]
    ne, nb, _ = aff.shape

    def count(mask):
        c = jnp.sum(mask.astype(jnp.int32), axis=1, keepdims=True)
        return jnp.sum(c, axis=2, keepdims=True)

    def bit_step(i, t):
        cand = t | (jnp.int32(1) << (30 - i))
        return jnp.where(count(aff >= lax.bitcast_convert_type(cand, F32)) >= cap, cand, t)

    bits = lax.fori_loop(0, 31, bit_step, jnp.zeros((ne, 1, 1), jnp.int32))
    lo = lax.bitcast_convert_type(bits, F32)
    hi = lax.bitcast_convert_type(bits + 1, F32)

    def mid_step(i, lh):
        lo, hi = lh
        mid = (lo + hi) * 0.5
        ok = count(aff >= mid) >= cap
        return jnp.where(ok, mid, lo), jnp.where(ok, hi, mid)

    lo, hi = lax.fori_loop(0, 30, mid_step, (lo, hi))
    gt = aff >= hi
    eq = (aff >= lo) & (aff < hi)
    need = (cap - count(gt)).astype(F32)

    lane = lax.broadcasted_iota(jnp.int32, (LANES, LANES), 0)
    lane_t = lax.broadcasted_iota(jnp.int32, (LANES, LANES), 1)
    tri_incl = (lane <= lane_t).astype(BF16)
    blk = lax.broadcasted_iota(jnp.int32, (nb, nb), 0)
    blk_t = lax.broadcasted_iota(jnp.int32, (nb, nb), 1)
    tri_blk = (blk <= blk_t).astype(BF16)
    ones_row = jnp.ones((SUBLANES, LANES), BF16)
    kvals = lax.broadcasted_iota(jnp.int32, (SUBLANES, nb), 1).astype(BF16)
    slot = lax.broadcasted_iota(jnp.int32, (cap, 1), 0).astype(F32)

    def block_prefix(mask_bf16):
        lc = _dot(mask_bf16, tri_incl)
        cnt_row = _dot_nt(ones_row, mask_bf16)
        inc_row = _dot(cnt_row.astype(BF16), tri_blk)
        return lc, inc_row - cnt_row, inc_row

    for e in range(ne):
        eq_e = eq[e].astype(BF16)
        lc_eq, off_eq, _ = block_prefix(eq_e)
        tri_strict = (blk_t < blk).astype(BF16)
        cnt_col = lc_eq[:, LANES - 1:LANES]
        before = _dot(tri_strict, jnp.broadcast_to(cnt_col, (nb, LANES)).astype(BF16))[:, :1]
        rank = lc_eq - eq[e].astype(F32) + before
        sel = gt[e] | (eq[e] & (rank < need[e]))
        sel_bf = sel.astype(BF16)

        lc, off_row, inc_row = block_prefix(sel_bf)
        off1 = off_row[0:1]
        inc1 = inc_row[0:1]
        onehot = ((slot >= off1) & (slot < inc1))
        oh_bf = onehot.astype(BF16)
        offk = jnp.sum(jnp.where(onehot, off1, 0.0), axis=1, keepdims=True)
        jl = slot - offk
        m = _dot(oh_bf, lc.astype(BF16))
        below = (m <= jl)
        kb_row = _dot_nt(kvals, oh_bf)
        r_row = _dot_nt(ones_row, below.astype(BF16))
        idx_ref[0, e:e + 1, :] = (kb_row[0:1] * float(LANES) + r_row[0:1]).astype(jnp.int32)

        msel = _dot(oh_bf, sel_bf)
        hit = (m == jl + 1.0) & (msel > 0.5)
        aff_rows = _dot_hi(onehot.astype(F32), aff[e])
        g_ref[0, e] = jnp.sum(jnp.where(hit, aff_rows, 0.0), axis=1, keepdims=True)


def _select_call(aff_t, cap):
    bt, ne, n = aff_t.shape
    nb = n // LANES
    return pl.pallas_call(
        functools.partial(_select_kernel, cap=cap),
        grid=(bt,),
        in_specs=[pl.BlockSpec((1, ne, nb, LANES), lambda b: (b, 0, 0, 0))],
        out_specs=[pl.BlockSpec((1, ne, cap), lambda b: (b, 0, 0)),
                   pl.BlockSpec((1, ne, cap, 1), lambda b: (b, 0, 0, 0))],
        out_shape=[jax.ShapeDtypeStruct((bt, ne, cap), jnp.int32),
                   jax.ShapeDtypeStruct((bt, ne, cap, 1), F32)],
        compiler_params=_cparams(("parallel",)),
        name="expert_select",
    )(aff_t.reshape(bt, ne, nb, LANES))


def _ffn_kernel(idx_ref, idxn_ref, h_hbm, g_ref, gate2_ref, wg_ref, wu_ref, wd_ref, o_ref,
                xs_ref, wgb_ref, wub_ref, wdb_ref, sem, *, cap, chunk):
    e = pl.program_id(0)
    b = pl.program_id(1)
    nb = pl.num_programs(1)
    step = e * nb + b
    slot = step % 2

    def row_copy(ids_ref, bi, dst_slot, j):
        return pltpu.make_async_copy(h_hbm.at[bi, pl.ds(ids_ref[0, 0, 0, j], 1), :],
                                     xs_ref.at[dst_slot, pl.ds(j, 1), :], sem.at[dst_slot])

    def wait_slot(s):
        pltpu.make_async_copy(h_hbm.at[b, pl.ds(0, cap), :], xs_ref.at[s], sem.at[s]).wait()

    @pl.when(step == 0)
    def _():
        def start(j, carry):
            row_copy(idx_ref, b, slot, j).start()
            return carry
        lax.fori_loop(0, cap, start, 0, unroll=8)

    @pl.when(b == 0)
    def _():
        wgb_ref[...] = wg_ref[0, 0].astype(BF16)
        wub_ref[...] = wu_ref[0, 0].astype(BF16)
        wdb_ref[...] = wd_ref[0, 0].astype(BF16)

    wait_slot(slot)

    bn = (b + 1) % nb
    for c in range(cap // chunk):
        rs = slice(c * chunk, (c + 1) * chunk)
        for j in range(c * chunk, (c + 1) * chunk):
            row_copy(idxn_ref, bn, 1 - slot, j).start()
        xb = xs_ref[slot, rs, :].astype(BF16)
        gate = _dot(xb, wgb_ref[...])
        up = _dot(xb, wub_ref[...])
        hid = (gate * jax.nn.sigmoid(gate)) * up
        y = _dot(hid.astype(BF16), wdb_ref[...])
        o_ref[0, 0, rs, :] = (y * g_ref[0, 0, rs, :]) * gate2_ref[0]

    @pl.when(step == pl.num_programs(0) * nb - 1)
    def _():
        wait_slot(1 - slot)


def _ffn_call(idx, h2, g, gate2, wg, wu, wd, layer):
    bt, ne, cap = idx.shape
    d = h2.shape[2]
    ff = wg.shape[3]
    chunk = min(cap, 256)
    wspec = lambda s: pl.BlockSpec((1, 1) + s, lambda e, b: (layer, e, 0, 0))
    idx4 = idx.reshape(bt, ne, 1, cap)
    nxt = lambda e, b: ((b + 1) % bt, jnp.minimum(e + (b + 1) // bt, ne - 1), 0, 0)
    return pl.pallas_call(
        functools.partial(_ffn_kernel, cap=cap, chunk=chunk),
        grid=(ne, bt),
        in_specs=[pl.BlockSpec((1, 1, 1, cap), lambda e, b: (b, e, 0, 0), memory_space=pltpu.SMEM),
                  pl.BlockSpec((1, 1, 1, cap), nxt, memory_space=pltpu.SMEM),
                  pl.BlockSpec(memory_space=pl.ANY),
                  pl.BlockSpec((1, 1, cap, 1), lambda e, b: (b, e, 0, 0)),
                  pl.BlockSpec((1, 1, d), lambda e, b: (b, 0, 0)),
                  wspec((d, ff)), wspec((d, ff)), wspec((ff, d))],
        out_specs=pl.BlockSpec((1, 1, cap, d), lambda e, b: (b, e, 0, 0)),
        out_shape=jax.ShapeDtypeStruct((bt, ne, cap, d), F32),
        scratch_shapes=[pltpu.VMEM((2, cap, d), F32), pltpu.VMEM((d, ff), BF16), pltpu.VMEM((d, ff), BF16),
                        pltpu.VMEM((ff, d), BF16), pltpu.SemaphoreType.DMA((2,))],
        compiler_params=_cparams(("arbitrary", "arbitrary")),
        name="expert_ffn",
    )(idx4, idx4, h2, g, gate2, wg, wu, wd)


def _combine_kernel(idx_ref, x_hbm, z_ref, gfin_ref, o_hbm, acc_ref, sem, *, cap, final_norm, norm_chunk):
    b = pl.program_id(0)
    e = pl.program_id(1)

    @pl.when(e == 0)
    def _():
        cp = pltpu.make_async_copy(x_hbm.at[b], acc_ref, sem)
        cp.start()
        cp.wait()

    group = 4
    for j0 in range(0, cap, group):
        toks = [idx_ref[0, 0, 0, j0 + u] for u in range(group)]
        vals = [acc_ref[pl.ds(toks[u], 1), :] + z_ref[0, 0, j0 + u:j0 + u + 1, :] for u in range(group)]
        for u in range(group):
            acc_ref[pl.ds(toks[u], 1), :] = vals[u]

    @pl.when(e == pl.num_programs(1) - 1)
    def _():
        if final_norm:
            def norm(i, carry):
                rs = pl.ds(pl.multiple_of(i * norm_chunk, norm_chunk), norm_chunk)
                acc_ref[rs, :] = _rms(acc_ref[rs, :], gfin_ref[...])
                return carry
            lax.fori_loop(0, acc_ref.shape[0] // norm_chunk, norm, 0)
        cp = pltpu.make_async_copy(acc_ref, o_hbm.at[b], sem)
        cp.start()
        cp.wait()


def _combine_call(idx, x, z, g_final, *, final_norm):
    bt, ne, cap = idx.shape
    _, n, d = x.shape
    return pl.pallas_call(
        functools.partial(_combine_kernel, cap=cap, final_norm=final_norm, norm_chunk=min(n, 256)),
        grid=(bt, ne),
        in_specs=[pl.BlockSpec((1, 1, 1, cap), lambda b, e: (b, e, 0, 0), memory_space=pltpu.SMEM),
                  pl.BlockSpec(memory_space=pl.ANY),
                  pl.BlockSpec((1, 1, cap, d), lambda b, e: (b, e, 0, 0)),
                  pl.BlockSpec((1, d), lambda b, e: (0, 0))],
        out_specs=pl.BlockSpec(memory_space=pl.ANY),
        out_shape=jax.ShapeDtypeStruct((bt, n, d), F32),
        scratch_shapes=[pltpu.VMEM((n, d), F32), pltpu.SemaphoreType.DMA(())],
        compiler_params=_cparams(("arbitrary", "arbitrary")),
        name="expert_combine",
    )(idx.reshape(bt, ne, 1, cap), x, z, g_final)


def _block_diag(w):
    h, hd, _ = w.shape
    eye = jnp.eye(h, dtype=w.dtype)
    return (w[:, :, None, :] * eye[:, None, :, None]).reshape(h * hd, h * hd)


def _gate_weights(wr, br, wi, bi):
    dense = [_block_diag(wr[0]), _block_diag(wi[0]), _block_diag(wr[1]), _block_diag(wi[1])]
    bias = [br[0], bi[0], br[1], bi[1]]
    lru_w = dense[0].shape[0]
    wg, bg = [], []
    for g in range(lru_w // GATE_GROUP):
        cs = slice(g * GATE_GROUP, (g + 1) * GATE_GROUP)
        wg.append(jnp.concatenate([m[cs, cs] for m in dense], axis=1))
        bg.append(jnp.concatenate([v[cs] for v in bias])[None, :])
    return jnp.stack(wg).astype(BF16), jnp.stack(bg)


def _channel_dft(fw):
    gd = fw // FFT_GROUPS
    c, s = _cos_sin(gd)
    eye = np.eye(FFT_GROUPS)
    return _split_const(np.concatenate([np.kron(eye, c), np.kron(eye, s)], axis=1))


def _mixer(x, lp, shift, scale, h0_f, h0_b, *, row_w, tm, tb, need_out):
    af, bf, ab, bb, gl, ysc, ucs = _inproj_call(
        x, shift, scale, lp["g1"], lp["w_in"], lp["layer"], lp["cw"], lp["cb"], lp["wg"], lp["bg"], lp["lam"],
        lp["scw"], lp["g_sc"], lp["dftc"], row_w=row_w, tm=tm)
    hf, sf = _scan_call(af, bf, h0_f, None, reverse=False, tb=tb)
    hs, sb = _scan_call(ab, bb, h0_b, hf, reverse=True, tb=tb)
    if not need_out:
        return sf, sb, None
    yfft = _fourier_call(ucs, lp["g_fft"])
    return sf, sb, (hs, gl, ysc, yfft)


def _moe(x, h2, aff_t, gate2, lp, g_final, *, final_norm):
    bt, n, d = x.shape
    cap = CAPACITY_FACTOR * n // N_EXPERTS
    pad = (-n) % (SUBLANES * LANES)
    if pad:
        aff_t = jnp.pad(aff_t, ((0, 0), (0, 0), (0, pad)), constant_values=-1.0)
    idx, g = _select_call(aff_t, cap)
    z = _ffn_call(idx, h2, g, gate2, lp["wge"], lp["wue"], lp["wde"], lp["layer"])
    return _combine_call(idx, x, z, g_final, final_norm=final_norm)


def kernel(x, c, ctx, c_ctx, w_ada, b_ada, g_norm1, w_in, lru_conv_w, lru_conv_b, lru_wr, lru_br, lru_wi,
           lru_bi, lru_lam, sc_conv_w, g_out, w_out, g_norm2, w_router, w_gate_e, w_up_e, w_down_e, g_final):
    depth = w_ada.shape[0]
    bsz, seq, d = x.shape
    ctx_len = ctx.shape[1]
    lru_w = lru_conv_w.shape[2]
    conv_w = sc_conv_w.shape[2]
    fft_w = w_in.shape[2] - 2 * lru_w - 3 * conv_w

    cs = jnp.concatenate([c, c_ctx[None, :], jnp.zeros((SUBLANES - bsz - 1, d), F32)], axis=0)
    mods = _ada_call(cs, w_ada, b_ada)
    dftc = _channel_dft(fft_w)
    gfin = g_final[None, :]
    zero_state = jnp.zeros((bsz, 1, lru_w), F32)

    for l in range(depth):
        last = l == depth - 1
        mx = [mods[l, :bsz, None, k * d:(k + 1) * d] for k in range(6)]
        mc = [jnp.broadcast_to(mods[l, bsz, k * d:(k + 1) * d], (bsz, 1, d)) for k in range(6)]
        wg, bg = _gate_weights(lru_wr[l], lru_br[l], lru_wi[l], lru_bi[l])
        lp = dict(
            g1=g_norm1[l][None, :], w_in=w_in, cw=lru_conv_w[l], cb=lru_conv_b[l][None, :],
            wg=wg, bg=bg, lam=lru_lam[l].reshape(1, 2 * lru_w), scw=sc_conv_w[l],
            g_sc=g_out[l][None, lru_w:lru_w + conv_w], g_fft=g_out[l][None, lru_w + conv_w:], dftc=dftc,
            wge=w_gate_e, wue=w_up_e, wde=w_down_e, layer=l)
        g_lru = g_out[l][None, :lru_w]
        w_router_t = w_router[l].T

        sf, sb, parts = _mixer(ctx, lp, mc[0], mc[1], zero_state, zero_state,
                               row_w=ctx_len, tm=ctx_len, tb=ctx_len, need_out=not last)
        if not last:
            ctx, hc2, aff_c = _outproj_call(*parts, ctx, w_out, l, g_lru, mc[2], mc[3], mc[4],
                                            g_norm2[l][None, :], w_router_t, tm=ctx_len)
            ctx = _moe(ctx, hc2, aff_c, mc[5], lp, gfin, final_norm=False)

        _, _, parts = _mixer(x, lp, mx[0], mx[1], sf, sb, row_w=GRID_W, tm=512, tb=512, need_out=True)
        x, hx2, aff_x = _outproj_call(*parts, x, w_out, l, g_lru, mx[2], mx[3], mx[4],
                                      g_norm2[l][None, :], w_router_t, tm=512)
        x = _moe(x, hx2, aff_x, mx[5], lp, gfin, final_norm=last)
    return x
```

```python
import functools
import math

import numpy as np
import jax
import jax.numpy as jnp
from jax import lax
from jax.experimental import pallas as pl
from jax.experimental.pallas import tpu as pltpu

F32 = jnp.float32
BF16 = jnp.bfloat16
HI = lax.Precision.HIGHEST

GRID_W = 64
LRU_HEADS = 8
LRU_C = 8.0
N_EXPERTS = 16
CAPACITY_FACTOR = 2
EPS = 1e-6
FFT_GROUPS = 4

LANES = 128
SUBLANES = 8
GATE_GROUP = 256
VMEM_LIMIT = 56 * 1024 * 1024


def _cparams(sem):
    return pltpu.CompilerParams(dimension_semantics=sem, vmem_limit_bytes=VMEM_LIMIT)


def _rms(x, g):
    return x * lax.rsqrt(jnp.mean(x * x, axis=-1, keepdims=True) + EPS) * g


def _dot(a, b):
    return jnp.dot(a, b, preferred_element_type=F32)


def _dot_hi(a, b):
    return jnp.dot(a, b, preferred_element_type=F32, precision=HI)


def _split_const(m):
    m = jnp.asarray(m, F32)
    hi = m.astype(BF16)
    return jnp.stack([hi, (m - hi.astype(F32)).astype(BF16)])


def _split(x):
    hi = x.astype(BF16)
    return hi, (x - hi.astype(F32)).astype(BF16)


def _dot3_const_lhs(m_ref, x):
    x_hi, x_lo = _split(x)
    return _dot(m_ref[0], x_hi) + (_dot(m_ref[1], x_hi) + _dot(m_ref[0], x_lo))


def _dot3_const_rhs(x, m_ref):
    x_hi, x_lo = _split(x)
    return _dot(x_hi, m_ref[0]) + (_dot(x_hi, m_ref[1]) + _dot(x_lo, m_ref[0]))


def _dot_nt(a, b, precision=None):
    return lax.dot_general(a, b, (((1,), (1,)), ((), ())), preferred_element_type=F32, precision=precision)


def _ada_kernel(c_ref, w_ref, b_ref, o_ref):
    c = c_ref[...]
    o_ref[...] = _dot_hi(c * jax.nn.sigmoid(c), w_ref[...]) + b_ref[...]


def _ada_call(cs, w_ada, b_ada):
    depth, d, six_d = w_ada.shape
    nblk = six_d // d
    return pl.pallas_call(
        _ada_kernel,
        grid=(depth, nblk),
        in_specs=[
            pl.BlockSpec((SUBLANES, d), lambda l, j: (0, 0)),
            pl.BlockSpec((None, d, d), lambda l, j: (l, 0, j)),
            pl.BlockSpec((None, 1, d), lambda l, j: (l, 0, j)),
        ],
        out_specs=pl.BlockSpec((None, SUBLANES, d), lambda l, j: (l, 0, j)),
        out_shape=jax.ShapeDtypeStruct((depth, SUBLANES, six_d), F32),
        compiler_params=_cparams(("parallel", "parallel")),
        name="ada_mod",
    )(cs, w_ada, b_ada.reshape(depth, 1, six_d))


def _shift_rows(u, d, pos, row_w):
    n = u.shape[0]
    if d == 0:
        return u
    rolled = pltpu.roll(u, (-d) % n, axis=0)
    valid = (pos + d >= 0) & (pos + d < row_w)
    return jnp.where(valid, rolled, 0.0)


def _conv_rows(u, w_ref, left, pos, row_w):
    out = None
    for k in range(w_ref.shape[0]):
        term = w_ref[k:k + 1, :] * _shift_rows(u, k - left, pos, row_w)
        out = term if out is None else out + term
    return out


def _inproj_kernel(x_ref, sh_ref, sc_ref, g1_ref, win_ref, cw_ref, cb_ref, wg_ref, bg_ref, lam_ref,
                   scw_ref, gsc_ref, dft_ref,
                   af_ref, bf_ref, ab_ref, bb_ref, gl_ref, ysc_ref, ucs_ref, winb_ref, *, row_w, lru_w, conv_w):
    @pl.when((pl.program_id(0) == 0) & (pl.program_id(1) == 0))
    def _():
        winb_ref[...] = win_ref[...].astype(BF16)

    x = x_ref[0]
    tm = x.shape[0]
    h = _rms(x, g1_ref[...]) * (1.0 + sc_ref[0]) + sh_ref[0]
    p = _dot(h.astype(BF16), winb_ref[...])
    pos = lax.broadcasted_iota(jnp.int32, (tm, 1), 0) % row_w

    u = _conv_rows(p[:, :lru_w], cw_ref, 1, pos, row_w) + cb_ref[...]
    lam = lam_ref[...]
    nl = -lam
    softplus = jnp.maximum(nl, 0.0) + jnp.log(1.0 + jnp.exp(-jnp.abs(nl)))
    out_refs = ((af_ref, bf_ref), (ab_ref, bb_ref))
    for g in range(lru_w // GATE_GROUP):
        cs = slice(g * GATE_GROUP, (g + 1) * GATE_GROUP)
        ug = u[:, cs]
        z = _dot(ug.astype(BF16), wg_ref[g]) + bg_ref[g]
        for d in range(2):
            r = jax.nn.sigmoid(z[:, (2 * d) * GATE_GROUP:(2 * d + 1) * GATE_GROUP])
            i = jax.nn.sigmoid(z[:, (2 * d + 1) * GATE_GROUP:(2 * d + 2) * GATE_GROUP])
            sp = softplus[:, d * lru_w + g * GATE_GROUP: d * lru_w + (g + 1) * GATE_GROUP]
            a = jnp.exp(-LRU_C * r * sp)
            out_refs[d][0][0, :, cs] = a
            out_refs[d][1][0, :, cs] = jnp.sqrt(1.0 - a * a) * (i * ug)

    gl_ref[0] = jax.nn.gelu(p[:, lru_w:2 * lru_w], approximate=True)

    o = 2 * lru_w
    sc_b = p[:, o:o + conv_w]
    sc_c = p[:, o + conv_w:o + 2 * conv_w]
    sc_x = p[:, o + 2 * conv_w:o + 3 * conv_w]
    y_sc = sc_b * _conv_rows(sc_c * sc_x, scw_ref, 1, pos, row_w)
    ysc_ref[0] = _rms(y_sc, gsc_ref[...])

    ucs_ref[0] = _dot3_const_rhs(p[:, o + 3 * conv_w:], dft_ref)


def _inproj_call(x, shift, scale, g1, w_in, layer, cw, cb, wg, bg, lam, scw, gsc, dftc, *, row_w, tm):
    bt, n, d = x.shape
    in_cols = w_in.shape[2]
    lru_w = cw.shape[1]
    conv_w = scw.shape[1]
    fft_w = in_cols - 2 * lru_w - 3 * conv_w
    ng = lru_w // GATE_GROUP
    tok = lambda w: pl.BlockSpec((1, tm, w), lambda b, i: (b, i, 0))
    vec = lambda w: pl.BlockSpec((1, 1, w), lambda b, i: (b, 0, 0))
    full = lambda *s: pl.BlockSpec(s, lambda b, i: (0,) * len(s))
    shp = lambda w: jax.ShapeDtypeStruct((bt, n, w), F32)
    return pl.pallas_call(
        functools.partial(_inproj_kernel, row_w=row_w, lru_w=lru_w, conv_w=conv_w),
        grid=(bt, n // tm),
        in_specs=[tok(d), vec(d), vec(d), full(1, d),
                  pl.BlockSpec((None, d, in_cols), lambda b, i: (layer, 0, 0), pipeline_mode=pl.Buffered(1)),
                  full(*cw.shape), full(1, lru_w),
                  full(ng, GATE_GROUP, 4 * GATE_GROUP), full(ng, 1, 4 * GATE_GROUP), full(1, 2 * lru_w),
                  full(*scw.shape), full(1, conv_w), full(2, fft_w, 2 * fft_w)],
        out_specs=[tok(lru_w)] * 5 + [tok(conv_w), tok(2 * fft_w)],
        out_shape=[shp(lru_w)] * 5 + [shp(conv_w), shp(2 * fft_w)],
        scratch_shapes=[pltpu.VMEM((d, in_cols), BF16)],
        compiler_params=_cparams(("arbitrary", "arbitrary")),
        name="inproj_local",
    )(x, shift, scale, g1, w_in, cw, cb, wg, bg, lam, scw, gsc, dftc)


def _scan_kernel(*refs, reverse, add_other):
    if add_other:
        a_ref, b_ref, h0_ref, other_ref, h_ref, hl_ref, carry_ref = refs
    else:
        a_ref, b_ref, h0_ref, h_ref, hl_ref, carry_ref = refs
        other_ref = None

    @pl.when(pl.program_id(1) == 0)
    def _():
        carry_ref[...] = jnp.broadcast_to(h0_ref[0], carry_ref.shape)

    tb, w = a_ref.shape[1], a_ref.shape[2]
    ntile = tb // SUBLANES
    a = a_ref[0].reshape(ntile, SUBLANES, w)
    b = b_ref[0].reshape(ntile, SUBLANES, w)
    row = lax.broadcasted_iota(jnp.int32, (1, SUBLANES, 1), 1)
    for s in (1, 2, 4):
        if reverse:
            a_sh = pltpu.roll(a, SUBLANES - s, axis=1)
            b_sh = pltpu.roll(b, SUBLANES - s, axis=1)
            valid = row < SUBLANES - s
        else:
            a_sh = pltpu.roll(a, s, axis=1)
            b_sh = pltpu.roll(b, s, axis=1)
            valid = row >= s
        b = jnp.where(valid, b + a * b_sh, b)
        a = jnp.where(valid, a * a_sh, a)
    carry = carry_ref[...]
    order = range(ntile - 1, -1, -1) if reverse else range(ntile)
    edge = 0 if reverse else SUBLANES - 1
    for t in order:
        rs = slice(t * SUBLANES, (t + 1) * SUBLANES)
        h = b[t] + a[t] * carry
        if other_ref is not None:
            h_ref[0, rs, :] = h + other_ref[0, rs, :]
        else:
            h_ref[0, rs, :] = h
        carry = jnp.broadcast_to(h[edge:edge + 1], h.shape)
    carry_ref[...] = carry
    hl_ref[0] = carry[0:1]


def _scan_call(a, b, h0, other, *, reverse, tb):
    bt, n, w = a.shape
    nt = n // tb
    blk = (lambda bi, i: (bi, nt - 1 - i, 0)) if reverse else (lambda bi, i: (bi, i, 0))
    tok = pl.BlockSpec((1, tb, w), blk)
    st = pl.BlockSpec((1, 1, w), lambda bi, i: (bi, 0, 0))
    ins = [a, b, h0] + ([other] if other is not None else [])
    in_specs = [tok, tok, st] + ([tok] if other is not None else [])
    return pl.pallas_call(
        functools.partial(_scan_kernel, reverse=reverse, add_other=other is not None),
        grid=(bt, nt),
        in_specs=in_specs,
        out_specs=[tok, st],
        out_shape=[jax.ShapeDtypeStruct((bt, n, w), F32), jax.ShapeDtypeStruct((bt, 1, w), F32)],
        scratch_shapes=[pltpu.VMEM((SUBLANES, w), F32)],
        compiler_params=_cparams(("parallel", "arbitrary")),
        name="lru_scan_bwd" if reverse else "lru_scan_fwd",
    )(*ins)


def _cos_sin(n):
    k = np.arange(n, dtype=np.float64)
    ang = 2.0 * np.pi * np.outer(k, k) / n
    return np.cos(ang), np.sin(ang)


def _slab_copies(src_hbm, dst_ref, sem, b, first, count, width, slot):
    return [pltpu.make_async_copy(src_hbm.at[b, :, first + j, :], dst_ref.at[slot, :, pl.ds(j * width, width)],
                                  sem.at[slot]) for j in range(count)]


def _prefetch_slabs(src_hbm, dst_ref, sem, count, width):
    b, i = pl.program_id(0), pl.program_id(1)
    ni = pl.num_programs(1)
    step = b * ni + i
    slot = step % 2

    @pl.when(step == 0)
    def _():
        for cp in _slab_copies(src_hbm, dst_ref, sem, b, i * count, count, width, slot):
            cp.start()

    @pl.when(step + 1 < pl.num_programs(0) * ni)
    def _():
        wrap = i + 1 == ni
        bn = jnp.where(wrap, b + 1, b)
        nxt = jnp.where(wrap, 0, i + 1)
        for cp in _slab_copies(src_hbm, dst_ref, sem, bn, nxt * count, count, width, 1 - slot):
            cp.start()

    for cp in _slab_copies(src_hbm, dst_ref, sem, b, i * count, count, width, slot):
        cp.wait()
    return step, slot


def _dft1_kernel(x_hbm, m_ref, tc_ref, ts_ref, o_ref, xs_ref, sem, *, n2, jn, fw):
    _, slot = _prefetch_slabs(x_hbm, xs_ref, sem, jn, 2 * fw)
    r = _dot3_const_lhs(m_ref, xs_ref[slot])
    for j in range(jn):
        base = j * 2 * fw
        c_uc = r[:n2, base:base + fw]
        c_us = r[:n2, base + fw:base + 2 * fw]
        s_uc = r[n2:, base:base + fw]
        s_us = r[n2:, base + fw:base + 2 * fw]
        br = c_uc - s_us
        bi = -(c_us + s_uc)
        tc = tc_ref[0, :, j:j + 1]
        ts = ts_ref[0, :, j:j + 1]
        o_ref[0, j, :, :fw] = br * tc + bi * ts
        o_ref[0, j, :, fw:] = bi * tc - br * ts


def _dft2_kernel(p_hbm, m_ref, g_ref, y_hbm, ps_ref, ys_ref, sem_in, sem_out, *, n1, kn, fw):
    step, slot = _prefetch_slabs(p_hbm, ps_ref, sem_in, kn, 2 * fw)
    b, i = pl.program_id(0), pl.program_id(1)
    last = pl.num_programs(0) * pl.num_programs(1) - 1

    def out_copies(s):
        return [pltpu.make_async_copy(ys_ref.at[s, :, pl.ds(k * fw, fw)], y_hbm.at[b, :, i * kn + k, :],
                                      sem_out.at[s]) for k in range(kn)]

    r = _dot3_const_lhs(m_ref, ps_ref[slot])

    @pl.when(step >= 2)
    def _():
        for cp in out_copies(slot):
            cp.wait()

    for k in range(kn):
        base = k * 2 * fw
        y = r[:n1, base:base + fw] + r[n1:, base + fw:base + 2 * fw]
        ys_ref[slot, :, k * fw:(k + 1) * fw] = _rms(y, g_ref[...])
    for cp in out_copies(slot):
        cp.start()

    @pl.when(step == last)
    def _():
        for cp in out_copies(slot):
            cp.wait()

    @pl.when((step == last) & (step >= 1))
    def _():
        for cp in out_copies(1 - slot):
            cp.wait()


def _dft_direct_kernel(x_ref, m_ref, g_ref, o_ref, *, n, fw):
    x = x_ref[0]
    r = _dot3_const_lhs(m_ref, x)
    y = r[:n, :fw] - r[n:, fw:]
    o_ref[0] = _rms(y, g_ref[...])


def _fourier_call(ucs, g_fft):
    bt, n, fw2 = ucs.shape
    fw = fw2 // 2
    gd = fw // FFT_GROUPS
    scale = 1.0 / math.sqrt(n * gd)
    if n <= 512:
        c, s = _cos_sin(n)
        m = _split_const(np.concatenate([c, s], 0) * scale)
        return pl.pallas_call(
            functools.partial(_dft_direct_kernel, n=n, fw=fw),
            grid=(bt,),
            in_specs=[pl.BlockSpec((1, n, fw2), lambda b: (b, 0, 0)),
                      pl.BlockSpec((2, 2 * n, n), lambda b: (0, 0, 0)),
                      pl.BlockSpec((1, fw), lambda b: (0, 0))],
            out_specs=pl.BlockSpec((1, n, fw), lambda b: (b, 0, 0)),
            out_shape=jax.ShapeDtypeStruct((bt, n, fw), F32),
            compiler_params=_cparams(("parallel",)),
            name="dft_direct",
        )(ucs, m, g_fft)

    n1 = LANES
    n2 = n // n1
    jn = 16
    c2, s2 = _cos_sin(n2)
    m1 = _split_const(np.concatenate([c2, s2], 0))
    ang = 2.0 * np.pi * np.outer(np.arange(n2), np.arange(n1)) / n
    tw = lambda f: jnp.asarray(f(ang).reshape(n2, n1 // jn, jn).transpose(1, 0, 2), F32)
    p = pl.pallas_call(
        functools.partial(_dft1_kernel, n2=n2, jn=jn, fw=fw),
        grid=(bt, n1 // jn),
        in_specs=[pl.BlockSpec(memory_space=pl.ANY),
                  pl.BlockSpec((2, 2 * n2, n2), lambda b, i: (0, 0, 0)),
                  pl.BlockSpec((1, n2, jn), lambda b, i: (i, 0, 0)),
                  pl.BlockSpec((1, n2, jn), lambda b, i: (i, 0, 0))],
        out_specs=pl.BlockSpec((1, jn, n2, fw2), lambda b, i: (b, i, 0, 0)),
        out_shape=jax.ShapeDtypeStruct((bt, n1, n2, fw2), F32),
        scratch_shapes=[pltpu.VMEM((2, n2, jn * fw2), F32), pltpu.SemaphoreType.DMA((2,))],
        compiler_params=_cparams(("arbitrary", "arbitrary")),
        name="dft_stage1",
    )(ucs.reshape(bt, n2, n1, fw2), m1, tw(np.cos), tw(np.sin))

    kn = min(8, n2)
    c1, s1 = _cos_sin(n1)
    m2 = _split_const(np.concatenate([c1, s1], 0) * scale)
    y = pl.pallas_call(
        functools.partial(_dft2_kernel, n1=n1, kn=kn, fw=fw),
        grid=(bt, n2 // kn),
        in_specs=[pl.BlockSpec(memory_space=pl.ANY),
                  pl.BlockSpec((2, 2 * n1, n1), lambda b, i: (0, 0, 0)),
                  pl.BlockSpec((1, fw), lambda b, i: (0, 0))],
        out_specs=pl.BlockSpec(memory_space=pl.ANY),
        out_shape=jax.ShapeDtypeStruct((bt, n1, n2, fw), F32),
        scratch_shapes=[pltpu.VMEM((2, n1, kn * fw2), F32), pltpu.VMEM((2, n1, kn * fw), F32),
                        pltpu.SemaphoreType.DMA((2,)), pltpu.SemaphoreType.DMA((2,))],
        compiler_params=_cparams(("arbitrary", "arbitrary")),
        name="dft_stage2",
    )(p, m2, g_fft)
    return y.reshape(bt, n, fw)


def _outproj_kernel(hs_ref, gl_ref, ysc_ref, yfft_ref, x_ref, wout_ref, glru_ref, gate_ref,
                    sh_ref, sc_ref, g2_ref, wr_ref, xn_ref, h2_ref, aff_ref, woutb_ref):
    @pl.when((pl.program_id(0) == 0) & (pl.program_id(1) == 0))
    def _():
        woutb_ref[...] = wout_ref[...].astype(BF16)

    y_lru = _rms(hs_ref[0] * gl_ref[0], glru_ref[...])
    y = jnp.concatenate([y_lru.astype(BF16), ysc_ref[0].astype(BF16), yfft_ref[0].astype(BF16)], axis=-1)
    xn = x_ref[0] + gate_ref[0] * _dot(y, woutb_ref[...])
    xn_ref[0] = xn
    h2 = _rms(xn, g2_ref[...]) * (1.0 + sc_ref[0]) + sh_ref[0]
    for k in range(h2.shape[1] // LANES):
        h2_ref[0, pl.ds(k, h2.shape[0], stride=h2.shape[1] // LANES), :] = h2[:, k * LANES:(k + 1) * LANES]
    logits = _dot_nt(wr_ref[...], h2, precision=HI)
    m = jnp.max(logits, axis=0, keepdims=True)
    e = jnp.exp(logits - m)
    aff_ref[0] = e / jnp.sum(e, axis=0, keepdims=True)


def _outproj_call(hs, gl, ysc, yfft, x, w_out, layer, g_lru, gate, shift, scale, g2, w_router_t, *, tm):
    bt, n, d = x.shape
    ne = w_router_t.shape[0]
    tok = lambda w: pl.BlockSpec((1, tm, w), lambda b, i: (b, i, 0))
    vec = lambda w: pl.BlockSpec((1, 1, w), lambda b, i: (b, 0, 0))
    full = lambda *s: pl.BlockSpec(s, lambda b, i: (0,) * len(s))
    return pl.pallas_call(
        _outproj_kernel,
        grid=(bt, n // tm),
        in_specs=[tok(hs.shape[2]), tok(gl.shape[2]), tok(ysc.shape[2]), tok(yfft.shape[2]), tok(d),
                  pl.BlockSpec((None,) + w_out.shape[1:], lambda b, i: (layer, 0, 0), pipeline_mode=pl.Buffered(1)),
                  full(1, hs.shape[2]), vec(d), vec(d), vec(d), full(1, d), full(ne, d)],
        out_specs=[tok(d), pl.BlockSpec((1, tm * d // LANES, LANES), lambda b, i: (b, i, 0)),
                   pl.BlockSpec((1, ne, tm), lambda b, i: (b, 0, i))],
        out_shape=[jax.ShapeDtypeStruct((bt, n, d), F32), jax.ShapeDtypeStruct((bt, n * d // LANES, LANES), F32),
                   jax.ShapeDtypeStruct((bt, ne, n), F32)],
        scratch_shapes=[pltpu.VMEM(w_out.shape[1:], BF16)],
        compiler_params=_cparams(("arbitrary", "arbitrary")),
        name="outproj_router",
    )(hs, gl, ysc, yfft, x, w_out, g_lru, gate, shift, scale, g2, w_router_t)


def _select_kernel(aff_ref, idx_ref, g_ref, *, cap):
    aff = aff_ref[0]
    ne, nb, _ = aff.shape

    def count(mask):
        c = jnp.sum(mask.astype(jnp.int32), axis=1, keepdims=True)
        return jnp.sum(c, axis=2, keepdims=True)

    def bit_step(i, t):
        cand = t | (jnp.int32(1) << (30 - i))
        return jnp.where(count(aff >= lax.bitcast_convert_type(cand, F32)) >= cap, cand, t)

    bits = lax.fori_loop(0, 31, bit_step, jnp.zeros((ne, 1, 1), jnp.int32))
    lo = lax.bitcast_convert_type(bits, F32)
    hi = lax.bitcast_convert_type(bits + 1, F32)

    def mid_step(i, lh):
        lo, hi = lh
        mid = (lo + hi) * 0.5
        ok = count(aff >= mid) >= cap
        return jnp.where(ok, mid, lo), jnp.where(ok, hi, mid)

    lo, hi = lax.fori_loop(0, 30, mid_step, (lo, hi))
    gt = aff >= hi
    eq = (aff >= lo) & (aff < hi)
    need = (cap - count(gt)).astype(F32)

    lane = lax.broadcasted_iota(jnp.int32, (LANES, LANES), 0)
    lane_t = lax.broadcasted_iota(jnp.int32, (LANES, LANES), 1)
    tri_incl = (lane <= lane_t).astype(BF16)
    blk = lax.broadcasted_iota(jnp.int32, (nb, nb), 0)
    blk_t = lax.broadcasted_iota(jnp.int32, (nb, nb), 1)
    tri_blk = (blk <= blk_t).astype(BF16)
    ones_row = jnp.ones((SUBLANES, LANES), BF16)
    kvals = lax.broadcasted_iota(jnp.int32, (SUBLANES, nb), 1).astype(BF16)
    slot = lax.broadcasted_iota(jnp.int32, (cap, 1), 0).astype(F32)

    def block_prefix(mask_bf16):
        lc = _dot(mask_bf16, tri_incl)
        cnt_row = _dot_nt(ones_row, mask_bf16)
        inc_row = _dot(cnt_row.astype(BF16), tri_blk)
        return lc, inc_row - cnt_row, inc_row

    for e in range(ne):
        eq_e = eq[e].astype(BF16)
        lc_eq, off_eq, _ = block_prefix(eq_e)
        tri_strict = (blk_t < blk).astype(BF16)
        cnt_col = lc_eq[:, LANES - 1:LANES]
        before = _dot(tri_strict, jnp.broadcast_to(cnt_col, (nb, LANES)).astype(BF16))[:, :1]
        rank = lc_eq - eq[e].astype(F32) + before
        sel = gt[e] | (eq[e] & (rank < need[e]))
        sel_bf = sel.astype(BF16)

        lc, off_row, inc_row = block_prefix(sel_bf)
        off1 = off_row[0:1]
        inc1 = inc_row[0:1]
        onehot = ((slot >= off1) & (slot < inc1))
        oh_bf = onehot.astype(BF16)
        offk = jnp.sum(jnp.where(onehot, off1, 0.0), axis=1, keepdims=True)
        jl = slot - offk
        m = _dot(oh_bf, lc.astype(BF16))
        below = (m <= jl)
        kb_row = _dot_nt(kvals, oh_bf)
        r_row = _dot_nt(ones_row, below.astype(BF16))
        idx_ref[0, e:e + 1, :] = (kb_row[0:1] * float(LANES) + r_row[0:1]).astype(jnp.int32)

        msel = _dot(oh_bf, sel_bf)
        hit = (m == jl + 1.0) & (msel > 0.5)
        a_hi, a_mid = _split(aff[e])
        a_lo = (aff[e] - a_hi.astype(F32) - a_mid.astype(F32)).astype(BF16)
        aff_rows = _dot(oh_bf, a_hi) + (_dot(oh_bf, a_mid) + _dot(oh_bf, a_lo))
        g_ref[0, e] = jnp.sum(jnp.where(hit, aff_rows, 0.0), axis=1, keepdims=True)


def _select_call(aff_t, cap):
    bt, ne, n = aff_t.shape
    nb = n // LANES
    return pl.pallas_call(
        functools.partial(_select_kernel, cap=cap),
        grid=(bt,),
        in_specs=[pl.BlockSpec((1, ne, nb, LANES), lambda b: (b, 0, 0, 0))],
        out_specs=[pl.BlockSpec((1, ne, cap), lambda b: (b, 0, 0)),
                   pl.BlockSpec((1, ne, cap, 1), lambda b: (b, 0, 0, 0))],
        out_shape=[jax.ShapeDtypeStruct((bt, ne, cap), jnp.int32),
                   jax.ShapeDtypeStruct((bt, ne, cap, 1), F32)],
        compiler_params=_cparams(("parallel",)),
        name="expert_select",
    )(aff_t.reshape(bt, ne, nb, LANES))


def _ffn_kernel(idx_ref, idxn_ref, h_hbm, g_ref, gate2_ref, wg_ref, wu_ref, wd_ref, o_ref,
                xs_ref, wgb_ref, wub_ref, wdb_ref, sem, *, cap, chunk, gs, d):
    e = pl.program_id(0)
    bb = pl.program_id(1)
    nbb = pl.num_programs(1)
    step = e * nbb + bb
    slot = step % 2
    rows = gs * cap
    tile = d // LANES

    def row_copy(ids_ref, b0, dst_slot, s, j):
        tok = pl.multiple_of(ids_ref[s, 0, 0, j] * tile, tile)
        return pltpu.make_async_copy(h_hbm.at[b0 + s, pl.ds(tok, tile), :],
                                     xs_ref.at[dst_slot, pl.ds((s * cap + j) * tile, tile), :], sem.at[dst_slot])

    def wait_slot(s):
        pltpu.make_async_copy(h_hbm.at[0, pl.ds(0, rows * tile), :], xs_ref.at[s], sem.at[s]).wait()

    @pl.when(step == 0)
    def _():
        for s in range(gs):
            def start(j, carry):
                row_copy(idx_ref, bb * gs, slot, s, j).start()
                return carry
            lax.fori_loop(0, cap, start, 0, unroll=8)

    @pl.when(bb == 0)
    def _():
        wgb_ref[...] = wg_ref[0, 0].astype(BF16)
        wub_ref[...] = wu_ref[0, 0].astype(BF16)
        wdb_ref[...] = wd_ref[0, 0].astype(BF16)

    wait_slot(slot)

    bn = ((bb + 1) % nbb) * gs
    for c in range(rows // chunk):
        for r in range(c * chunk, (c + 1) * chunk):
            row_copy(idxn_ref, bn, 1 - slot, r // cap, r % cap).start()
        pieces = [xs_ref[slot, pl.ds(c * chunk * tile + k, chunk, stride=tile), :].astype(BF16) for k in range(tile)]
        xb = jnp.concatenate(pieces, axis=-1)
        gate = _dot(xb, wgb_ref[...])
        up = _dot(xb, wub_ref[...])
        hid = (gate * jax.nn.sigmoid(gate)) * up
        y = _dot(hid.astype(BF16), wdb_ref[...])
        s, r0 = (c * chunk) // cap, (c * chunk) % cap
        o_ref[s, 0, r0:r0 + chunk, :] = (y * g_ref[s, 0, r0:r0 + chunk, :]) * gate2_ref[s]

    @pl.when(step == pl.num_programs(0) * nbb - 1)
    def _():
        wait_slot(1 - slot)


def _ffn_call(idx, h2t, g, gate2, wg, wu, wd, layer, *, gs):
    bt, ne, cap = idx.shape
    d = gate2.shape[2]
    ff = wg.shape[3]
    nbb = bt // gs
    chunk = min(cap, 256)
    wspec = lambda s: pl.BlockSpec((1, 1) + s, lambda e, b: (layer, e, 0, 0))
    idx4 = idx.reshape(bt, ne, 1, cap)
    nxt = lambda e, b: ((b + 1) % nbb, jnp.minimum(e + (b + 1) // nbb, ne - 1), 0, 0)
    return pl.pallas_call(
        functools.partial(_ffn_kernel, cap=cap, chunk=chunk, gs=gs, d=d),
        grid=(ne, nbb),
        in_specs=[pl.BlockSpec((gs, 1, 1, cap), lambda e, b: (b, e, 0, 0), memory_space=pltpu.SMEM),
                  pl.BlockSpec((gs, 1, 1, cap), nxt, memory_space=pltpu.SMEM),
                  pl.BlockSpec(memory_space=pl.ANY),
                  pl.BlockSpec((gs, 1, cap, 1), lambda e, b: (b, e, 0, 0)),
                  pl.BlockSpec((gs, 1, d), lambda e, b: (b, 0, 0)),
                  wspec((d, ff)), wspec((d, ff)), wspec((ff, d))],
        out_specs=pl.BlockSpec((gs, 1, cap, d), lambda e, b: (b, e, 0, 0)),
        out_shape=jax.ShapeDtypeStruct((bt, ne, cap, d), F32),
        scratch_shapes=[pltpu.VMEM((2, gs * cap * d // LANES, LANES), F32), pltpu.VMEM((d, ff), BF16),
                        pltpu.VMEM((d, ff), BF16), pltpu.VMEM((ff, d), BF16), pltpu.SemaphoreType.DMA((2,))],
        compiler_params=_cparams(("arbitrary", "arbitrary")),
        name="expert_ffn",
    )(idx4, idx4, h2t, g, gate2, wg, wu, wd)


def _combine_kernel(idx_ref, x_hbm, z_ref, gfin_ref, o_hbm, acc_ref, sem, *, cap, final_norm, norm_chunk):
    b = pl.program_id(0)
    e = pl.program_id(1)

    @pl.when(e == 0)
    def _():
        cp = pltpu.make_async_copy(x_hbm.at[b], acc_ref, sem)
        cp.start()
        cp.wait()

    group = 4
    for j0 in range(0, cap, group):
        toks = [idx_ref[0, 0, 0, j0 + u] for u in range(group)]
        vals = [acc_ref[pl.ds(toks[u], 1), :] + z_ref[0, 0, j0 + u:j0 + u + 1, :] for u in range(group)]
        for u in range(group):
            acc_ref[pl.ds(toks[u], 1), :] = vals[u]

    @pl.when(e == pl.num_programs(1) - 1)
    def _():
        if final_norm:
            def norm(i, carry):
                rs = pl.ds(pl.multiple_of(i * norm_chunk, norm_chunk), norm_chunk)
                acc_ref[rs, :] = _rms(acc_ref[rs, :], gfin_ref[...])
                return carry
            lax.fori_loop(0, acc_ref.shape[0] // norm_chunk, norm, 0)
        cp = pltpu.make_async_copy(acc_ref, o_hbm.at[b], sem)
        cp.start()
        cp.wait()


def _combine_call(idx, x, z, g_final, *, final_norm):
    bt, ne, cap = idx.shape
    _, n, d = x.shape
    return pl.pallas_call(
        functools.partial(_combine_kernel, cap=cap, final_norm=final_norm, norm_chunk=min(n, 256)),
        grid=(bt, ne),
        in_specs=[pl.BlockSpec((1, 1, 1, cap), lambda b, e: (b, e, 0, 0), memory_space=pltpu.SMEM),
                  pl.BlockSpec(memory_space=pl.ANY),
                  pl.BlockSpec((1, 1, cap, d), lambda b, e: (b, e, 0, 0)),
                  pl.BlockSpec((1, d), lambda b, e: (0, 0))],
        out_specs=pl.BlockSpec(memory_space=pl.ANY),
        out_shape=jax.ShapeDtypeStruct((bt, n, d), F32),
        scratch_shapes=[pltpu.VMEM((n, d), F32), pltpu.SemaphoreType.DMA(())],
        compiler_params=_cparams(("arbitrary", "arbitrary")),
        name="expert_combine",
    )(idx.reshape(bt, ne, 1, cap), x, z, g_final)


def _block_diag(w):
    h, hd, _ = w.shape
    eye = jnp.eye(h, dtype=w.dtype)
    return (w[:, :, None, :] * eye[:, None, :, None]).reshape(h * hd, h * hd)


def _gate_weights(wr, br, wi, bi):
    dense = [_block_diag(wr[0]), _block_diag(wi[0]), _block_diag(wr[1]), _block_diag(wi[1])]
    bias = [br[0], bi[0], br[1], bi[1]]
    lru_w = dense[0].shape[0]
    wg, bg = [], []
    for g in range(lru_w // GATE_GROUP):
        cs = slice(g * GATE_GROUP, (g + 1) * GATE_GROUP)
        wg.append(jnp.concatenate([m[cs, cs] for m in dense], axis=1))
        bg.append(jnp.concatenate([v[cs] for v in bias])[None, :])
    return jnp.stack(wg).astype(BF16), jnp.stack(bg)


def _channel_dft(fw):
    gd = fw // FFT_GROUPS
    c, s = _cos_sin(gd)
    eye = np.eye(FFT_GROUPS)
    return _split_const(np.concatenate([np.kron(eye, c), np.kron(eye, s)], axis=1))


def _mixer(x, lp, shift, scale, h0_f, h0_b, *, row_w, tm, tb, need_out):
    af, bf, ab, bb, gl, ysc, ucs = _inproj_call(
        x, shift, scale, lp["g1"], lp["w_in"], lp["layer"], lp["cw"], lp["cb"], lp["wg"], lp["bg"], lp["lam"],
        lp["scw"], lp["g_sc"], lp["dftc"], row_w=row_w, tm=tm)
    hf, sf = _scan_call(af, bf, h0_f, None, reverse=False, tb=tb)
    hs, sb = _scan_call(ab, bb, h0_b, hf, reverse=True, tb=tb)
    if not need_out:
        return sf, sb, None
    yfft = _fourier_call(ucs, lp["g_fft"])
    return sf, sb, (hs, gl, ysc, yfft)


def _moe(x, h2, aff_t, gate2, lp, g_final, *, final_norm, gs):
    bt, n, d = x.shape
    cap = CAPACITY_FACTOR * n // N_EXPERTS
    pad = (-n) % (SUBLANES * LANES)
    if pad:
        aff_t = jnp.pad(aff_t, ((0, 0), (0, 0), (0, pad)), constant_values=-1.0)
    idx, g = _select_call(aff_t, cap)
    z = _ffn_call(idx, h2, g, gate2, lp["wge"], lp["wue"], lp["wde"], lp["layer"], gs=gs)
    return _combine_call(idx, x, z, g_final, final_norm=final_norm)


def kernel(x, c, ctx, c_ctx, w_ada, b_ada, g_norm1, w_in, lru_conv_w, lru_conv_b, lru_wr, lru_br, lru_wi,
           lru_bi, lru_lam, sc_conv_w, g_out, w_out, g_norm2, w_router, w_gate_e, w_up_e, w_down_e, g_final):
    depth = w_ada.shape[0]
    bsz, seq, d = x.shape
    ctx_len = ctx.shape[1]
    lru_w = lru_conv_w.shape[2]
    conv_w = sc_conv_w.shape[2]
    fft_w = w_in.shape[2] - 2 * lru_w - 3 * conv_w

    cs = jnp.concatenate([c, c_ctx[None, :], jnp.zeros((SUBLANES - bsz - 1, d), F32)], axis=0)
    mods = _ada_call(cs, w_ada, b_ada)
    dftc = _channel_dft(fft_w)
    gfin = g_final[None, :]
    zero_state = jnp.zeros((bsz, 1, lru_w), F32)

    for l in range(depth):
        last = l == depth - 1
        mx = [mods[l, :bsz, None, k * d:(k + 1) * d] for k in range(6)]
        mc = [jnp.broadcast_to(mods[l, bsz, k * d:(k + 1) * d], (bsz, 1, d)) for k in range(6)]
        wg, bg = _gate_weights(lru_wr[l], lru_br[l], lru_wi[l], lru_bi[l])
        lp = dict(
            g1=g_norm1[l][None, :], w_in=w_in, cw=lru_conv_w[l], cb=lru_conv_b[l][None, :],
            wg=wg, bg=bg, lam=lru_lam[l].reshape(1, 2 * lru_w), scw=sc_conv_w[l],
            g_sc=g_out[l][None, lru_w:lru_w + conv_w], g_fft=g_out[l][None, lru_w + conv_w:], dftc=dftc,
            wge=w_gate_e, wue=w_up_e, wde=w_down_e, layer=l)
        g_lru = g_out[l][None, :lru_w]
        w_router_t = w_router[l].T

        sf, sb, parts = _mixer(ctx, lp, mc[0], mc[1], zero_state, zero_state,
                               row_w=ctx_len, tm=ctx_len, tb=ctx_len, need_out=not last)
        if not last:
            ctx, hc2, aff_c = _outproj_call(*parts, ctx, w_out, l, g_lru, mc[2], mc[3], mc[4],
                                            g_norm2[l][None, :], w_router_t, tm=ctx_len)
            ctx = _moe(ctx, hc2, aff_c, mc[5], lp, gfin, final_norm=False, gs=bsz)

        _, _, parts = _mixer(x, lp, mx[0], mx[1], sf, sb, row_w=GRID_W, tm=512, tb=512, need_out=True)
        x, hx2, aff_x = _outproj_call(*parts, x, w_out, l, g_lru, mx[2], mx[3], mx[4],
                                      g_norm2[l][None, :], w_router_t, tm=512)
        x = _moe(x, hx2, aff_x, mx[5], lp, gfin, final_norm=last, gs=1)
    return x
```

```python
import functools
import math

import numpy as np
import jax
import jax.numpy as jnp
from jax import lax
from jax.experimental import pallas as pl
from jax.experimental.pallas import tpu as pltpu

F32 = jnp.float32
BF16 = jnp.bfloat16
HI = lax.Precision.HIGHEST

GRID_W = 64
LRU_HEADS = 8
LRU_C = 8.0
N_EXPERTS = 16
CAPACITY_FACTOR = 2
EPS = 1e-6
FFT_GROUPS = 4

LANES = 128
SUBLANES = 8
GATE_GROUP = 256
VMEM_LIMIT = 56 * 1024 * 1024


def _cparams(sem):
    return pltpu.CompilerParams(dimension_semantics=sem, vmem_limit_bytes=VMEM_LIMIT)


def _rms(x, g):
    return x * lax.rsqrt(jnp.mean(x * x, axis=-1, keepdims=True) + EPS) * g


def _dot(a, b):
    return jnp.dot(a, b, preferred_element_type=F32)


def _dot_hi(a, b):
    return jnp.dot(a, b, preferred_element_type=F32, precision=HI)


def _split_const(m):
    m = jnp.asarray(m, F32)
    hi = m.astype(BF16)
    return jnp.stack([hi, (m - hi.astype(F32)).astype(BF16)])


def _split(x):
    hi = x.astype(BF16)
    return hi, (x - hi.astype(F32)).astype(BF16)


def _dot3_const_lhs(m_ref, x):
    x_hi, x_lo = _split(x)
    return _dot(m_ref[0], x_hi) + (_dot(m_ref[1], x_hi) + _dot(m_ref[0], x_lo))


def _dot3_const_rhs(x, m_ref):
    x_hi, x_lo = _split(x)
    return _dot(x_hi, m_ref[0]) + (_dot(x_hi, m_ref[1]) + _dot(x_lo, m_ref[0]))


def _dot_nt(a, b, precision=None):
    return lax.dot_general(a, b, (((1,), (1,)), ((), ())), preferred_element_type=F32, precision=precision)


def _ada_kernel(c_ref, w_ref, b_ref, o_ref):
    c = c_ref[...]
    o_ref[...] = _dot_hi(c * jax.nn.sigmoid(c), w_ref[...]) + b_ref[...]


def _ada_call(cs, w_ada, b_ada):
    depth, d, six_d = w_ada.shape
    nblk = six_d // d
    return pl.pallas_call(
        _ada_kernel,
        grid=(depth, nblk),
        in_specs=[
            pl.BlockSpec((SUBLANES, d), lambda l, j: (0, 0)),
            pl.BlockSpec((None, d, d), lambda l, j: (l, 0, j)),
            pl.BlockSpec((None, 1, d), lambda l, j: (l, 0, j)),
        ],
        out_specs=pl.BlockSpec((None, SUBLANES, d), lambda l, j: (l, 0, j)),
        out_shape=jax.ShapeDtypeStruct((depth, SUBLANES, six_d), F32),
        compiler_params=_cparams(("parallel", "parallel")),
        name="ada_mod",
    )(cs, w_ada, b_ada.reshape(depth, 1, six_d))


def _shift_rows(u, d, pos, row_w):
    n = u.shape[0]
    if d == 0:
        return u
    rolled = pltpu.roll(u, (-d) % n, axis=0)
    valid = (pos + d >= 0) & (pos + d < row_w)
    return jnp.where(valid, rolled, 0.0)


def _conv_rows(u, w_ref, left, pos, row_w):
    out = None
    for k in range(w_ref.shape[0]):
        term = w_ref[k:k + 1, :] * _shift_rows(u, k - left, pos, row_w)
        out = term if out is None else out + term
    return out


def _inproj_kernel(x_ref, sh_ref, sc_ref, g1_ref, win_ref, cw_ref, cb_ref, wg_ref, bg_ref, lam_ref,
                   scw_ref, gsc_ref, dft_ref,
                   af_ref, bf_ref, ab_ref, bb_ref, gl_ref, ysc_ref, ucs_ref, winb_ref, *, row_w, lru_w, conv_w):
    @pl.when((pl.program_id(0) == 0) & (pl.program_id(1) == 0))
    def _():
        winb_ref[...] = win_ref[...].astype(BF16)

    x = x_ref[0]
    tm = x.shape[0]
    h = _rms(x, g1_ref[...]) * (1.0 + sc_ref[0]) + sh_ref[0]
    p = _dot(h.astype(BF16), winb_ref[...])
    pos = lax.broadcasted_iota(jnp.int32, (tm, 1), 0) % row_w

    u = _conv_rows(p[:, :lru_w], cw_ref, 1, pos, row_w) + cb_ref[...]
    lam = lam_ref[...]
    nl = -lam
    softplus = jnp.maximum(nl, 0.0) + jnp.log(1.0 + jnp.exp(-jnp.abs(nl)))
    out_refs = ((af_ref, bf_ref), (ab_ref, bb_ref))
    for g in range(lru_w // GATE_GROUP):
        cs = slice(g * GATE_GROUP, (g + 1) * GATE_GROUP)
        ug = u[:, cs]
        z = _dot(ug.astype(BF16), wg_ref[g]) + bg_ref[g]
        for d in range(2):
            r = jax.nn.sigmoid(z[:, (2 * d) * GATE_GROUP:(2 * d + 1) * GATE_GROUP])
            i = jax.nn.sigmoid(z[:, (2 * d + 1) * GATE_GROUP:(2 * d + 2) * GATE_GROUP])
            sp = softplus[:, d * lru_w + g * GATE_GROUP: d * lru_w + (g + 1) * GATE_GROUP]
            a = jnp.exp(-LRU_C * r * sp)
            out_refs[d][0][0, :, cs] = a
            out_refs[d][1][0, :, cs] = jnp.sqrt(1.0 - a * a) * (i * ug)

    gl_ref[0] = jax.nn.gelu(p[:, lru_w:2 * lru_w], approximate=True)

    o = 2 * lru_w
    sc_b = p[:, o:o + conv_w]
    sc_c = p[:, o + conv_w:o + 2 * conv_w]
    sc_x = p[:, o + 2 * conv_w:o + 3 * conv_w]
    y_sc = sc_b * _conv_rows(sc_c * sc_x, scw_ref, 1, pos, row_w)
    ysc_ref[0] = _rms(y_sc, gsc_ref[...])

    ucs_ref[0] = _dot3_const_rhs(p[:, o + 3 * conv_w:], dft_ref)


def _inproj_call(x, shift, scale, g1, w_in, layer, cw, cb, wg, bg, lam, scw, gsc, dftc, *, row_w, tm):
    bt, n, d = x.shape
    in_cols = w_in.shape[2]
    lru_w = cw.shape[1]
    conv_w = scw.shape[1]
    fft_w = in_cols - 2 * lru_w - 3 * conv_w
    ng = lru_w // GATE_GROUP
    tok = lambda w: pl.BlockSpec((1, tm, w), lambda b, i: (b, i, 0))
    vec = lambda w: pl.BlockSpec((1, 1, w), lambda b, i: (b, 0, 0))
    full = lambda *s: pl.BlockSpec(s, lambda b, i: (0,) * len(s))
    shp = lambda w: jax.ShapeDtypeStruct((bt, n, w), F32)
    return pl.pallas_call(
        functools.partial(_inproj_kernel, row_w=row_w, lru_w=lru_w, conv_w=conv_w),
        grid=(bt, n // tm),
        in_specs=[tok(d), vec(d), vec(d), full(1, d),
                  pl.BlockSpec((None, d, in_cols), lambda b, i: (layer, 0, 0), pipeline_mode=pl.Buffered(1)),
                  full(*cw.shape), full(1, lru_w),
                  full(ng, GATE_GROUP, 4 * GATE_GROUP), full(ng, 1, 4 * GATE_GROUP), full(1, 2 * lru_w),
                  full(*scw.shape), full(1, conv_w), full(2, fft_w, 2 * fft_w)],
        out_specs=[tok(lru_w)] * 5 + [tok(conv_w), tok(2 * fft_w)],
        out_shape=[shp(lru_w)] * 5 + [shp(conv_w), shp(2 * fft_w)],
        scratch_shapes=[pltpu.VMEM((d, in_cols), BF16)],
        compiler_params=_cparams(("arbitrary", "arbitrary")),
        name="inproj_local",
    )(x, shift, scale, g1, w_in, cw, cb, wg, bg, lam, scw, gsc, dftc)


def _scan_kernel(*refs, reverse, add_other):
    if add_other:
        a_ref, b_ref, h0_ref, other_ref, h_ref, hl_ref, carry_ref = refs
    else:
        a_ref, b_ref, h0_ref, h_ref, hl_ref, carry_ref = refs
        other_ref = None

    @pl.when(pl.program_id(1) == 0)
    def _():
        carry_ref[...] = jnp.broadcast_to(h0_ref[0], carry_ref.shape)

    tb, w = a_ref.shape[1], a_ref.shape[2]
    ntile = tb // SUBLANES
    a = a_ref[0].reshape(ntile, SUBLANES, w)
    b = b_ref[0].reshape(ntile, SUBLANES, w)
    row = lax.broadcasted_iota(jnp.int32, (1, SUBLANES, 1), 1)
    for s in (1, 2, 4):
        if reverse:
            a_sh = pltpu.roll(a, SUBLANES - s, axis=1)
            b_sh = pltpu.roll(b, SUBLANES - s, axis=1)
            valid = row < SUBLANES - s
        else:
            a_sh = pltpu.roll(a, s, axis=1)
            b_sh = pltpu.roll(b, s, axis=1)
            valid = row >= s
        b = jnp.where(valid, b + a * b_sh, b)
        a = jnp.where(valid, a * a_sh, a)
    carry = carry_ref[...]
    order = range(ntile - 1, -1, -1) if reverse else range(ntile)
    edge = 0 if reverse else SUBLANES - 1
    for t in order:
        rs = slice(t * SUBLANES, (t + 1) * SUBLANES)
        h = b[t] + a[t] * carry
        if other_ref is not None:
            h_ref[0, rs, :] = h + other_ref[0, rs, :]
        else:
            h_ref[0, rs, :] = h
        carry = jnp.broadcast_to(h[edge:edge + 1], h.shape)
    carry_ref[...] = carry
    hl_ref[0] = carry[0:1]


def _scan_call(a, b, h0, other, *, reverse, tb):
    bt, n, w = a.shape
    nt = n // tb
    blk = (lambda bi, i: (bi, nt - 1 - i, 0)) if reverse else (lambda bi, i: (bi, i, 0))
    tok = pl.BlockSpec((1, tb, w), blk)
    st = pl.BlockSpec((1, 1, w), lambda bi, i: (bi, 0, 0))
    ins = [a, b, h0] + ([other] if other is not None else [])
    in_specs = [tok, tok, st] + ([tok] if other is not None else [])
    return pl.pallas_call(
        functools.partial(_scan_kernel, reverse=reverse, add_other=other is not None),
        grid=(bt, nt),
        in_specs=in_specs,
        out_specs=[tok, st],
        out_shape=[jax.ShapeDtypeStruct((bt, n, w), F32), jax.ShapeDtypeStruct((bt, 1, w), F32)],
        scratch_shapes=[pltpu.VMEM((SUBLANES, w), F32)],
        compiler_params=_cparams(("parallel", "arbitrary")),
        name="lru_scan_bwd" if reverse else "lru_scan_fwd",
    )(*ins)


def _cos_sin(n):
    k = np.arange(n, dtype=np.float64)
    ang = 2.0 * np.pi * np.outer(k, k) / n
    return np.cos(ang), np.sin(ang)


def _slab_copies(src_hbm, dst_ref, sem, b, first, count, width, slot):
    return [pltpu.make_async_copy(src_hbm.at[b, :, first + j, :], dst_ref.at[slot, :, pl.ds(j * width, width)],
                                  sem.at[slot]) for j in range(count)]


def _prefetch_slabs(src_hbm, dst_ref, sem, count, width):
    b, i = pl.program_id(0), pl.program_id(1)
    ni = pl.num_programs(1)
    step = b * ni + i
    slot = step % 2

    @pl.when(step == 0)
    def _():
        for cp in _slab_copies(src_hbm, dst_ref, sem, b, i * count, count, width, slot):
            cp.start()

    @pl.when(step + 1 < pl.num_programs(0) * ni)
    def _():
        wrap = i + 1 == ni
        bn = jnp.where(wrap, b + 1, b)
        nxt = jnp.where(wrap, 0, i + 1)
        for cp in _slab_copies(src_hbm, dst_ref, sem, bn, nxt * count, count, width, 1 - slot):
            cp.start()

    for cp in _slab_copies(src_hbm, dst_ref, sem, b, i * count, count, width, slot):
        cp.wait()
    return step, slot


def _dft1_kernel(x_hbm, m_ref, tc_ref, ts_ref, o_ref, xs_ref, sem, *, n2, jn, fw):
    _, slot = _prefetch_slabs(x_hbm, xs_ref, sem, jn, 2 * fw)
    r = _dot3_const_lhs(m_ref, xs_ref[slot])
    for j in range(jn):
        base = j * 2 * fw
        c_uc = r[:n2, base:base + fw]
        c_us = r[:n2, base + fw:base + 2 * fw]
        s_uc = r[n2:, base:base + fw]
        s_us = r[n2:, base + fw:base + 2 * fw]
        br = c_uc - s_us
        bi = -(c_us + s_uc)
        tc = tc_ref[0, :, j:j + 1]
        ts = ts_ref[0, :, j:j + 1]
        o_ref[0, j, :, :fw] = br * tc + bi * ts
        o_ref[0, j, :, fw:] = bi * tc - br * ts


def _dft2_kernel(p_hbm, m_ref, g_ref, y_hbm, ps_ref, ys_ref, sem_in, sem_out, *, n1, kn, fw):
    step, slot = _prefetch_slabs(p_hbm, ps_ref, sem_in, kn, 2 * fw)
    b, i = pl.program_id(0), pl.program_id(1)
    last = pl.num_programs(0) * pl.num_programs(1) - 1

    def out_copies(s):
        return [pltpu.make_async_copy(ys_ref.at[s, :, pl.ds(k * fw, fw)], y_hbm.at[b, :, i * kn + k, :],
                                      sem_out.at[s]) for k in range(kn)]

    r = _dot3_const_lhs(m_ref, ps_ref[slot])

    @pl.when(step >= 2)
    def _():
        for cp in out_copies(slot):
            cp.wait()

    for k in range(kn):
        base = k * 2 * fw
        y = r[:n1, base:base + fw] + r[n1:, base + fw:base + 2 * fw]
        ys_ref[slot, :, k * fw:(k + 1) * fw] = _rms(y, g_ref[...])
    for cp in out_copies(slot):
        cp.start()

    @pl.when(step == last)
    def _():
        for cp in out_copies(slot):
            cp.wait()

    @pl.when((step == last) & (step >= 1))
    def _():
        for cp in out_copies(1 - slot):
            cp.wait()


def _dft_direct_kernel(x_ref, m_ref, g_ref, o_ref, *, n, fw):
    x = x_ref[0]
    r = _dot3_const_lhs(m_ref, x)
    y = r[:n, :fw] - r[n:, fw:]
    o_ref[0] = _rms(y, g_ref[...])


def _fourier_call(ucs, g_fft):
    bt, n, fw2 = ucs.shape
    fw = fw2 // 2
    gd = fw // FFT_GROUPS
    scale = 1.0 / math.sqrt(n * gd)
    if n <= 512:
        c, s = _cos_sin(n)
        m = _split_const(np.concatenate([c, s], 0) * scale)
        return pl.pallas_call(
            functools.partial(_dft_direct_kernel, n=n, fw=fw),
            grid=(bt,),
            in_specs=[pl.BlockSpec((1, n, fw2), lambda b: (b, 0, 0)),
                      pl.BlockSpec((2, 2 * n, n), lambda b: (0, 0, 0)),
                      pl.BlockSpec((1, fw), lambda b: (0, 0))],
            out_specs=pl.BlockSpec((1, n, fw), lambda b: (b, 0, 0)),
            out_shape=jax.ShapeDtypeStruct((bt, n, fw), F32),
            compiler_params=_cparams(("parallel",)),
            name="dft_direct",
        )(ucs, m, g_fft)

    n1 = LANES
    n2 = n // n1
    jn = 16
    c2, s2 = _cos_sin(n2)
    m1 = _split_const(np.concatenate([c2, s2], 0))
    ang = 2.0 * np.pi * np.outer(np.arange(n2), np.arange(n1)) / n
    tw = lambda f: jnp.asarray(f(ang).reshape(n2, n1 // jn, jn).transpose(1, 0, 2), F32)
    p = pl.pallas_call(
        functools.partial(_dft1_kernel, n2=n2, jn=jn, fw=fw),
        grid=(bt, n1 // jn),
        in_specs=[pl.BlockSpec(memory_space=pl.ANY),
                  pl.BlockSpec((2, 2 * n2, n2), lambda b, i: (0, 0, 0)),
                  pl.BlockSpec((1, n2, jn), lambda b, i: (i, 0, 0)),
                  pl.BlockSpec((1, n2, jn), lambda b, i: (i, 0, 0))],
        out_specs=pl.BlockSpec((1, jn, n2, fw2), lambda b, i: (b, i, 0, 0)),
        out_shape=jax.ShapeDtypeStruct((bt, n1, n2, fw2), F32),
        scratch_shapes=[pltpu.VMEM((2, n2, jn * fw2), F32), pltpu.SemaphoreType.DMA((2,))],
        compiler_params=_cparams(("arbitrary", "arbitrary")),
        name="dft_stage1",
    )(ucs.reshape(bt, n2, n1, fw2), m1, tw(np.cos), tw(np.sin))

    kn = min(8, n2)
    c1, s1 = _cos_sin(n1)
    m2 = _split_const(np.concatenate([c1, s1], 0) * scale)
    y = pl.pallas_call(
        functools.partial(_dft2_kernel, n1=n1, kn=kn, fw=fw),
        grid=(bt, n2 // kn),
        in_specs=[pl.BlockSpec(memory_space=pl.ANY),
                  pl.BlockSpec((2, 2 * n1, n1), lambda b, i: (0, 0, 0)),
                  pl.BlockSpec((1, fw), lambda b, i: (0, 0))],
        out_specs=pl.BlockSpec(memory_space=pl.ANY),
        out_shape=jax.ShapeDtypeStruct((bt, n1, n2, fw), F32),
        scratch_shapes=[pltpu.VMEM((2, n1, kn * fw2), F32), pltpu.VMEM((2, n1, kn * fw), F32),
                        pltpu.SemaphoreType.DMA((2,)), pltpu.SemaphoreType.DMA((2,))],
        compiler_params=_cparams(("arbitrary", "arbitrary")),
        name="dft_stage2",
    )(p, m2, g_fft)
    return y.reshape(bt, n, fw)


def _outproj_kernel(hs_ref, gl_ref, ysc_ref, yfft_ref, x_ref, wout_ref, glru_ref, gate_ref,
                    sh_ref, sc_ref, g2_ref, wr_ref, xn_ref, h2_ref, aff_ref, woutb_ref):
    @pl.when((pl.program_id(0) == 0) & (pl.program_id(1) == 0))
    def _():
        woutb_ref[...] = wout_ref[...].astype(BF16)

    y_lru = _rms(hs_ref[0] * gl_ref[0], glru_ref[...])
    y = jnp.concatenate([y_lru.astype(BF16), ysc_ref[0].astype(BF16), yfft_ref[0].astype(BF16)], axis=-1)
    xn = x_ref[0] + gate_ref[0] * _dot(y, woutb_ref[...])
    xn_ref[0] = xn
    h2 = _rms(xn, g2_ref[...]) * (1.0 + sc_ref[0]) + sh_ref[0]
    for k in range(h2.shape[1] // LANES):
        h2_ref[0, pl.ds(k, h2.shape[0], stride=h2.shape[1] // LANES), :] = h2[:, k * LANES:(k + 1) * LANES]
    h_hi, h_lo = _split(h2)
    logits = _dot_nt(wr_ref[0], h_hi) + (_dot_nt(wr_ref[1], h_hi) + _dot_nt(wr_ref[0], h_lo))
    m = jnp.max(logits, axis=0, keepdims=True)
    e = jnp.exp(logits - m)
    aff_ref[0] = e / jnp.sum(e, axis=0, keepdims=True)


def _outproj_call(hs, gl, ysc, yfft, x, w_out, layer, g_lru, gate, shift, scale, g2, w_router_t, *, tm):
    bt, n, d = x.shape
    ne = w_router_t.shape[1]
    tok = lambda w: pl.BlockSpec((1, tm, w), lambda b, i: (b, i, 0))
    vec = lambda w: pl.BlockSpec((1, 1, w), lambda b, i: (b, 0, 0))
    full = lambda *s: pl.BlockSpec(s, lambda b, i: (0,) * len(s))
    return pl.pallas_call(
        _outproj_kernel,
        grid=(bt, n // tm),
        in_specs=[tok(hs.shape[2]), tok(gl.shape[2]), tok(ysc.shape[2]), tok(yfft.shape[2]), tok(d),
                  pl.BlockSpec((None,) + w_out.shape[1:], lambda b, i: (layer, 0, 0), pipeline_mode=pl.Buffered(1)),
                  full(1, hs.shape[2]), vec(d), vec(d), vec(d), full(1, d), full(2, ne, d)],
        out_specs=[tok(d), pl.BlockSpec((1, tm * d // LANES, LANES), lambda b, i: (b, i, 0)),
                   pl.BlockSpec((1, ne, tm), lambda b, i: (b, 0, i))],
        out_shape=[jax.ShapeDtypeStruct((bt, n, d), F32), jax.ShapeDtypeStruct((bt, n * d // LANES, LANES), F32),
                   jax.ShapeDtypeStruct((bt, ne, n), F32)],
        scratch_shapes=[pltpu.VMEM(w_out.shape[1:], BF16)],
        compiler_params=_cparams(("arbitrary", "arbitrary")),
        name="outproj_router",
    )(hs, gl, ysc, yfft, x, w_out, g_lru, gate, shift, scale, g2, w_router_t)


def _select_kernel(aff_ref, idx_ref, g_ref, *, cap):
    aff = aff_ref[0]
    ne, nb, _ = aff.shape

    def count(mask):
        c = jnp.sum(mask.astype(jnp.int32), axis=1, keepdims=True)
        return jnp.sum(c, axis=2, keepdims=True)

    def bit_step(i, t):
        cand = t | (jnp.int32(1) << (30 - i))
        return jnp.where(count(aff >= lax.bitcast_convert_type(cand, F32)) >= cap, cand, t)

    bits = lax.fori_loop(0, 31, bit_step, jnp.zeros((ne, 1, 1), jnp.int32))
    lo = lax.bitcast_convert_type(bits, F32)
    hi = lax.bitcast_convert_type(bits + 1, F32)

    def mid_step(i, lh):
        lo, hi = lh
        mid = (lo + hi) * 0.5
        ok = count(aff >= mid) >= cap
        return jnp.where(ok, mid, lo), jnp.where(ok, hi, mid)

    lo, hi = lax.fori_loop(0, 30, mid_step, (lo, hi))
    gt = aff >= hi
    eq = (aff >= lo) & (aff < hi)
    need = (cap - count(gt)).astype(F32)

    lane = lax.broadcasted_iota(jnp.int32, (LANES, LANES), 0)
    lane_t = lax.broadcasted_iota(jnp.int32, (LANES, LANES), 1)
    tri_incl = (lane <= lane_t).astype(BF16)
    blk = lax.broadcasted_iota(jnp.int32, (nb, nb), 0)
    blk_t = lax.broadcasted_iota(jnp.int32, (nb, nb), 1)
    tri_blk = (blk <= blk_t).astype(BF16)
    ones_row = jnp.ones((SUBLANES, LANES), BF16)
    kvals = lax.broadcasted_iota(jnp.int32, (SUBLANES, nb), 1).astype(BF16)
    slot = lax.broadcasted_iota(jnp.int32, (cap, 1), 0).astype(F32)

    def block_prefix(mask_bf16):
        lc = _dot(mask_bf16, tri_incl)
        cnt_row = _dot_nt(ones_row, mask_bf16)
        inc_row = _dot(cnt_row.astype(BF16), tri_blk)
        return lc, inc_row - cnt_row, inc_row

    for e in range(ne):
        eq_e = eq[e].astype(BF16)
        lc_eq, off_eq, _ = block_prefix(eq_e)
        tri_strict = (blk_t < blk).astype(BF16)
        cnt_col = lc_eq[:, LANES - 1:LANES]
        before = _dot(tri_strict, jnp.broadcast_to(cnt_col, (nb, LANES)).astype(BF16))[:, :1]
        rank = lc_eq - eq[e].astype(F32) + before
        sel = gt[e] | (eq[e] & (rank < need[e]))
        sel_bf = sel.astype(BF16)

        lc, off_row, inc_row = block_prefix(sel_bf)
        off1 = off_row[0:1]
        inc1 = inc_row[0:1]
        onehot = ((slot >= off1) & (slot < inc1))
        oh_bf = onehot.astype(BF16)
        offk = jnp.sum(jnp.where(onehot, off1, 0.0), axis=1, keepdims=True)
        jl = slot - offk
        m = _dot(oh_bf, lc.astype(BF16))
        below = (m <= jl)
        kb_row = _dot_nt(kvals, oh_bf)
        r_row = _dot_nt(ones_row, below.astype(BF16))
        idx_ref[0, e:e + 1, :] = (kb_row[0:1] * float(LANES) + r_row[0:1]).astype(jnp.int32)

        msel = _dot(oh_bf, sel_bf)
        hit = (m == jl + 1.0) & (msel > 0.5)
        a_hi, a_mid = _split(aff[e])
        a_lo = (aff[e] - a_hi.astype(F32) - a_mid.astype(F32)).astype(BF16)
        aff_rows = _dot(oh_bf, a_hi) + (_dot(oh_bf, a_mid) + _dot(oh_bf, a_lo))
        g_ref[0, e] = jnp.sum(jnp.where(hit, aff_rows, 0.0), axis=1, keepdims=True)


def _select_call(aff_t, cap):
    bt, ne, n = aff_t.shape
    nb = n // LANES
    return pl.pallas_call(
        functools.partial(_select_kernel, cap=cap),
        grid=(bt,),
        in_specs=[pl.BlockSpec((1, ne, nb, LANES), lambda b: (b, 0, 0, 0))],
        out_specs=[pl.BlockSpec((1, ne, cap), lambda b: (b, 0, 0)),
                   pl.BlockSpec((1, ne, cap, 1), lambda b: (b, 0, 0, 0))],
        out_shape=[jax.ShapeDtypeStruct((bt, ne, cap), jnp.int32),
                   jax.ShapeDtypeStruct((bt, ne, cap, 1), F32)],
        compiler_params=_cparams(("parallel",)),
        name="expert_select",
    )(aff_t.reshape(bt, ne, nb, LANES))


def _ffn_kernel(idx_ref, idxn_ref, h_hbm, g_ref, gate2_ref, wg_ref, wu_ref, wd_ref, o_ref,
                xs_ref, wgb_ref, wub_ref, wdb_ref, sem, *, cap, chunk, gs, d):
    e = pl.program_id(0)
    bb = pl.program_id(1)
    nbb = pl.num_programs(1)
    step = e * nbb + bb
    slot = step % 2
    rows = gs * cap
    tile = d // LANES

    def row_copy(ids_ref, b0, dst_slot, s, j):
        tok = pl.multiple_of(ids_ref[s, 0, 0, j] * tile, tile)
        return pltpu.make_async_copy(h_hbm.at[b0 + s, pl.ds(tok, tile), :],
                                     xs_ref.at[dst_slot, pl.ds((s * cap + j) * tile, tile), :], sem.at[dst_slot])

    def wait_slot(s):
        pltpu.make_async_copy(h_hbm.at[0, pl.ds(0, rows * tile), :], xs_ref.at[s], sem.at[s]).wait()

    @pl.when(step == 0)
    def _():
        for s in range(gs):
            def start(j, carry):
                row_copy(idx_ref, bb * gs, slot, s, j).start()
                return carry
            lax.fori_loop(0, cap, start, 0, unroll=8)

    @pl.when(bb == 0)
    def _():
        wgb_ref[...] = wg_ref[0, 0].astype(BF16)
        wub_ref[...] = wu_ref[0, 0].astype(BF16)
        wdb_ref[...] = wd_ref[0, 0].astype(BF16)

    wait_slot(slot)

    bn = ((bb + 1) % nbb) * gs
    nch = rows // chunk
    issue_chunks = max(nch - 1, 1)
    for c in range(nch):
        for r in range(min(rows, -(-c * rows // issue_chunks)), min(rows, -(-(c + 1) * rows // issue_chunks))):
            row_copy(idxn_ref, bn, 1 - slot, r // cap, r % cap).start()
        pieces = [xs_ref[slot, pl.ds(c * chunk * tile + k, chunk, stride=tile), :].astype(BF16) for k in range(tile)]
        xb = jnp.concatenate(pieces, axis=-1)
        gate = _dot(xb, wgb_ref[...])
        up = _dot(xb, wub_ref[...])
        hid = (gate * jax.nn.sigmoid(gate)) * up
        y = _dot(hid.astype(BF16), wdb_ref[...])
        s, r0 = (c * chunk) // cap, (c * chunk) % cap
        o_ref[s, 0, r0:r0 + chunk, :] = (y * g_ref[s, 0, r0:r0 + chunk, :]) * gate2_ref[s]

    @pl.when(step == pl.num_programs(0) * nbb - 1)
    def _():
        wait_slot(1 - slot)


def _ffn_call(idx, h2t, g, gate2, wg, wu, wd, layer, *, gs):
    bt, ne, cap = idx.shape
    d = gate2.shape[2]
    ff = wg.shape[3]
    nbb = bt // gs
    chunk = min(cap, 256)
    wspec = lambda s: pl.BlockSpec((1, 1) + s, lambda e, b: (layer, e, 0, 0))
    idx4 = idx.reshape(bt, ne, 1, cap)
    nxt = lambda e, b: ((b + 1) % nbb, jnp.minimum(e + (b + 1) // nbb, ne - 1), 0, 0)
    return pl.pallas_call(
        functools.partial(_ffn_kernel, cap=cap, chunk=chunk, gs=gs, d=d),
        grid=(ne, nbb),
        in_specs=[pl.BlockSpec((gs, 1, 1, cap), lambda e, b: (b, e, 0, 0), memory_space=pltpu.SMEM),
                  pl.BlockSpec((gs, 1, 1, cap), nxt, memory_space=pltpu.SMEM),
                  pl.BlockSpec(memory_space=pl.ANY),
                  pl.BlockSpec((gs, 1, cap, 1), lambda e, b: (b, e, 0, 0)),
                  pl.BlockSpec((gs, 1, d), lambda e, b: (b, 0, 0)),
                  wspec((d, ff)), wspec((d, ff)), wspec((ff, d))],
        out_specs=pl.BlockSpec((gs, 1, cap, d), lambda e, b: (b, e, 0, 0)),
        out_shape=jax.ShapeDtypeStruct((bt, ne, cap, d), F32),
        scratch_shapes=[pltpu.VMEM((2, gs * cap * d // LANES, LANES), F32), pltpu.VMEM((d, ff), BF16),
                        pltpu.VMEM((d, ff), BF16), pltpu.VMEM((ff, d), BF16), pltpu.SemaphoreType.DMA((2,))],
        compiler_params=_cparams(("arbitrary", "arbitrary")),
        name="expert_ffn",
    )(idx4, idx4, h2t, g, gate2, wg, wu, wd)


def _combine_kernel(idx_ref, x_hbm, z_ref, gfin_ref, o_hbm, acc_ref, sem_in, sem_out, *, cap, final_norm, nchunk):
    b = pl.program_id(0)
    e = pl.program_id(1)
    rc = acc_ref.shape[0] // nchunk

    def in_copy(bi, c):
        rs = pl.ds(c * rc, rc)
        return pltpu.make_async_copy(x_hbm.at[bi, rs, :], acc_ref.at[rs, :], sem_in.at[c])

    def out_copy(c):
        rs = pl.ds(c * rc, rc)
        return pltpu.make_async_copy(acc_ref.at[rs, :], o_hbm.at[b, rs, :], sem_out.at[c])

    @pl.when((e == 0) & (b == 0))
    def _():
        for c in range(nchunk):
            in_copy(b, c).start()

    @pl.when(e == 0)
    def _():
        for c in range(nchunk):
            in_copy(b, c).wait()

    group = 4
    for j0 in range(0, cap, group):
        toks = [idx_ref[0, 0, 0, j0 + u] for u in range(group)]
        vals = [acc_ref[pl.ds(toks[u], 1), :] + z_ref[0, 0, j0 + u:j0 + u + 1, :] for u in range(group)]
        for u in range(group):
            acc_ref[pl.ds(toks[u], 1), :] = vals[u]

    @pl.when(e == pl.num_programs(1) - 1)
    def _():
        for c in range(nchunk):
            if final_norm:
                rs = pl.ds(c * rc, rc)
                acc_ref[rs, :] = _rms(acc_ref[rs, :], gfin_ref[...])
            out_copy(c).start()
        for c in range(nchunk):
            out_copy(c).wait()

            @pl.when(b + 1 < pl.num_programs(0))
            def _():
                in_copy(b + 1, c).start()


def _combine_call(idx, x, z, g_final, *, final_norm):
    bt, ne, cap = idx.shape
    _, n, d = x.shape
    nchunk = 32
    return pl.pallas_call(
        functools.partial(_combine_kernel, cap=cap, final_norm=final_norm, nchunk=nchunk),
        grid=(bt, ne),
        in_specs=[pl.BlockSpec((1, 1, 1, cap), lambda b, e: (b, e, 0, 0), memory_space=pltpu.SMEM),
                  pl.BlockSpec(memory_space=pl.ANY),
                  pl.BlockSpec((1, 1, cap, d), lambda b, e: (b, e, 0, 0)),
                  pl.BlockSpec((1, d), lambda b, e: (0, 0))],
        out_specs=pl.BlockSpec(memory_space=pl.ANY),
        out_shape=jax.ShapeDtypeStruct((bt, n, d), F32),
        scratch_shapes=[pltpu.VMEM((n, d), F32), pltpu.SemaphoreType.DMA((nchunk,)),
                        pltpu.SemaphoreType.DMA((nchunk,))],
        compiler_params=_cparams(("arbitrary", "arbitrary")),
        name="expert_combine",
    )(idx.reshape(bt, ne, 1, cap), x, z, g_final)


def _block_diag(w):
    h, hd, _ = w.shape
    eye = jnp.eye(h, dtype=w.dtype)
    return (w[:, :, None, :] * eye[:, None, :, None]).reshape(h * hd, h * hd)


def _gate_weights(wr, br, wi, bi):
    dense = [_block_diag(wr[0]), _block_diag(wi[0]), _block_diag(wr[1]), _block_diag(wi[1])]
    bias = [br[0], bi[0], br[1], bi[1]]
    lru_w = dense[0].shape[0]
    wg, bg = [], []
    for g in range(lru_w // GATE_GROUP):
        cs = slice(g * GATE_GROUP, (g + 1) * GATE_GROUP)
        wg.append(jnp.concatenate([m[cs, cs] for m in dense], axis=1))
        bg.append(jnp.concatenate([v[cs] for v in bias])[None, :])
    return jnp.stack(wg).astype(BF16), jnp.stack(bg)


def _channel_dft(fw):
    gd = fw // FFT_GROUPS
    c, s = _cos_sin(gd)
    eye = np.eye(FFT_GROUPS)
    return _split_const(np.concatenate([np.kron(eye, c), np.kron(eye, s)], axis=1))


def _mixer(x, lp, shift, scale, h0_f, h0_b, *, row_w, tm, tb, need_out):
    af, bf, ab, bb, gl, ysc, ucs = _inproj_call(
        x, shift, scale, lp["g1"], lp["w_in"], lp["layer"], lp["cw"], lp["cb"], lp["wg"], lp["bg"], lp["lam"],
        lp["scw"], lp["g_sc"], lp["dftc"], row_w=row_w, tm=tm)
    hf, sf = _scan_call(af, bf, h0_f, None, reverse=False, tb=tb)
    hs, sb = _scan_call(ab, bb, h0_b, hf, reverse=True, tb=tb)
    if not need_out:
        return sf, sb, None
    yfft = _fourier_call(ucs, lp["g_fft"])
    return sf, sb, (hs, gl, ysc, yfft)


def _moe(x, h2, aff_t, gate2, lp, g_final, *, final_norm, gs):
    bt, n, d = x.shape
    cap = CAPACITY_FACTOR * n // N_EXPERTS
    pad = (-n) % (SUBLANES * LANES)
    if pad:
        aff_t = jnp.pad(aff_t, ((0, 0), (0, 0), (0, pad)), constant_values=-1.0)
    idx, g = _select_call(aff_t, cap)
    z = _ffn_call(idx, h2, g, gate2, lp["wge"], lp["wue"], lp["wde"], lp["layer"], gs=gs)
    return _combine_call(idx, x, z, g_final, final_norm=final_norm)


def kernel(x, c, ctx, c_ctx, w_ada, b_ada, g_norm1, w_in, lru_conv_w, lru_conv_b, lru_wr, lru_br, lru_wi,
           lru_bi, lru_lam, sc_conv_w, g_out, w_out, g_norm2, w_router, w_gate_e, w_up_e, w_down_e, g_final):
    depth = w_ada.shape[0]
    bsz, seq, d = x.shape
    ctx_len = ctx.shape[1]
    lru_w = lru_conv_w.shape[2]
    conv_w = sc_conv_w.shape[2]
    fft_w = w_in.shape[2] - 2 * lru_w - 3 * conv_w

    cs = jnp.concatenate([c, c_ctx[None, :], jnp.zeros((SUBLANES - bsz - 1, d), F32)], axis=0)
    mods = _ada_call(cs, w_ada, b_ada)
    dftc = _channel_dft(fft_w)
    gfin = g_final[None, :]
    zero_state = jnp.zeros((bsz, 1, lru_w), F32)

    for l in range(depth):
        last = l == depth - 1
        mx = [mods[l, :bsz, None, k * d:(k + 1) * d] for k in range(6)]
        mc = [jnp.broadcast_to(mods[l, bsz, k * d:(k + 1) * d], (bsz, 1, d)) for k in range(6)]
        wg, bg = _gate_weights(lru_wr[l], lru_br[l], lru_wi[l], lru_bi[l])
        lp = dict(
            g1=g_norm1[l][None, :], w_in=w_in, cw=lru_conv_w[l], cb=lru_conv_b[l][None, :],
            wg=wg, bg=bg, lam=lru_lam[l].reshape(1, 2 * lru_w), scw=sc_conv_w[l],
            g_sc=g_out[l][None, lru_w:lru_w + conv_w], g_fft=g_out[l][None, lru_w + conv_w:], dftc=dftc,
            wge=w_gate_e, wue=w_up_e, wde=w_down_e, layer=l)
        g_lru = g_out[l][None, :lru_w]
        w_router_t = _split_const(w_router[l].T)

        sf, sb, parts = _mixer(ctx, lp, mc[0], mc[1], zero_state, zero_state,
                               row_w=ctx_len, tm=ctx_len, tb=ctx_len, need_out=not last)
        if not last:
            ctx, hc2, aff_c = _outproj_call(*parts, ctx, w_out, l, g_lru, mc[2], mc[3], mc[4],
                                            g_norm2[l][None, :], w_router_t, tm=ctx_len)
            ctx = _moe(ctx, hc2, aff_c, mc[5], lp, gfin, final_norm=False, gs=bsz)

        _, _, parts = _mixer(x, lp, mx[0], mx[1], sf, sb, row_w=GRID_W, tm=512, tb=512, need_out=True)
        x, hx2, aff_x = _outproj_call(*parts, x, w_out, l, g_lru, mx[2], mx[3], mx[4],
                                      g_norm2[l][None, :], w_router_t, tm=512)
        x = _moe(x, hx2, aff_x, mx[5], lp, gfin, final_norm=last, gs=1)
    return x
```

```python
import functools
import math

import numpy as np
import jax
import jax.numpy as jnp
from jax import lax
from jax.experimental import pallas as pl
from jax.experimental.pallas import tpu as pltpu

F32 = jnp.float32
BF16 = jnp.bfloat16
HI = lax.Precision.HIGHEST

GRID_W = 64
LRU_HEADS = 8
LRU_C = 8.0
N_EXPERTS = 16
CAPACITY_FACTOR = 2
EPS = 1e-6
FFT_GROUPS = 4

LANES = 128
SUBLANES = 8
GATE_GROUP = 256
SUB_ROWS = 512
VMEM_LIMIT = 56 * 1024 * 1024


def _cparams(sem):
    return pltpu.CompilerParams(dimension_semantics=sem, vmem_limit_bytes=VMEM_LIMIT)


def _rms(x, g):
    return x * lax.rsqrt(jnp.mean(x * x, axis=-1, keepdims=True) + EPS) * g


def _dot(a, b):
    return jnp.dot(a, b, preferred_element_type=F32)


def _dot_hi(a, b):
    return jnp.dot(a, b, preferred_element_type=F32, precision=HI)


def _split_const(m):
    m = jnp.asarray(m, F32)
    hi = m.astype(BF16)
    return jnp.stack([hi, (m - hi.astype(F32)).astype(BF16)])


def _split(x):
    hi = x.astype(BF16)
    return hi, (x - hi.astype(F32)).astype(BF16)


def _dot3_const_lhs(m_ref, x):
    x_hi, x_lo = _split(x)
    return _dot(m_ref[0], x_hi) + (_dot(m_ref[1], x_hi) + _dot(m_ref[0], x_lo))


def _dot3_const_rhs(x, m_ref):
    x_hi, x_lo = _split(x)
    return _dot(x_hi, m_ref[0]) + (_dot(x_hi, m_ref[1]) + _dot(x_lo, m_ref[0]))


def _dot_nt(a, b, precision=None):
    return lax.dot_general(a, b, (((1,), (1,)), ((), ())), preferred_element_type=F32, precision=precision)


def _ada_kernel(c_ref, w_ref, b_ref, o_ref):
    c = c_ref[...]
    o_ref[...] = _dot_hi(c * jax.nn.sigmoid(c), w_ref[...]) + b_ref[...]


def _ada_call(cs, w_ada, b_ada):
    depth, d, six_d = w_ada.shape
    nblk = six_d // d
    return pl.pallas_call(
        _ada_kernel,
        grid=(depth, nblk),
        in_specs=[
            pl.BlockSpec((SUBLANES, d), lambda l, j: (0, 0)),
            pl.BlockSpec((None, d, d), lambda l, j: (l, 0, j)),
            pl.BlockSpec((None, 1, d), lambda l, j: (l, 0, j)),
        ],
        out_specs=pl.BlockSpec((None, SUBLANES, d), lambda l, j: (l, 0, j)),
        out_shape=jax.ShapeDtypeStruct((depth, SUBLANES, six_d), F32),
        compiler_params=_cparams(("parallel", "parallel")),
        name="ada_mod",
    )(cs, w_ada, b_ada.reshape(depth, 1, six_d))


def _shift_rows(u, d, pos, row_w):
    n = u.shape[0]
    if d == 0:
        return u
    rolled = pltpu.roll(u, (-d) % n, axis=0)
    valid = (pos + d >= 0) & (pos + d < row_w)
    return jnp.where(valid, rolled, 0.0)


def _conv_rows(u, w_ref, left, pos, row_w):
    out = None
    for k in range(w_ref.shape[0]):
        term = w_ref[k:k + 1, :] * _shift_rows(u, k - left, pos, row_w)
        out = term if out is None else out + term
    return out


def _inproj_kernel(x_ref, sh_ref, sc_ref, g1_ref, win_ref, cw_ref, cb_ref, wg_ref, bg_ref, lam_ref,
                   scw_ref, gsc_ref, dft_ref,
                   af_ref, bf_ref, ab_ref, bb_ref, lg_ref, ysc_ref, ucs_ref, winb_ref, *, row_w, lru_w, conv_w, sub):
    @pl.when((pl.program_id(0) == 0) & (pl.program_id(1) == 0))
    def _():
        winb_ref[...] = win_ref[...].astype(BF16)

    tm = x_ref.shape[1]
    gmod = g1_ref[...] * (1.0 + sc_ref[0])
    shift = sh_ref[0]
    nl = -lam_ref[...]
    nsp = -LRU_C * (jnp.maximum(nl, 0.0) + jnp.log(1.0 + jnp.exp(-jnp.abs(nl))))
    pos = lax.broadcasted_iota(jnp.int32, (sub, 1), 0) % row_w
    out_refs = ((af_ref, bf_ref), (ab_ref, bb_ref))
    o = 2 * lru_w

    for t in range(tm // sub):
        rs = slice(t * sub, (t + 1) * sub)
        x = x_ref[0, rs, :]
        h = x * lax.rsqrt(jnp.mean(x * x, axis=-1, keepdims=True) + EPS) * gmod + shift
        p = _dot(h.astype(BF16), winb_ref[...])

        u = _conv_rows(p[:, :lru_w], cw_ref, 1, pos, row_w) + cb_ref[...]
        for g in range(lru_w // GATE_GROUP):
            cs = slice(g * GATE_GROUP, (g + 1) * GATE_GROUP)
            ug = u[:, cs]
            z = _dot(ug.astype(BF16), wg_ref[g]) + bg_ref[g]
            for d in range(2):
                r = jax.nn.sigmoid(z[:, (2 * d) * GATE_GROUP:(2 * d + 1) * GATE_GROUP])
                i = jax.nn.sigmoid(z[:, (2 * d + 1) * GATE_GROUP:(2 * d + 2) * GATE_GROUP])
                a = jnp.exp(r * nsp[:, d * lru_w + g * GATE_GROUP: d * lru_w + (g + 1) * GATE_GROUP])
                inp = jnp.sqrt(1.0 - a * a) * (i * ug)
                for k in range(GATE_GROUP // LANES):
                    plane = g * (GATE_GROUP // LANES) + k
                    out_refs[d][0][0, plane, rs, :] = a[:, k * LANES:(k + 1) * LANES]
                    out_refs[d][1][0, plane, rs, :] = inp[:, k * LANES:(k + 1) * LANES]

        lg_ref[0, rs, :] = jax.nn.gelu(p[:, lru_w:o], approximate=True)

        sc_b = p[:, o:o + conv_w]
        sc_c = p[:, o + conv_w:o + 2 * conv_w]
        sc_x = p[:, o + 2 * conv_w:o + 3 * conv_w]
        y_sc = sc_b * _conv_rows(sc_c * sc_x, scw_ref, 1, pos, row_w)
        ysc_ref[0, rs, :] = _rms(y_sc, gsc_ref[...])

        ucs_ref[0, rs, :] = _dot3_const_rhs(p[:, o + 3 * conv_w:], dft_ref)


def _inproj_call(x, shift, scale, g1, w_in, layer, cw, cb, wg, bg, lam, scw, gsc, dftc, *, row_w, tm):
    bt, n, d = x.shape
    in_cols = w_in.shape[2]
    lru_w = cw.shape[1]
    conv_w = scw.shape[1]
    fft_w = in_cols - 2 * lru_w - 3 * conv_w
    ng = lru_w // GATE_GROUP
    tok = lambda w: pl.BlockSpec((1, tm, w), lambda b, i: (b, i, 0))
    vec = lambda w: pl.BlockSpec((1, 1, w), lambda b, i: (b, 0, 0))
    full = lambda *s: pl.BlockSpec(s, lambda b, i: (0,) * len(s))
    shp = lambda w: jax.ShapeDtypeStruct((bt, n, w), F32)
    planes = pl.BlockSpec((1, lru_w // LANES, tm, LANES), lambda b, i: (b, 0, i, 0))
    return pl.pallas_call(
        functools.partial(_inproj_kernel, row_w=row_w, lru_w=lru_w, conv_w=conv_w, sub=min(tm, SUB_ROWS)),
        grid=(bt, n // tm),
        in_specs=[tok(d), vec(d), vec(d), full(1, d),
                  pl.BlockSpec((None, d, in_cols), lambda b, i: (layer, 0, 0), pipeline_mode=pl.Buffered(1)),
                  full(*cw.shape), full(1, lru_w),
                  full(ng, GATE_GROUP, 4 * GATE_GROUP), full(ng, 1, 4 * GATE_GROUP), full(1, 2 * lru_w),
                  full(*scw.shape), full(1, conv_w), full(2, fft_w, 2 * fft_w)],
        out_specs=[planes] * 4 + [tok(lru_w), tok(conv_w), tok(2 * fft_w)],
        out_shape=[jax.ShapeDtypeStruct((bt, lru_w // LANES, n, LANES), F32)] * 4
        + [shp(lru_w), shp(conv_w), shp(2 * fft_w)],
        scratch_shapes=[pltpu.VMEM((d, in_cols), BF16)],
        compiler_params=_cparams(("arbitrary", "arbitrary")),
        name="inproj_local",
    )(x, shift, scale, g1, w_in, cw, cb, wg, bg, lam, scw, gsc, dftc)


def _scan_kernel(*refs, reverse, add_other):
    if add_other:
        a_ref, b_ref, h0_ref, other_ref, h_ref, hl_ref, carry_ref = refs
    else:
        a_ref, b_ref, h0_ref, h_ref, hl_ref, carry_ref = refs
        other_ref = None

    @pl.when(pl.program_id(2) == 0)
    def _():
        carry_ref[...] = jnp.broadcast_to(h0_ref[0], carry_ref.shape)

    tb = a_ref.shape[2]
    group = SUBLANES * SUBLANES
    row = lax.broadcasted_iota(jnp.int32, (SUBLANES, LANES), 0)
    steps = range(SUBLANES - 1, -1, -1) if reverse else range(SUBLANES)
    first = SUBLANES - 1 if reverse else 0
    last = 0 if reverse else SUBLANES - 1
    groups = range(tb // group - 1, -1, -1) if reverse else range(tb // group)

    def shift_chunks(v, k):
        return pltpu.roll(v, (SUBLANES - k) if reverse else k, axis=0)

    carry = carry_ref[...]
    for g in groups:
        rows = [pl.ds(g * group + s, SUBLANES, stride=SUBLANES) for s in range(SUBLANES)]
        hs, ps = {}, {}
        h = p = None
        for s in steps:
            a = a_ref[0, 0, rows[s], :]
            b = b_ref[0, 0, rows[s], :]
            h = b if h is None else a * h + b
            p = a if p is None else p * a
            hs[s], ps[s] = h, p
        pe, he = p, h
        for k in (1, 2, 4):
            valid = (row <= SUBLANES - 1 - k) if reverse else (row >= k)
            he = jnp.where(valid, he + pe * shift_chunks(he, k), he)
            pe = jnp.where(valid, pe * shift_chunks(pe, k), pe)
        end = he + pe * carry
        h_in = jnp.where(row == first, carry, shift_chunks(end, 1))
        carry = jnp.broadcast_to(end[last:last + 1], end.shape)
        for s in range(SUBLANES):
            out = hs[s] + ps[s] * h_in
            if other_ref is not None:
                out = out + other_ref[0, 0, rows[s], :]
            h_ref[0, 0, rows[s], :] = out
    carry_ref[...] = carry
    hl_ref[0] = carry[0:1]


def _scan_call(a, b, h0, other, *, reverse, tb):
    bt, npl, n, _ = a.shape
    nt = n // tb
    blk = (lambda bi, l, i: (bi, l, nt - 1 - i, 0)) if reverse else (lambda bi, l, i: (bi, l, i, 0))
    tok = pl.BlockSpec((1, 1, tb, LANES), blk)
    st = pl.BlockSpec((1, 1, LANES), lambda bi, l, i: (bi, 0, l))
    ins = [a, b, h0] + ([other] if other is not None else [])
    in_specs = [tok, tok, st] + ([tok] if other is not None else [])
    return pl.pallas_call(
        functools.partial(_scan_kernel, reverse=reverse, add_other=other is not None),
        grid=(bt, npl, nt),
        in_specs=in_specs,
        out_specs=[tok, st],
        out_shape=[jax.ShapeDtypeStruct(a.shape, F32), jax.ShapeDtypeStruct((bt, 1, npl * LANES), F32)],
        scratch_shapes=[pltpu.VMEM((SUBLANES, LANES), F32)],
        compiler_params=_cparams(("parallel", "parallel", "arbitrary")),
        name="lru_scan_bwd" if reverse else "lru_scan_fwd",
    )(*ins)


def _cos_sin(n):
    k = np.arange(n, dtype=np.float64)
    ang = 2.0 * np.pi * np.outer(k, k) / n
    return np.cos(ang), np.sin(ang)


def _slab_copies(src_hbm, dst_ref, sem, b, first, count, width, slot):
    return [pltpu.make_async_copy(src_hbm.at[b, :, first + j, :], dst_ref.at[slot, :, pl.ds(j * width, width)],
                                  sem.at[slot]) for j in range(count)]


def _prefetch_slabs(src_hbm, dst_ref, sem, count, width):
    b, i = pl.program_id(0), pl.program_id(1)
    ni = pl.num_programs(1)
    step = b * ni + i
    slot = step % 2

    @pl.when(step == 0)
    def _():
        for cp in _slab_copies(src_hbm, dst_ref, sem, b, i * count, count, width, slot):
            cp.start()

    @pl.when(step + 1 < pl.num_programs(0) * ni)
    def _():
        wrap = i + 1 == ni
        bn = jnp.where(wrap, b + 1, b)
        nxt = jnp.where(wrap, 0, i + 1)
        for cp in _slab_copies(src_hbm, dst_ref, sem, bn, nxt * count, count, width, 1 - slot):
            cp.start()

    for cp in _slab_copies(src_hbm, dst_ref, sem, b, i * count, count, width, slot):
        cp.wait()
    return step, slot


def _dft1_kernel(x_hbm, m_ref, tc_ref, ts_ref, o_ref, xs_ref, sem, *, n2, jn, fw):
    _, slot = _prefetch_slabs(x_hbm, xs_ref, sem, jn, 2 * fw)
    r = _dot3_const_lhs(m_ref, xs_ref[slot])
    for j in range(jn):
        base = j * 2 * fw
        c_uc = r[:n2, base:base + fw]
        c_us = r[:n2, base + fw:base + 2 * fw]
        s_uc = r[n2:, base:base + fw]
        s_us = r[n2:, base + fw:base + 2 * fw]
        br = c_uc - s_us
        bi = -(c_us + s_uc)
        tc = tc_ref[0, :, j:j + 1]
        ts = ts_ref[0, :, j:j + 1]
        o_ref[0, j, :, :fw] = br * tc + bi * ts
        o_ref[0, j, :, fw:] = bi * tc - br * ts


def _dft2_kernel(p_hbm, m_ref, g_ref, y_hbm, ps_ref, ys_ref, sem_in, sem_out, *, n1, kn, fw):
    step, slot = _prefetch_slabs(p_hbm, ps_ref, sem_in, kn, 2 * fw)
    b, i = pl.program_id(0), pl.program_id(1)
    last = pl.num_programs(0) * pl.num_programs(1) - 1

    def out_copies(s):
        return [pltpu.make_async_copy(ys_ref.at[s, :, pl.ds(k * fw, fw)], y_hbm.at[b, :, i * kn + k, :],
                                      sem_out.at[s]) for k in range(kn)]

    r = _dot3_const_lhs(m_ref, ps_ref[slot])

    @pl.when(step >= 2)
    def _():
        for cp in out_copies(slot):
            cp.wait()

    for k in range(kn):
        base = k * 2 * fw
        y = r[:n1, base:base + fw] + r[n1:, base + fw:base + 2 * fw]
        ys_ref[slot, :, k * fw:(k + 1) * fw] = _rms(y, g_ref[...])
    for cp in out_copies(slot):
        cp.start()

    @pl.when(step == last)
    def _():
        for cp in out_copies(slot):
            cp.wait()

    @pl.when((step == last) & (step >= 1))
    def _():
        for cp in out_copies(1 - slot):
            cp.wait()


def _dft_direct_kernel(x_ref, m_ref, g_ref, o_ref, *, n, fw):
    x = x_ref[0]
    r = _dot3_const_lhs(m_ref, x)
    y = r[:n, :fw] - r[n:, fw:]
    o_ref[0] = _rms(y, g_ref[...])


def _fourier_call(ucs, g_fft):
    bt, n, fw2 = ucs.shape
    fw = fw2 // 2
    gd = fw // FFT_GROUPS
    scale = 1.0 / math.sqrt(n * gd)
    if n <= 512:
        c, s = _cos_sin(n)
        m = _split_const(np.concatenate([c, s], 0) * scale)
        return pl.pallas_call(
            functools.partial(_dft_direct_kernel, n=n, fw=fw),
            grid=(bt,),
            in_specs=[pl.BlockSpec((1, n, fw2), lambda b: (b, 0, 0)),
                      pl.BlockSpec((2, 2 * n, n), lambda b: (0, 0, 0)),
                      pl.BlockSpec((1, fw), lambda b: (0, 0))],
            out_specs=pl.BlockSpec((1, n, fw), lambda b: (b, 0, 0)),
            out_shape=jax.ShapeDtypeStruct((bt, n, fw), F32),
            compiler_params=_cparams(("parallel",)),
            name="dft_direct",
        )(ucs, m, g_fft)

    n1 = LANES
    n2 = n // n1
    jn = 16
    c2, s2 = _cos_sin(n2)
    m1 = _split_const(np.concatenate([c2, s2], 0))
    ang = 2.0 * np.pi * np.outer(np.arange(n2), np.arange(n1)) / n
    tw = lambda f: jnp.asarray(f(ang).reshape(n2, n1 // jn, jn).transpose(1, 0, 2), F32)
    p = pl.pallas_call(
        functools.partial(_dft1_kernel, n2=n2, jn=jn, fw=fw),
        grid=(bt, n1 // jn),
        in_specs=[pl.BlockSpec(memory_space=pl.ANY),
                  pl.BlockSpec((2, 2 * n2, n2), lambda b, i: (0, 0, 0)),
                  pl.BlockSpec((1, n2, jn), lambda b, i: (i, 0, 0)),
                  pl.BlockSpec((1, n2, jn), lambda b, i: (i, 0, 0))],
        out_specs=pl.BlockSpec((1, jn, n2, fw2), lambda b, i: (b, i, 0, 0)),
        out_shape=jax.ShapeDtypeStruct((bt, n1, n2, fw2), F32),
        scratch_shapes=[pltpu.VMEM((2, n2, jn * fw2), F32), pltpu.SemaphoreType.DMA((2,))],
        compiler_params=_cparams(("arbitrary", "arbitrary")),
        name="dft_stage1",
    )(ucs.reshape(bt, n2, n1, fw2), m1, tw(np.cos), tw(np.sin))

    kn = min(8, n2)
    c1, s1 = _cos_sin(n1)
    m2 = _split_const(np.concatenate([c1, s1], 0) * scale)
    y = pl.pallas_call(
        functools.partial(_dft2_kernel, n1=n1, kn=kn, fw=fw),
        grid=(bt, n2 // kn),
        in_specs=[pl.BlockSpec(memory_space=pl.ANY),
                  pl.BlockSpec((2, 2 * n1, n1), lambda b, i: (0, 0, 0)),
                  pl.BlockSpec((1, fw), lambda b, i: (0, 0))],
        out_specs=pl.BlockSpec(memory_space=pl.ANY),
        out_shape=jax.ShapeDtypeStruct((bt, n1, n2, fw), F32),
        scratch_shapes=[pltpu.VMEM((2, n1, kn * fw2), F32), pltpu.VMEM((2, n1, kn * fw), F32),
                        pltpu.SemaphoreType.DMA((2,)), pltpu.SemaphoreType.DMA((2,))],
        compiler_params=_cparams(("arbitrary", "arbitrary")),
        name="dft_stage2",
    )(p, m2, g_fft)
    return y.reshape(bt, n, fw)


def _outproj_kernel(hs_ref, lg_ref, ysc_ref, yfft_ref, x_ref, wout_ref, glru_ref, gate_ref,
                    sh_ref, sc_ref, g2_ref, wr_ref, xn_ref, h2_ref, aff_ref, woutb_ref, *, sub):
    @pl.when((pl.program_id(0) == 0) & (pl.program_id(1) == 0))
    def _():
        woutb_ref[...] = wout_ref[...].astype(BF16)

    tm, d = x_ref.shape[1], x_ref.shape[2]
    tile = d // LANES
    gmod = g2_ref[...] * (1.0 + sc_ref[0])
    for t in range(tm // sub):
        rs = slice(t * sub, (t + 1) * sub)
        hs = jnp.concatenate([hs_ref[0, k, rs, :] for k in range(hs_ref.shape[1])], axis=-1)
        y_lru = _rms(hs * lg_ref[0, rs, :], glru_ref[...])
        y = jnp.concatenate([y_lru.astype(BF16), ysc_ref[0, rs, :].astype(BF16), yfft_ref[0, rs, :].astype(BF16)],
                            axis=-1)
        xn = x_ref[0, rs, :] + gate_ref[0] * _dot(y, woutb_ref[...])
        xn_ref[0, rs, :] = xn
        h2 = xn * lax.rsqrt(jnp.mean(xn * xn, axis=-1, keepdims=True) + EPS) * gmod + sh_ref[0]
        for k in range(tile):
            h2_ref[0, pl.ds(t * sub * tile + k, sub, stride=tile), :] = h2[:, k * LANES:(k + 1) * LANES]
        h_hi, h_lo = _split(h2)
        logits = _dot_nt(wr_ref[0], h_hi) + (_dot_nt(wr_ref[1], h_hi) + _dot_nt(wr_ref[0], h_lo))
        m = jnp.max(logits, axis=0, keepdims=True)
        e = jnp.exp(logits - m)
        aff_ref[0, :, rs] = e / jnp.sum(e, axis=0, keepdims=True)


def _outproj_call(hs, lg, ysc, yfft, x, w_out, layer, g_lru, gate, shift, scale, g2, w_router_t, *, tm):
    bt, n, d = x.shape
    ne = w_router_t.shape[1]
    tok = lambda w: pl.BlockSpec((1, tm, w), lambda b, i: (b, i, 0))
    vec = lambda w: pl.BlockSpec((1, 1, w), lambda b, i: (b, 0, 0))
    full = lambda *s: pl.BlockSpec(s, lambda b, i: (0,) * len(s))
    return pl.pallas_call(
        functools.partial(_outproj_kernel, sub=min(tm, SUB_ROWS)),
        grid=(bt, n // tm),
        in_specs=[pl.BlockSpec((1, hs.shape[1], tm, LANES), lambda b, i: (b, 0, i, 0)),
                  tok(lg.shape[2]), tok(ysc.shape[2]), tok(yfft.shape[2]), tok(d),
                  pl.BlockSpec((None,) + w_out.shape[1:], lambda b, i: (layer, 0, 0), pipeline_mode=pl.Buffered(1)),
                  full(1, lg.shape[2]), vec(d), vec(d), vec(d), full(1, d), full(2, ne, d)],
        out_specs=[tok(d), pl.BlockSpec((1, tm * d // LANES, LANES), lambda b, i: (b, i, 0)),
                   pl.BlockSpec((1, ne, tm), lambda b, i: (b, 0, i))],
        out_shape=[jax.ShapeDtypeStruct((bt, n, d), F32), jax.ShapeDtypeStruct((bt, n * d // LANES, LANES), F32),
                   jax.ShapeDtypeStruct((bt, ne, n), F32)],
        scratch_shapes=[pltpu.VMEM(w_out.shape[1:], BF16)],
        compiler_params=_cparams(("arbitrary", "arbitrary")),
        name="outproj_router",
    )(hs, lg, ysc, yfft, x, w_out, g_lru, gate, shift, scale, g2, w_router_t)


def _select_kernel(aff_ref, idx_ref, g_ref, *, cap):
    aff = aff_ref[0]
    ne, nb, _ = aff.shape

    def count(mask):
        c = jnp.sum(mask.astype(jnp.int32), axis=1, keepdims=True)
        return jnp.sum(c, axis=2, keepdims=True)

    def bit_step(i, t):
        cand = t | (jnp.int32(1) << (30 - i))
        return jnp.where(count(aff >= lax.bitcast_convert_type(cand, F32)) >= cap, cand, t)

    bits = lax.fori_loop(0, 31, bit_step, jnp.zeros((ne, 1, 1), jnp.int32))
    lo = lax.bitcast_convert_type(bits, F32)
    hi = lax.bitcast_convert_type(bits + 1, F32)

    def mid_step(i, lh):
        lo, hi = lh
        mid = (lo + hi) * 0.5
        ok = count(aff >= mid) >= cap
        return jnp.where(ok, mid, lo), jnp.where(ok, hi, mid)

    lo, hi = lax.fori_loop(0, 14, mid_step, (lo, hi))
    gt = aff >= hi
    eq = (aff >= lo) & (aff < hi)
    need = (cap - count(gt)).astype(F32)

    lane = lax.broadcasted_iota(jnp.int32, (LANES, LANES), 0)
    lane_t = lax.broadcasted_iota(jnp.int32, (LANES, LANES), 1)
    tri_incl = (lane <= lane_t).astype(BF16)
    blk = lax.broadcasted_iota(jnp.int32, (nb, nb), 0)
    blk_t = lax.broadcasted_iota(jnp.int32, (nb, nb), 1)
    tri_blk = (blk <= blk_t).astype(BF16)
    ones_row = jnp.ones((SUBLANES, LANES), BF16)
    kvals = lax.broadcasted_iota(jnp.int32, (SUBLANES, nb), 1).astype(BF16)
    slot = lax.broadcasted_iota(jnp.int32, (cap, 1), 0).astype(F32)

    def block_prefix(mask_bf16):
        lc = _dot(mask_bf16, tri_incl)
        cnt_row = _dot_nt(ones_row, mask_bf16)
        inc_row = _dot(cnt_row.astype(BF16), tri_blk)
        return lc, inc_row - cnt_row, inc_row

    for e in range(ne):
        eq_e = eq[e].astype(BF16)
        lc_eq, off_eq, _ = block_prefix(eq_e)
        tri_strict = (blk_t < blk).astype(BF16)
        cnt_col = lc_eq[:, LANES - 1:LANES]
        before = _dot(tri_strict, jnp.broadcast_to(cnt_col, (nb, LANES)).astype(BF16))[:, :1]
        rank = lc_eq - eq[e].astype(F32) + before
        sel = gt[e] | (eq[e] & (rank < need[e]))
        sel_bf = sel.astype(BF16)

        lc, off_row, inc_row = block_prefix(sel_bf)
        off1 = off_row[0:1]
        inc1 = inc_row[0:1]
        onehot = ((slot >= off1) & (slot < inc1))
        oh_bf = onehot.astype(BF16)
        offk = jnp.sum(jnp.where(onehot, off1, 0.0), axis=1, keepdims=True)
        jl = slot - offk
        m = _dot(oh_bf, lc.astype(BF16))
        below = (m <= jl)
        kb_row = _dot_nt(kvals, oh_bf)
        r_row = _dot_nt(ones_row, below.astype(BF16))
        idx_ref[0, e:e + 1, :] = (kb_row[0:1] * float(LANES) + r_row[0:1]).astype(jnp.int32)

        msel = _dot(oh_bf, sel_bf)
        hit = (m == jl + 1.0) & (msel > 0.5)
        a_hi, a_mid = _split(aff[e])
        a_lo = (aff[e] - a_hi.astype(F32) - a_mid.astype(F32)).astype(BF16)
        aff_rows = _dot(oh_bf, a_hi) + (_dot(oh_bf, a_mid) + _dot(oh_bf, a_lo))
        g_ref[0, e] = jnp.sum(jnp.where(hit, aff_rows, 0.0), axis=1, keepdims=True)


def _select_call(aff_t, cap):
    bt, ne, n = aff_t.shape
    nb = n // LANES
    return pl.pallas_call(
        functools.partial(_select_kernel, cap=cap),
        grid=(bt,),
        in_specs=[pl.BlockSpec((1, ne, nb, LANES), lambda b: (b, 0, 0, 0))],
        out_specs=[pl.BlockSpec((1, ne, cap), lambda b: (b, 0, 0)),
                   pl.BlockSpec((1, ne, cap, 1), lambda b: (b, 0, 0, 0))],
        out_shape=[jax.ShapeDtypeStruct((bt, ne, cap), jnp.int32),
                   jax.ShapeDtypeStruct((bt, ne, cap, 1), F32)],
        compiler_params=_cparams(("parallel",)),
        name="expert_select",
    )(aff_t.reshape(bt, ne, nb, LANES))


def _ffn_kernel(idx_ref, idxn_ref, h_hbm, g_ref, gate2_ref, wg_ref, wu_ref, wd_ref, o_ref,
                xs_ref, wgb_ref, wub_ref, wdb_ref, sem, *, cap, chunk, gs, d):
    e = pl.program_id(0)
    bb = pl.program_id(1)
    nbb = pl.num_programs(1)
    step = e * nbb + bb
    slot = step % 2
    rows = gs * cap
    tile = d // LANES

    def row_copy(ids_ref, b0, dst_slot, s, j):
        tok = pl.multiple_of(ids_ref[s, 0, 0, j] * tile, tile)
        return pltpu.make_async_copy(h_hbm.at[b0 + s, pl.ds(tok, tile), :],
                                     xs_ref.at[dst_slot, pl.ds((s * cap + j) * tile, tile), :], sem.at[dst_slot])

    def wait_slot(s):
        pltpu.make_async_copy(h_hbm.at[0, pl.ds(0, rows * tile), :], xs_ref.at[s], sem.at[s]).wait()

    @pl.when(step == 0)
    def _():
        for s in range(gs):
            def start(j, carry):
                row_copy(idx_ref, bb * gs, slot, s, j).start()
                return carry
            lax.fori_loop(0, cap, start, 0, unroll=8)

    @pl.when(bb == 0)
    def _():
        wgb_ref[...] = wg_ref[0, 0].astype(BF16)
        wub_ref[...] = wu_ref[0, 0].astype(BF16)
        wdb_ref[...] = wd_ref[0, 0].astype(BF16)

    wait_slot(slot)

    bn = ((bb + 1) % nbb) * gs
    for c in range(rows // chunk):
        pieces = [xs_ref[slot, pl.ds(c * chunk * tile + k, chunk, stride=tile), :].astype(BF16) for k in range(tile)]
        xb = jnp.concatenate(pieces, axis=-1)
        gate = _dot(xb, wgb_ref[...])
        up = _dot(xb, wub_ref[...])
        for r in range(c * chunk, (c + 1) * chunk):
            row_copy(idxn_ref, bn, 1 - slot, r // cap, r % cap).start()
        hid = (gate * jax.nn.sigmoid(gate)) * up
        y = _dot(hid.astype(BF16), wdb_ref[...])
        s, r0 = (c * chunk) // cap, (c * chunk) % cap
        o_ref[s, 0, r0:r0 + chunk, :] = (y * g_ref[s, 0, r0:r0 + chunk, :]) * gate2_ref[s]

    @pl.when(step == pl.num_programs(0) * nbb - 1)
    def _():
        wait_slot(1 - slot)


def _ffn_call(idx, h2t, g, gate2, wg, wu, wd, layer, *, gs):
    bt, ne, cap = idx.shape
    d = gate2.shape[2]
    ff = wg.shape[3]
    nbb = bt // gs
    chunk = min(cap, 128)
    wspec = lambda s: pl.BlockSpec((1, 1) + s, lambda e, b: (layer, e, 0, 0))
    idx4 = idx.reshape(bt, ne, 1, cap)
    nxt = lambda e, b: ((b + 1) % nbb, jnp.minimum(e + (b + 1) // nbb, ne - 1), 0, 0)
    return pl.pallas_call(
        functools.partial(_ffn_kernel, cap=cap, chunk=chunk, gs=gs, d=d),
        grid=(ne, nbb),
        in_specs=[pl.BlockSpec((gs, 1, 1, cap), lambda e, b: (b, e, 0, 0), memory_space=pltpu.SMEM),
                  pl.BlockSpec((gs, 1, 1, cap), nxt, memory_space=pltpu.SMEM),
                  pl.BlockSpec(memory_space=pl.ANY),
                  pl.BlockSpec((gs, 1, cap, 1), lambda e, b: (b, e, 0, 0)),
                  pl.BlockSpec((gs, 1, d), lambda e, b: (b, 0, 0)),
                  wspec((d, ff)), wspec((d, ff)), wspec((ff, d))],
        out_specs=pl.BlockSpec((gs, 1, cap, d), lambda e, b: (b, e, 0, 0)),
        out_shape=jax.ShapeDtypeStruct((bt, ne, cap, d), F32),
        scratch_shapes=[pltpu.VMEM((2, gs * cap * d // LANES, LANES), F32), pltpu.VMEM((d, ff), BF16),
                        pltpu.VMEM((d, ff), BF16), pltpu.VMEM((ff, d), BF16), pltpu.SemaphoreType.DMA((2,))],
        compiler_params=_cparams(("arbitrary", "arbitrary")),
        name="expert_ffn",
    )(idx4, idx4, h2t, g, gate2, wg, wu, wd)


def _combine_kernel(idx_ref, x_hbm, z_ref, gfin_ref, o_hbm, acc_ref, sem_in, sem_out, *, cap, final_norm, nchunk):
    b = pl.program_id(0)
    e = pl.program_id(1)
    rc = acc_ref.shape[0] // nchunk

    def in_copy(bi, c):
        rs = pl.ds(c * rc, rc)
        return pltpu.make_async_copy(x_hbm.at[bi, rs, :], acc_ref.at[rs, :], sem_in.at[c])

    def out_copy(c):
        rs = pl.ds(c * rc, rc)
        return pltpu.make_async_copy(acc_ref.at[rs, :], o_hbm.at[b, rs, :], sem_out.at[c])

    @pl.when((e == 0) & (b == 0))
    def _():
        for c in range(nchunk):
            in_copy(b, c).start()

    @pl.when(e == 0)
    def _():
        for c in range(nchunk):
            in_copy(b, c).wait()

    group = 4
    for j0 in range(0, cap, group):
        toks = [idx_ref[0, 0, 0, j0 + u] for u in range(group)]
        vals = [acc_ref[pl.ds(toks[u], 1), :] + z_ref[0, 0, j0 + u:j0 + u + 1, :] for u in range(group)]
        for u in range(group):
            acc_ref[pl.ds(toks[u], 1), :] = vals[u]

    @pl.when(e == pl.num_programs(1) - 1)
    def _():
        for c in range(nchunk):
            if final_norm:
                rs = pl.ds(c * rc, rc)
                acc_ref[rs, :] = _rms(acc_ref[rs, :], gfin_ref[...])
            out_copy(c).start()
        for c in range(nchunk):
            out_copy(c).wait()

            @pl.when(b + 1 < pl.num_programs(0))
            def _():
                in_copy(b + 1, c).start()


def _combine_call(idx, x, z, g_final, *, final_norm):
    bt, ne, cap = idx.shape
    _, n, d = x.shape
    nchunk = 32
    return pl.pallas_call(
        functools.partial(_combine_kernel, cap=cap, final_norm=final_norm, nchunk=nchunk),
        grid=(bt, ne),
        in_specs=[pl.BlockSpec((1, 1, 1, cap), lambda b, e: (b, e, 0, 0), memory_space=pltpu.SMEM),
                  pl.BlockSpec(memory_space=pl.ANY),
                  pl.BlockSpec((1, 1, cap, d), lambda b, e: (b, e, 0, 0)),
                  pl.BlockSpec((1, d), lambda b, e: (0, 0))],
        out_specs=pl.BlockSpec(memory_space=pl.ANY),
        out_shape=jax.ShapeDtypeStruct((bt, n, d), F32),
        scratch_shapes=[pltpu.VMEM((n, d), F32), pltpu.SemaphoreType.DMA((nchunk,)),
                        pltpu.SemaphoreType.DMA((nchunk,))],
        compiler_params=_cparams(("arbitrary", "arbitrary")),
        name="expert_combine",
    )(idx.reshape(bt, ne, 1, cap), x, z, g_final)


def _block_diag(w):
    h, hd, _ = w.shape
    eye = jnp.eye(h, dtype=w.dtype)
    return (w[:, :, None, :] * eye[:, None, :, None]).reshape(h * hd, h * hd)


def _gate_weights(wr, br, wi, bi):
    dense = [_block_diag(wr[0]), _block_diag(wi[0]), _block_diag(wr[1]), _block_diag(wi[1])]
    bias = [br[0], bi[0], br[1], bi[1]]
    lru_w = dense[0].shape[0]
    wg, bg = [], []
    for g in range(lru_w // GATE_GROUP):
        cs = slice(g * GATE_GROUP, (g + 1) * GATE_GROUP)
        wg.append(jnp.concatenate([m[cs, cs] for m in dense], axis=1))
        bg.append(jnp.concatenate([v[cs] for v in bias])[None, :])
    return jnp.stack(wg).astype(BF16), jnp.stack(bg)


def _channel_dft(fw):
    gd = fw // FFT_GROUPS
    c, s = _cos_sin(gd)
    eye = np.eye(FFT_GROUPS)
    return _split_const(np.concatenate([np.kron(eye, c), np.kron(eye, s)], axis=1))


def _mixer(x, lp, shift, scale, h0_f, h0_b, *, row_w, tm, tb, need_out):
    af, bf, ab, bb, lg, ysc, ucs = _inproj_call(
        x, shift, scale, lp["g1"], lp["w_in"], lp["layer"], lp["cw"], lp["cb"], lp["wg"], lp["bg"], lp["lam"],
        lp["scw"], lp["g_sc"], lp["dftc"], row_w=row_w, tm=tm)
    hf, sf = _scan_call(af, bf, h0_f, None, reverse=False, tb=tb)
    hs, sb = _scan_call(ab, bb, h0_b, hf, reverse=True, tb=tb)
    if not need_out:
        return sf, sb, None
    yfft = _fourier_call(ucs, lp["g_fft"])
    return sf, sb, (hs, lg, ysc, yfft)


def _moe(x, h2, aff_t, gate2, lp, g_final, *, final_norm, gs):
    bt, n, d = x.shape
    cap = CAPACITY_FACTOR * n // N_EXPERTS
    pad = (-n) % (SUBLANES * LANES)
    if pad:
        aff_t = jnp.pad(aff_t, ((0, 0), (0, 0), (0, pad)), constant_values=-1.0)
    idx, g = _select_call(aff_t, cap)
    z = _ffn_call(idx, h2, g, gate2, lp["wge"], lp["wue"], lp["wde"], lp["layer"], gs=gs)
    return _combine_call(idx, x, z, g_final, final_norm=final_norm)


def kernel(x, c, ctx, c_ctx, w_ada, b_ada, g_norm1, w_in, lru_conv_w, lru_conv_b, lru_wr, lru_br, lru_wi,
           lru_bi, lru_lam, sc_conv_w, g_out, w_out, g_norm2, w_router, w_gate_e, w_up_e, w_down_e, g_final):
    depth = w_ada.shape[0]
    bsz, seq, d = x.shape
    ctx_len = ctx.shape[1]
    lru_w = lru_conv_w.shape[2]
    conv_w = sc_conv_w.shape[2]
    fft_w = w_in.shape[2] - 2 * lru_w - 3 * conv_w

    cs = jnp.concatenate([c, c_ctx[None, :], jnp.zeros((SUBLANES - bsz - 1, d), F32)], axis=0)
    mods = _ada_call(cs, w_ada, b_ada)
    dftc = _channel_dft(fft_w)
    gfin = g_final[None, :]
    zero_state = jnp.zeros((bsz, 1, lru_w), F32)

    for l in range(depth):
        last = l == depth - 1
        mx = [mods[l, :bsz, None, k * d:(k + 1) * d] for k in range(6)]
        mc = [jnp.broadcast_to(mods[l, bsz, k * d:(k + 1) * d], (bsz, 1, d)) for k in range(6)]
        wg, bg = _gate_weights(lru_wr[l], lru_br[l], lru_wi[l], lru_bi[l])
        lp = dict(
            g1=g_norm1[l][None, :], w_in=w_in, cw=lru_conv_w[l], cb=lru_conv_b[l][None, :],
            wg=wg, bg=bg, lam=lru_lam[l].reshape(1, 2 * lru_w), scw=sc_conv_w[l],
            g_sc=g_out[l][None, lru_w:lru_w + conv_w], g_fft=g_out[l][None, lru_w + conv_w:], dftc=dftc,
            wge=w_gate_e, wue=w_up_e, wde=w_down_e, layer=l)
        g_lru = g_out[l][None, :lru_w]
        w_router_t = _split_const(w_router[l].T)

        sf, sb, parts = _mixer(ctx, lp, mc[0], mc[1], zero_state, zero_state,
                               row_w=ctx_len, tm=ctx_len, tb=ctx_len, need_out=not last)
        if not last:
            ctx, hc2, aff_c = _outproj_call(*parts, ctx, w_out, l, g_lru, mc[2], mc[3], mc[4],
                                            g_norm2[l][None, :], w_router_t, tm=ctx_len)
            ctx = _moe(ctx, hc2, aff_c, mc[5], lp, gfin, final_norm=False, gs=bsz)

        _, _, parts = _mixer(x, lp, mx[0], mx[1], sf, sb, row_w=GRID_W, tm=512, tb=2048, need_out=True)
        x, hx2, aff_x = _outproj_call(*parts, x, w_out, l, g_lru, mx[2], mx[3], mx[4],
                                      g_norm2[l][None, :], w_router_t, tm=512)
        x = _moe(x, hx2, aff_x, mx[5], lp, gfin, final_norm=last, gs=1)
    return x
```

```python
import functools
import math

import numpy as np
import jax
import jax.numpy as jnp
from jax import lax
from jax.experimental import pallas as pl
from jax.experimental.pallas import tpu as pltpu

F32 = jnp.float32
BF16 = jnp.bfloat16
HI = lax.Precision.HIGHEST

GRID_W = 64
LRU_HEADS = 8
LRU_C = 8.0
N_EXPERTS = 16
CAPACITY_FACTOR = 2
EPS = 1e-6
FFT_GROUPS = 4

LANES = 128
SUBLANES = 8
GATE_GROUP = 256
SUB_ROWS = 512
VMEM_LIMIT = 56 * 1024 * 1024


def _cparams(sem):
    return pltpu.CompilerParams(dimension_semantics=sem, vmem_limit_bytes=VMEM_LIMIT)


def _rms(x, g):
    return x * lax.rsqrt(jnp.mean(x * x, axis=-1, keepdims=True) + EPS) * g


def _dot(a, b):
    return jnp.dot(a, b, preferred_element_type=F32)


def _dot_hi(a, b):
    return jnp.dot(a, b, preferred_element_type=F32, precision=HI)


def _split_const(m):
    m = jnp.asarray(m, F32)
    hi = m.astype(BF16)
    return jnp.stack([hi, (m - hi.astype(F32)).astype(BF16)])


def _split(x):
    hi = x.astype(BF16)
    return hi, (x - hi.astype(F32)).astype(BF16)


def _dot3_const_lhs(m_ref, x):
    x_hi, x_lo = _split(x)
    return _dot(m_ref[0], x_hi) + (_dot(m_ref[1], x_hi) + _dot(m_ref[0], x_lo))


def _dot3_const_rhs(x, m_ref):
    x_hi, x_lo = _split(x)
    return _dot(x_hi, m_ref[0]) + (_dot(x_hi, m_ref[1]) + _dot(x_lo, m_ref[0]))


def _dot_nt(a, b, precision=None):
    return lax.dot_general(a, b, (((1,), (1,)), ((), ())), preferred_element_type=F32, precision=precision)


def _ada_kernel(c_ref, w_ref, b_ref, o_ref):
    c = c_ref[...]
    o_ref[...] = _dot_hi(c * jax.nn.sigmoid(c), w_ref[...]) + b_ref[...]


def _ada_call(cs, w_ada, b_ada):
    depth, d, six_d = w_ada.shape
    nblk = six_d // d
    return pl.pallas_call(
        _ada_kernel,
        grid=(depth, nblk),
        in_specs=[
            pl.BlockSpec((SUBLANES, d), lambda l, j: (0, 0)),
            pl.BlockSpec((None, d, d), lambda l, j: (l, 0, j)),
            pl.BlockSpec((None, 1, d), lambda l, j: (l, 0, j)),
        ],
        out_specs=pl.BlockSpec((None, SUBLANES, d), lambda l, j: (l, 0, j)),
        out_shape=jax.ShapeDtypeStruct((depth, SUBLANES, six_d), F32),
        compiler_params=_cparams(("parallel", "parallel")),
        name="ada_mod",
    )(cs, w_ada, b_ada.reshape(depth, 1, six_d))


def _shift_rows(u, d, pos, row_w):
    n = u.shape[0]
    if d == 0:
        return u
    rolled = pltpu.roll(u, (-d) % n, axis=0)
    valid = (pos + d >= 0) & (pos + d < row_w)
    return jnp.where(valid, rolled, 0.0)


def _conv_rows(u, w_ref, left, pos, row_w):
    out = None
    for k in range(w_ref.shape[0]):
        term = w_ref[k:k + 1, :] * _shift_rows(u, k - left, pos, row_w)
        out = term if out is None else out + term
    return out


def _inproj_kernel(x_ref, sh_ref, sc_ref, g1_ref, win_ref, cw_ref, cb_ref, wg_ref, bg_ref, lam_ref,
                   scw_ref, gsc_ref, dft_ref,
                   af_ref, bf_ref, ab_ref, bb_ref, lg_ref, ysc_ref, ucs_ref, winb_ref, *, row_w, lru_w, conv_w, sub):
    @pl.when((pl.program_id(0) == 0) & (pl.program_id(1) == 0))
    def _():
        winb_ref[...] = win_ref[...].astype(BF16)

    tm = x_ref.shape[1]
    gmod = g1_ref[...] * (1.0 + sc_ref[0])
    shift = sh_ref[0]
    nl = -lam_ref[...]
    nsp = -LRU_C * (jnp.maximum(nl, 0.0) + jnp.log(1.0 + jnp.exp(-jnp.abs(nl))))
    pos = lax.broadcasted_iota(jnp.int32, (sub, 1), 0) % row_w
    out_refs = ((af_ref, bf_ref), (ab_ref, bb_ref))
    o = 2 * lru_w

    for t in range(tm // sub):
        rs = slice(t * sub, (t + 1) * sub)
        x = x_ref[0, rs, :]
        h = x * lax.rsqrt(jnp.mean(x * x, axis=-1, keepdims=True) + EPS) * gmod + shift
        p = _dot(h.astype(BF16), winb_ref[...])

        u = _conv_rows(p[:, :lru_w], cw_ref, 1, pos, row_w) + cb_ref[...]
        for g in range(lru_w // GATE_GROUP):
            cs = slice(g * GATE_GROUP, (g + 1) * GATE_GROUP)
            ug = u[:, cs]
            z = _dot(ug.astype(BF16), wg_ref[g]) + bg_ref[g]
            for d in range(2):
                r = jax.nn.sigmoid(z[:, (2 * d) * GATE_GROUP:(2 * d + 1) * GATE_GROUP])
                i = jax.nn.sigmoid(z[:, (2 * d + 1) * GATE_GROUP:(2 * d + 2) * GATE_GROUP])
                a = jnp.exp(r * nsp[:, d * lru_w + g * GATE_GROUP: d * lru_w + (g + 1) * GATE_GROUP])
                inp = jnp.sqrt(1.0 - a * a) * (i * ug)
                for k in range(GATE_GROUP // LANES):
                    plane = g * (GATE_GROUP // LANES) + k
                    out_refs[d][0][0, plane, rs, :] = a[:, k * LANES:(k + 1) * LANES]
                    out_refs[d][1][0, plane, rs, :] = inp[:, k * LANES:(k + 1) * LANES]

        lg_ref[0, rs, :] = jax.nn.gelu(p[:, lru_w:o], approximate=True)

        sc_b = p[:, o:o + conv_w]
        sc_c = p[:, o + conv_w:o + 2 * conv_w]
        sc_x = p[:, o + 2 * conv_w:o + 3 * conv_w]
        y_sc = sc_b * _conv_rows(sc_c * sc_x, scw_ref, 1, pos, row_w)
        ysc_ref[0, rs, :] = _rms(y_sc, gsc_ref[...])

        ucs_ref[0, rs, :] = _dot3_const_rhs(p[:, o + 3 * conv_w:], dft_ref)


def _inproj_call(x, shift, scale, g1, w_in, layer, cw, cb, wg, bg, lam, scw, gsc, dftc, *, row_w, tm):
    bt, n, d = x.shape
    in_cols = w_in.shape[2]
    lru_w = cw.shape[1]
    conv_w = scw.shape[1]
    fft_w = in_cols - 2 * lru_w - 3 * conv_w
    ng = lru_w // GATE_GROUP
    tok = lambda w: pl.BlockSpec((1, tm, w), lambda b, i: (b, i, 0))
    vec = lambda w: pl.BlockSpec((1, 1, w), lambda b, i: (b, 0, 0))
    full = lambda *s: pl.BlockSpec(s, lambda b, i: (0,) * len(s))
    shp = lambda w: jax.ShapeDtypeStruct((bt, n, w), F32)
    planes = pl.BlockSpec((1, lru_w // LANES, tm, LANES), lambda b, i: (b, 0, i, 0))
    return pl.pallas_call(
        functools.partial(_inproj_kernel, row_w=row_w, lru_w=lru_w, conv_w=conv_w, sub=min(tm, SUB_ROWS)),
        grid=(bt, n // tm),
        in_specs=[tok(d), vec(d), vec(d), full(1, d),
                  pl.BlockSpec((None, d, in_cols), lambda b, i: (layer, 0, 0), pipeline_mode=pl.Buffered(1)),
                  full(*cw.shape), full(1, lru_w),
                  full(ng, GATE_GROUP, 4 * GATE_GROUP), full(ng, 1, 4 * GATE_GROUP), full(1, 2 * lru_w),
                  full(*scw.shape), full(1, conv_w), full(2, fft_w, 2 * fft_w)],
        out_specs=[planes] * 4 + [tok(lru_w), tok(conv_w), tok(2 * fft_w)],
        out_shape=[jax.ShapeDtypeStruct((bt, lru_w // LANES, n, LANES), F32)] * 4
        + [shp(lru_w), shp(conv_w), shp(2 * fft_w)],
        scratch_shapes=[pltpu.VMEM((d, in_cols), BF16)],
        compiler_params=_cparams(("arbitrary", "arbitrary")),
        name="inproj_local",
    )(x, shift, scale, g1, w_in, cw, cb, wg, bg, lam, scw, gsc, dftc)


def _scan_kernel(*refs, reverse, add_other):
    if add_other:
        a_ref, b_ref, h0_ref, other_ref, h_ref, hl_ref, carry_ref = refs
    else:
        a_ref, b_ref, h0_ref, h_ref, hl_ref, carry_ref = refs
        other_ref = None

    @pl.when(pl.program_id(2) == 0)
    def _():
        carry_ref[...] = jnp.broadcast_to(h0_ref[0], carry_ref.shape)

    tb = a_ref.shape[2]
    group = SUBLANES * SUBLANES
    row = lax.broadcasted_iota(jnp.int32, (SUBLANES, LANES), 0)
    steps = range(SUBLANES - 1, -1, -1) if reverse else range(SUBLANES)
    first = SUBLANES - 1 if reverse else 0
    last = 0 if reverse else SUBLANES - 1
    groups = range(tb // group - 1, -1, -1) if reverse else range(tb // group)

    def shift_chunks(v, k):
        return pltpu.roll(v, (SUBLANES - k) if reverse else k, axis=0)

    carry = carry_ref[...]
    for g in groups:
        rows = [pl.ds(g * group + s, SUBLANES, stride=SUBLANES) for s in range(SUBLANES)]
        hs, ps = {}, {}
        h = p = None
        for s in steps:
            a = a_ref[0, 0, rows[s], :]
            b = b_ref[0, 0, rows[s], :]
            h = b if h is None else a * h + b
            p = a if p is None else p * a
            hs[s], ps[s] = h, p
        pe, he = p, h
        for k in (1, 2, 4):
            valid = (row <= SUBLANES - 1 - k) if reverse else (row >= k)
            he = jnp.where(valid, he + pe * shift_chunks(he, k), he)
            pe = jnp.where(valid, pe * shift_chunks(pe, k), pe)
        end = he + pe * carry
        h_in = jnp.where(row == first, carry, shift_chunks(end, 1))
        carry = jnp.broadcast_to(end[last:last + 1], end.shape)
        for s in range(SUBLANES):
            out = hs[s] + ps[s] * h_in
            if other_ref is not None:
                out = out + other_ref[0, 0, rows[s], :]
            h_ref[0, 0, rows[s], :] = out
    carry_ref[...] = carry
    hl_ref[0] = carry[0:1]


def _scan_call(a, b, h0, other, *, reverse, tb):
    bt, npl, n, _ = a.shape
    nt = n // tb
    blk = (lambda bi, l, i: (bi, l, nt - 1 - i, 0)) if reverse else (lambda bi, l, i: (bi, l, i, 0))
    tok = pl.BlockSpec((1, 1, tb, LANES), blk)
    st = pl.BlockSpec((1, 1, LANES), lambda bi, l, i: (bi, 0, l))
    ins = [a, b, h0] + ([other] if other is not None else [])
    in_specs = [tok, tok, st] + ([tok] if other is not None else [])
    return pl.pallas_call(
        functools.partial(_scan_kernel, reverse=reverse, add_other=other is not None),
        grid=(bt, npl, nt),
        in_specs=in_specs,
        out_specs=[tok, st],
        out_shape=[jax.ShapeDtypeStruct(a.shape, F32), jax.ShapeDtypeStruct((bt, 1, npl * LANES), F32)],
        scratch_shapes=[pltpu.VMEM((SUBLANES, LANES), F32)],
        compiler_params=_cparams(("parallel", "parallel", "arbitrary")),
        name="lru_scan_bwd" if reverse else "lru_scan_fwd",
    )(*ins)


def _cos_sin(n):
    k = np.arange(n, dtype=np.float64)
    ang = 2.0 * np.pi * np.outer(k, k) / n
    return np.cos(ang), np.sin(ang)


def _slab_copies(src_hbm, dst_ref, sem, b, first, count, width, slot):
    return [pltpu.make_async_copy(src_hbm.at[b, :, first + j, :], dst_ref.at[slot, :, pl.ds(j * width, width)],
                                  sem.at[slot]) for j in range(count)]


def _prefetch_slabs(src_hbm, dst_ref, sem, count, width):
    b, i = pl.program_id(0), pl.program_id(1)
    ni = pl.num_programs(1)
    step = b * ni + i
    slot = step % 2

    @pl.when(step == 0)
    def _():
        for cp in _slab_copies(src_hbm, dst_ref, sem, b, i * count, count, width, slot):
            cp.start()

    @pl.when(step + 1 < pl.num_programs(0) * ni)
    def _():
        wrap = i + 1 == ni
        bn = jnp.where(wrap, b + 1, b)
        nxt = jnp.where(wrap, 0, i + 1)
        for cp in _slab_copies(src_hbm, dst_ref, sem, bn, nxt * count, count, width, 1 - slot):
            cp.start()

    for cp in _slab_copies(src_hbm, dst_ref, sem, b, i * count, count, width, slot):
        cp.wait()
    return step, slot


def _dft1_kernel(x_hbm, m_ref, tc_ref, ts_ref, o_ref, xs_ref, sem, *, n2, jn, fw):
    _, slot = _prefetch_slabs(x_hbm, xs_ref, sem, jn, 2 * fw)
    r = _dot3_const_lhs(m_ref, xs_ref[slot])
    for j in range(jn):
        base = j * 2 * fw
        c_uc = r[:n2, base:base + fw]
        c_us = r[:n2, base + fw:base + 2 * fw]
        s_uc = r[n2:, base:base + fw]
        s_us = r[n2:, base + fw:base + 2 * fw]
        br = c_uc - s_us
        bi = -(c_us + s_uc)
        tc = tc_ref[0, :, j:j + 1]
        ts = ts_ref[0, :, j:j + 1]
        o_ref[0, j, :, :fw] = br * tc + bi * ts
        o_ref[0, j, :, fw:] = bi * tc - br * ts


def _dft2_kernel(p_hbm, m_ref, g_ref, y_hbm, ps_ref, ys_ref, sem_in, sem_out, *, n1, kn, fw):
    step, slot = _prefetch_slabs(p_hbm, ps_ref, sem_in, kn, 2 * fw)
    b, i = pl.program_id(0), pl.program_id(1)
    last = pl.num_programs(0) * pl.num_programs(1) - 1

    def out_copies(s):
        return [pltpu.make_async_copy(ys_ref.at[s, :, pl.ds(k * fw, fw)], y_hbm.at[b, :, i * kn + k, :],
                                      sem_out.at[s]) for k in range(kn)]

    r = _dot3_const_lhs(m_ref, ps_ref[slot])

    @pl.when(step >= 2)
    def _():
        for cp in out_copies(slot):
            cp.wait()

    for k in range(kn):
        base = k * 2 * fw
        y = r[:n1, base:base + fw] + r[n1:, base + fw:base + 2 * fw]
        ys_ref[slot, :, k * fw:(k + 1) * fw] = _rms(y, g_ref[...])
    for cp in out_copies(slot):
        cp.start()

    @pl.when(step == last)
    def _():
        for cp in out_copies(slot):
            cp.wait()

    @pl.when((step == last) & (step >= 1))
    def _():
        for cp in out_copies(1 - slot):
            cp.wait()


def _dft_direct_kernel(x_ref, m_ref, g_ref, o_ref, *, n, fw):
    x = x_ref[0]
    r = _dot3_const_lhs(m_ref, x)
    y = r[:n, :fw] - r[n:, fw:]
    o_ref[0] = _rms(y, g_ref[...])


def _fourier_call(ucs, g_fft):
    bt, n, fw2 = ucs.shape
    fw = fw2 // 2
    gd = fw // FFT_GROUPS
    scale = 1.0 / math.sqrt(n * gd)
    if n <= 512:
        c, s = _cos_sin(n)
        m = _split_const(np.concatenate([c, s], 0) * scale)
        return pl.pallas_call(
            functools.partial(_dft_direct_kernel, n=n, fw=fw),
            grid=(bt,),
            in_specs=[pl.BlockSpec((1, n, fw2), lambda b: (b, 0, 0)),
                      pl.BlockSpec((2, 2 * n, n), lambda b: (0, 0, 0)),
                      pl.BlockSpec((1, fw), lambda b: (0, 0))],
            out_specs=pl.BlockSpec((1, n, fw), lambda b: (b, 0, 0)),
            out_shape=jax.ShapeDtypeStruct((bt, n, fw), F32),
            compiler_params=_cparams(("parallel",)),
            name="dft_direct",
        )(ucs, m, g_fft)

    n1 = LANES
    n2 = n // n1
    jn = 16
    c2, s2 = _cos_sin(n2)
    m1 = _split_const(np.concatenate([c2, s2], 0))
    ang = 2.0 * np.pi * np.outer(np.arange(n2), np.arange(n1)) / n
    tw = lambda f: jnp.asarray(f(ang).reshape(n2, n1 // jn, jn).transpose(1, 0, 2), F32)
    p = pl.pallas_call(
        functools.partial(_dft1_kernel, n2=n2, jn=jn, fw=fw),
        grid=(bt, n1 // jn),
        in_specs=[pl.BlockSpec(memory_space=pl.ANY),
                  pl.BlockSpec((2, 2 * n2, n2), lambda b, i: (0, 0, 0)),
                  pl.BlockSpec((1, n2, jn), lambda b, i: (i, 0, 0)),
                  pl.BlockSpec((1, n2, jn), lambda b, i: (i, 0, 0))],
        out_specs=pl.BlockSpec((1, jn, n2, fw2), lambda b, i: (b, i, 0, 0)),
        out_shape=jax.ShapeDtypeStruct((bt, n1, n2, fw2), F32),
        scratch_shapes=[pltpu.VMEM((2, n2, jn * fw2), F32), pltpu.SemaphoreType.DMA((2,))],
        compiler_params=_cparams(("arbitrary", "arbitrary")),
        name="dft_stage1",
    )(ucs.reshape(bt, n2, n1, fw2), m1, tw(np.cos), tw(np.sin))

    kn = min(8, n2)
    c1, s1 = _cos_sin(n1)
    m2 = _split_const(np.concatenate([c1, s1], 0) * scale)
    y = pl.pallas_call(
        functools.partial(_dft2_kernel, n1=n1, kn=kn, fw=fw),
        grid=(bt, n2 // kn),
        in_specs=[pl.BlockSpec(memory_space=pl.ANY),
                  pl.BlockSpec((2, 2 * n1, n1), lambda b, i: (0, 0, 0)),
                  pl.BlockSpec((1, fw), lambda b, i: (0, 0))],
        out_specs=pl.BlockSpec(memory_space=pl.ANY),
        out_shape=jax.ShapeDtypeStruct((bt, n1, n2, fw), F32),
        scratch_shapes=[pltpu.VMEM((2, n1, kn * fw2), F32), pltpu.VMEM((2, n1, kn * fw), F32),
                        pltpu.SemaphoreType.DMA((2,)), pltpu.SemaphoreType.DMA((2,))],
        compiler_params=_cparams(("arbitrary", "arbitrary")),
        name="dft_stage2",
    )(p, m2, g_fft)
    return y.reshape(bt, n, fw)


def _outproj_kernel(hs_ref, lg_ref, ysc_ref, yfft_ref, x_ref, wout_ref, glru_ref, gate_ref,
                    sh_ref, sc_ref, g2_ref, wr_ref, xn_ref, h2_ref, aff_ref, woutb_ref, *, sub):
    @pl.when((pl.program_id(0) == 0) & (pl.program_id(1) == 0))
    def _():
        woutb_ref[...] = wout_ref[...].astype(BF16)

    tm, d = x_ref.shape[1], x_ref.shape[2]
    tile = d // LANES
    gmod = g2_ref[...] * (1.0 + sc_ref[0])
    for t in range(tm // sub):
        rs = slice(t * sub, (t + 1) * sub)
        hs = jnp.concatenate([hs_ref[0, k, rs, :] for k in range(hs_ref.shape[1])], axis=-1)
        y_lru = _rms(hs * lg_ref[0, rs, :], glru_ref[...])
        y = jnp.concatenate([y_lru.astype(BF16), ysc_ref[0, rs, :].astype(BF16), yfft_ref[0, rs, :].astype(BF16)],
                            axis=-1)
        xn = x_ref[0, rs, :] + gate_ref[0] * _dot(y, woutb_ref[...])
        for k in range(tile):
            xn_ref[0, pl.ds(t * sub * tile + k, sub, stride=tile), :] = xn[:, k * LANES:(k + 1) * LANES]
        h2 = xn * lax.rsqrt(jnp.mean(xn * xn, axis=-1, keepdims=True) + EPS) * gmod + sh_ref[0]
        for k in range(tile):
            h2_ref[0, pl.ds(t * sub * tile + k, sub, stride=tile), :] = h2[:, k * LANES:(k + 1) * LANES]
        h_hi, h_lo = _split(h2)
        logits = _dot_nt(wr_ref[0], h_hi) + (_dot_nt(wr_ref[1], h_hi) + _dot_nt(wr_ref[0], h_lo))
        m = jnp.max(logits, axis=0, keepdims=True)
        e = jnp.exp(logits - m)
        aff_ref[0, :, rs] = e / jnp.sum(e, axis=0, keepdims=True)


def _outproj_call(hs, lg, ysc, yfft, x, w_out, layer, g_lru, gate, shift, scale, g2, w_router_t, *, tm):
    bt, n, d = x.shape
    ne = w_router_t.shape[1]
    tok = lambda w: pl.BlockSpec((1, tm, w), lambda b, i: (b, i, 0))
    vec = lambda w: pl.BlockSpec((1, 1, w), lambda b, i: (b, 0, 0))
    full = lambda *s: pl.BlockSpec(s, lambda b, i: (0,) * len(s))
    tiles = pl.BlockSpec((1, tm * d // LANES, LANES), lambda b, i: (b, i, 0))
    return pl.pallas_call(
        functools.partial(_outproj_kernel, sub=min(tm, SUB_ROWS)),
        grid=(bt, n // tm),
        in_specs=[pl.BlockSpec((1, hs.shape[1], tm, LANES), lambda b, i: (b, 0, i, 0)),
                  tok(lg.shape[2]), tok(ysc.shape[2]), tok(yfft.shape[2]), tok(d),
                  pl.BlockSpec((None,) + w_out.shape[1:], lambda b, i: (layer, 0, 0), pipeline_mode=pl.Buffered(1)),
                  full(1, lg.shape[2]), vec(d), vec(d), vec(d), full(1, d), full(2, ne, d)],
        out_specs=[tiles, tiles, pl.BlockSpec((1, ne, tm), lambda b, i: (b, 0, i))],
        out_shape=[jax.ShapeDtypeStruct((bt, n * d // LANES, LANES), F32)] * 2 + [
                   jax.ShapeDtypeStruct((bt, ne, n), F32)],
        scratch_shapes=[pltpu.VMEM(w_out.shape[1:], BF16)],
        compiler_params=_cparams(("arbitrary", "arbitrary")),
        name="outproj_router",
    )(hs, lg, ysc, yfft, x, w_out, g_lru, gate, shift, scale, g2, w_router_t)


def _select_kernel(aff_ref, idx_ref, g_ref, *, cap):
    aff = aff_ref[0]
    ne, nb, _ = aff.shape

    def count(mask):
        c = jnp.sum(mask.astype(jnp.int32), axis=1, keepdims=True)
        return jnp.sum(c, axis=2, keepdims=True)

    def bit_step(i, t):
        cand = t | (jnp.int32(1) << (30 - i))
        return jnp.where(count(aff >= lax.bitcast_convert_type(cand, F32)) >= cap, cand, t)

    bits = lax.fori_loop(0, 31, bit_step, jnp.zeros((ne, 1, 1), jnp.int32))
    lo = lax.bitcast_convert_type(bits, F32)
    hi = lax.bitcast_convert_type(bits + 1, F32)

    def mid_step(i, lh):
        lo, hi = lh
        mid = (lo + hi) * 0.5
        ok = count(aff >= mid) >= cap
        return jnp.where(ok, mid, lo), jnp.where(ok, hi, mid)

    lo, hi = lax.fori_loop(0, 14, mid_step, (lo, hi))
    gt = aff >= hi
    eq = (aff >= lo) & (aff < hi)
    need = (cap - count(gt)).astype(F32)

    lane = lax.broadcasted_iota(jnp.int32, (LANES, LANES), 0)
    lane_t = lax.broadcasted_iota(jnp.int32, (LANES, LANES), 1)
    tri_incl = (lane <= lane_t).astype(BF16)
    blk = lax.broadcasted_iota(jnp.int32, (nb, nb), 0)
    blk_t = lax.broadcasted_iota(jnp.int32, (nb, nb), 1)
    tri_blk = (blk <= blk_t).astype(BF16)
    ones_row = jnp.ones((SUBLANES, LANES), BF16)
    kvals = lax.broadcasted_iota(jnp.int32, (SUBLANES, nb), 1).astype(BF16)
    slot = lax.broadcasted_iota(jnp.int32, (cap, 1), 0).astype(F32)

    def block_prefix(mask_bf16):
        lc = _dot(mask_bf16, tri_incl)
        cnt_row = _dot_nt(ones_row, mask_bf16)
        inc_row = _dot(cnt_row.astype(BF16), tri_blk)
        return lc, inc_row - cnt_row, inc_row

    for e in range(ne):
        eq_e = eq[e].astype(BF16)
        lc_eq, off_eq, _ = block_prefix(eq_e)
        tri_strict = (blk_t < blk).astype(BF16)
        cnt_col = lc_eq[:, LANES - 1:LANES]
        before = _dot(tri_strict, jnp.broadcast_to(cnt_col, (nb, LANES)).astype(BF16))[:, :1]
        rank = lc_eq - eq[e].astype(F32) + before
        sel = gt[e] | (eq[e] & (rank < need[e]))
        sel_bf = sel.astype(BF16)

        lc, off_row, inc_row = block_prefix(sel_bf)
        off1 = off_row[0:1]
        inc1 = inc_row[0:1]
        onehot = ((slot >= off1) & (slot < inc1))
        oh_bf = onehot.astype(BF16)
        offk = jnp.sum(jnp.where(onehot, off1, 0.0), axis=1, keepdims=True)
        jl = slot - offk
        m = _dot(oh_bf, lc.astype(BF16))
        below = (m <= jl)
        kb_row = _dot_nt(kvals, oh_bf)
        r_row = _dot_nt(ones_row, below.astype(BF16))
        idx_ref[0, e:e + 1, :] = (kb_row[0:1] * float(LANES) + r_row[0:1]).astype(jnp.int32)

        msel = _dot(oh_bf, sel_bf)
        hit = (m == jl + 1.0) & (msel > 0.5)
        a_hi, a_mid = _split(aff[e])
        a_lo = (aff[e] - a_hi.astype(F32) - a_mid.astype(F32)).astype(BF16)
        aff_rows = _dot(oh_bf, a_hi) + (_dot(oh_bf, a_mid) + _dot(oh_bf, a_lo))
        g_ref[0, e] = jnp.sum(jnp.where(hit, aff_rows, 0.0), axis=1, keepdims=True)


def _select_call(aff_t, cap):
    bt, ne, n = aff_t.shape
    nb = n // LANES
    return pl.pallas_call(
        functools.partial(_select_kernel, cap=cap),
        grid=(bt,),
        in_specs=[pl.BlockSpec((1, ne, nb, LANES), lambda b: (b, 0, 0, 0))],
        out_specs=[pl.BlockSpec((1, ne, cap), lambda b: (b, 0, 0)),
                   pl.BlockSpec((1, ne, cap, 1), lambda b: (b, 0, 0, 0))],
        out_shape=[jax.ShapeDtypeStruct((bt, ne, cap), jnp.int32),
                   jax.ShapeDtypeStruct((bt, ne, cap, 1), F32)],
        compiler_params=_cparams(("parallel",)),
        name="expert_select",
    )(aff_t.reshape(bt, ne, nb, LANES))


def _ffn_kernel(idx_ref, idxn_ref, h_hbm, g_ref, gate2_ref, wg_ref, wu_ref, wd_ref, o_ref,
                xs_ref, wgb_ref, wub_ref, wdb_ref, sem, *, cap, chunk, gs, d):
    e = pl.program_id(0)
    bb = pl.program_id(1)
    nbb = pl.num_programs(1)
    step = e * nbb + bb
    slot = step % 2
    rows = gs * cap
    tile = d // LANES

    def row_copy(ids_ref, b0, dst_slot, s, j):
        tok = pl.multiple_of(ids_ref[s, 0, 0, j] * tile, tile)
        return pltpu.make_async_copy(h_hbm.at[b0 + s, pl.ds(tok, tile), :],
                                     xs_ref.at[dst_slot, pl.ds((s * cap + j) * tile, tile), :], sem.at[dst_slot])

    def wait_slot(s):
        pltpu.make_async_copy(h_hbm.at[0, pl.ds(0, rows * tile), :], xs_ref.at[s], sem.at[s]).wait()

    @pl.when(step == 0)
    def _():
        for s in range(gs):
            def start(j, carry):
                row_copy(idx_ref, bb * gs, slot, s, j).start()
                return carry
            lax.fori_loop(0, cap, start, 0, unroll=8)

    @pl.when(bb == 0)
    def _():
        wgb_ref[...] = wg_ref[0, 0].astype(BF16)
        wub_ref[...] = wu_ref[0, 0].astype(BF16)
        wdb_ref[...] = wd_ref[0, 0].astype(BF16)

    wait_slot(slot)

    bn = ((bb + 1) % nbb) * gs
    for c in range(rows // chunk):
        pieces = [xs_ref[slot, pl.ds(c * chunk * tile + k, chunk, stride=tile), :].astype(BF16) for k in range(tile)]
        xb = jnp.concatenate(pieces, axis=-1)
        gate = _dot(xb, wgb_ref[...])
        up = _dot(xb, wub_ref[...])
        for r in range(c * chunk, (c + 1) * chunk):
            row_copy(idxn_ref, bn, 1 - slot, r // cap, r % cap).start()
        hid = (gate * jax.nn.sigmoid(gate)) * up
        y = _dot(hid.astype(BF16), wdb_ref[...])
        s, r0 = (c * chunk) // cap, (c * chunk) % cap
        zc = (y * g_ref[s, 0, r0:r0 + chunk, :]) * gate2_ref[s]
        for k in range(tile):
            o_ref[s, 0, pl.ds(r0 * tile + k, chunk, stride=tile), :] = zc[:, k * LANES:(k + 1) * LANES]

    @pl.when(step == pl.num_programs(0) * nbb - 1)
    def _():
        wait_slot(1 - slot)


def _ffn_call(idx, h2t, g, gate2, wg, wu, wd, layer, *, gs):
    bt, ne, cap = idx.shape
    d = gate2.shape[2]
    ff = wg.shape[3]
    nbb = bt // gs
    chunk = min(cap, 256)
    wspec = lambda s: pl.BlockSpec((1, 1) + s, lambda e, b: (layer, e, 0, 0))
    idx4 = idx.reshape(bt, ne, 1, cap)
    nxt = lambda e, b: ((b + 1) % nbb, jnp.minimum(e + (b + 1) // nbb, ne - 1), 0, 0)
    return pl.pallas_call(
        functools.partial(_ffn_kernel, cap=cap, chunk=chunk, gs=gs, d=d),
        grid=(ne, nbb),
        in_specs=[pl.BlockSpec((gs, 1, 1, cap), lambda e, b: (b, e, 0, 0), memory_space=pltpu.SMEM),
                  pl.BlockSpec((gs, 1, 1, cap), nxt, memory_space=pltpu.SMEM),
                  pl.BlockSpec(memory_space=pl.ANY),
                  pl.BlockSpec((gs, 1, cap, 1), lambda e, b: (b, e, 0, 0)),
                  pl.BlockSpec((gs, 1, d), lambda e, b: (b, 0, 0)),
                  wspec((d, ff)), wspec((d, ff)), wspec((ff, d))],
        out_specs=pl.BlockSpec((gs, 1, cap * d // LANES, LANES), lambda e, b: (b, e, 0, 0)),
        out_shape=jax.ShapeDtypeStruct((bt, ne, cap * d // LANES, LANES), F32),
        scratch_shapes=[pltpu.VMEM((2, gs * cap * d // LANES, LANES), F32), pltpu.VMEM((d, ff), BF16),
                        pltpu.VMEM((d, ff), BF16), pltpu.VMEM((ff, d), BF16), pltpu.SemaphoreType.DMA((2,))],
        compiler_params=_cparams(("arbitrary", "arbitrary")),
        name="expert_ffn",
    )(idx4, idx4, h2t, g, gate2, wg, wu, wd)


def _combine_kernel(idx_ref, x_hbm, z_ref, gfin_ref, o_hbm, acc_ref, stage_ref, sem_in, sem_out, *,
                    cap, final_norm, nchunk, d):
    b = pl.program_id(0)
    e = pl.program_id(1)
    tile = d // LANES
    rc = acc_ref.shape[0] // tile // nchunk

    def in_copy(bi, c):
        rs = pl.ds(c * rc * tile, rc * tile)
        return pltpu.make_async_copy(x_hbm.at[bi, rs, :], acc_ref.at[rs, :], sem_in.at[c])

    def out_copy(c):
        return pltpu.make_async_copy(stage_ref.at[c % 2], o_hbm.at[b, pl.ds(c * rc, rc), :], sem_out.at[c % 2])

    @pl.when((e == 0) & (b == 0))
    def _():
        for c in range(nchunk):
            in_copy(b, c).start()

    @pl.when(e == 0)
    def _():
        for c in range(nchunk):
            in_copy(b, c).wait()

    group = 16
    for j0 in range(0, cap, group):
        toks = [pl.ds(pl.multiple_of(idx_ref[0, 0, 0, j0 + u] * tile, tile), tile) for u in range(group)]
        vals = [acc_ref[toks[u], :] + z_ref[0, 0, (j0 + u) * tile:(j0 + u + 1) * tile, :] for u in range(group)]
        for u in range(group):
            acc_ref[toks[u], :] = vals[u]

    @pl.when(e == pl.num_programs(1) - 1)
    def _():
        for c in range(nchunk):
            if c >= 2:
                out_copy(c - 2).wait()
            v = jnp.concatenate([acc_ref[pl.ds(c * rc * tile + k, rc, stride=tile), :] for k in range(tile)], axis=-1)
            if final_norm:
                v = _rms(v, gfin_ref[...])
            stage_ref[c % 2] = v

            @pl.when(b + 1 < pl.num_programs(0))
            def _():
                in_copy(b + 1, c).start()

            out_copy(c).start()
        for c in range(max(nchunk - 2, 0), nchunk):
            out_copy(c).wait()


def _combine_call(idx, xt, z, g_final, *, final_norm):
    bt, ne, cap = idx.shape
    d = g_final.shape[1]
    tile = d // LANES
    n = xt.shape[1] // tile
    nchunk = 32
    return pl.pallas_call(
        functools.partial(_combine_kernel, cap=cap, final_norm=final_norm, nchunk=nchunk, d=d),
        grid=(bt, ne),
        in_specs=[pl.BlockSpec((1, 1, 1, cap), lambda b, e: (b, e, 0, 0), memory_space=pltpu.SMEM),
                  pl.BlockSpec(memory_space=pl.ANY),
                  pl.BlockSpec((1, 1, cap * tile, LANES), lambda b, e: (b, e, 0, 0)),
                  pl.BlockSpec((1, d), lambda b, e: (0, 0))],
        out_specs=pl.BlockSpec(memory_space=pl.ANY),
        out_shape=jax.ShapeDtypeStruct((bt, n, d), F32),
        scratch_shapes=[pltpu.VMEM((n * tile, LANES), F32), pltpu.VMEM((2, n // nchunk, d), F32),
                        pltpu.SemaphoreType.DMA((nchunk,)), pltpu.SemaphoreType.DMA((2,))],
        compiler_params=_cparams(("arbitrary", "arbitrary")),
        name="expert_combine",
    )(idx.reshape(bt, ne, 1, cap), xt, z, g_final)


def _block_diag(w):
    h, hd, _ = w.shape
    eye = jnp.eye(h, dtype=w.dtype)
    return (w[:, :, None, :] * eye[:, None, :, None]).reshape(h * hd, h * hd)


def _gate_weights(wr, br, wi, bi):
    dense = [_block_diag(wr[0]), _block_diag(wi[0]), _block_diag(wr[1]), _block_diag(wi[1])]
    bias = [br[0], bi[0], br[1], bi[1]]
    lru_w = dense[0].shape[0]
    wg, bg = [], []
    for g in range(lru_w // GATE_GROUP):
        cs = slice(g * GATE_GROUP, (g + 1) * GATE_GROUP)
        wg.append(jnp.concatenate([m[cs, cs] for m in dense], axis=1))
        bg.append(jnp.concatenate([v[cs] for v in bias])[None, :])
    return jnp.stack(wg).astype(BF16), jnp.stack(bg)


def _channel_dft(fw):
    gd = fw // FFT_GROUPS
    c, s = _cos_sin(gd)
    eye = np.eye(FFT_GROUPS)
    return _split_const(np.concatenate([np.kron(eye, c), np.kron(eye, s)], axis=1))


def _mixer(x, lp, shift, scale, h0_f, h0_b, *, row_w, tm, tb, need_out):
    af, bf, ab, bb, lg, ysc, ucs = _inproj_call(
        x, shift, scale, lp["g1"], lp["w_in"], lp["layer"], lp["cw"], lp["cb"], lp["wg"], lp["bg"], lp["lam"],
        lp["scw"], lp["g_sc"], lp["dftc"], row_w=row_w, tm=tm)
    hf, sf = _scan_call(af, bf, h0_f, None, reverse=False, tb=tb)
    hs, sb = _scan_call(ab, bb, h0_b, hf, reverse=True, tb=tb)
    if not need_out:
        return sf, sb, None
    yfft = _fourier_call(ucs, lp["g_fft"])
    return sf, sb, (hs, lg, ysc, yfft)


def _moe(xt, h2, aff_t, gate2, lp, g_final, *, final_norm, gs):
    n = aff_t.shape[2]
    cap = CAPACITY_FACTOR * n // N_EXPERTS
    pad = (-n) % (SUBLANES * LANES)
    if pad:
        aff_t = jnp.pad(aff_t, ((0, 0), (0, 0), (0, pad)), constant_values=-1.0)
    idx, g = _select_call(aff_t, cap)
    z = _ffn_call(idx, h2, g, gate2, lp["wge"], lp["wue"], lp["wde"], lp["layer"], gs=gs)
    return _combine_call(idx, xt, z, g_final, final_norm=final_norm)


def kernel(x, c, ctx, c_ctx, w_ada, b_ada, g_norm1, w_in, lru_conv_w, lru_conv_b, lru_wr, lru_br, lru_wi,
           lru_bi, lru_lam, sc_conv_w, g_out, w_out, g_norm2, w_router, w_gate_e, w_up_e, w_down_e, g_final):
    depth = w_ada.shape[0]
    bsz, seq, d = x.shape
    ctx_len = ctx.shape[1]
    lru_w = lru_conv_w.shape[2]
    conv_w = sc_conv_w.shape[2]
    fft_w = w_in.shape[2] - 2 * lru_w - 3 * conv_w

    cs = jnp.concatenate([c, c_ctx[None, :], jnp.zeros((SUBLANES - bsz - 1, d), F32)], axis=0)
    mods = _ada_call(cs, w_ada, b_ada)
    dftc = _channel_dft(fft_w)
    gfin = g_final[None, :]
    zero_state = jnp.zeros((bsz, 1, lru_w), F32)

    for l in range(depth):
        last = l == depth - 1
        mx = [mods[l, :bsz, None, k * d:(k + 1) * d] for k in range(6)]
        mc = [jnp.broadcast_to(mods[l, bsz, k * d:(k + 1) * d], (bsz, 1, d)) for k in range(6)]
        wg, bg = _gate_weights(lru_wr[l], lru_br[l], lru_wi[l], lru_bi[l])
        lp = dict(
            g1=g_norm1[l][None, :], w_in=w_in, cw=lru_conv_w[l], cb=lru_conv_b[l][None, :],
            wg=wg, bg=bg, lam=lru_lam[l].reshape(1, 2 * lru_w), scw=sc_conv_w[l],
            g_sc=g_out[l][None, lru_w:lru_w + conv_w], g_fft=g_out[l][None, lru_w + conv_w:], dftc=dftc,
            wge=w_gate_e, wue=w_up_e, wde=w_down_e, layer=l)
        g_lru = g_out[l][None, :lru_w]
        w_router_t = _split_const(w_router[l].T)

        sf, sb, parts = _mixer(ctx, lp, mc[0], mc[1], zero_state, zero_state,
                               row_w=ctx_len, tm=ctx_len, tb=ctx_len, need_out=not last)
        if not last:
            ctx, hc2, aff_c = _outproj_call(*parts, ctx, w_out, l, g_lru, mc[2], mc[3], mc[4],
                                            g_norm2[l][None, :], w_router_t, tm=ctx_len)
            ctx = _moe(ctx, hc2, aff_c, mc[5], lp, gfin, final_norm=False, gs=bsz)

        _, _, parts = _mixer(x, lp, mx[0], mx[1], sf, sb, row_w=GRID_W, tm=512, tb=min(seq, 2048), need_out=True)
        x, hx2, aff_x = _outproj_call(*parts, x, w_out, l, g_lru, mx[2], mx[3], mx[4],
                                      g_norm2[l][None, :], w_router_t, tm=512)
        x = _moe(x, hx2, aff_x, mx[5], lp, gfin, final_norm=last, gs=1)
    return x
```

```python
import functools
import math

import numpy as np
import jax
import jax.numpy as jnp
from jax import lax
from jax.experimental import pallas as pl
from jax.experimental.pallas import tpu as pltpu

F32 = jnp.float32
BF16 = jnp.bfloat16
HI = lax.Precision.HIGHEST

GRID_W = 64
LRU_HEADS = 8
LRU_C = 8.0
N_EXPERTS = 16
CAPACITY_FACTOR = 2
EPS = 1e-6
FFT_GROUPS = 4

LANES = 128
SUBLANES = 8
GATE_GROUP = 256
SUB_ROWS = 512
VMEM_LIMIT = 56 * 1024 * 1024


def _cparams(sem):
    return pltpu.CompilerParams(dimension_semantics=sem, vmem_limit_bytes=VMEM_LIMIT)


def _rms(x, g):
    return x * lax.rsqrt(jnp.mean(x * x, axis=-1, keepdims=True) + EPS) * g


def _dot(a, b):
    return jnp.dot(a, b, preferred_element_type=F32)


def _dot_hi(a, b):
    return jnp.dot(a, b, preferred_element_type=F32, precision=HI)


def _split_const(m):
    m = jnp.asarray(m, F32)
    hi = m.astype(BF16)
    return jnp.stack([hi, (m - hi.astype(F32)).astype(BF16)])


def _split(x):
    hi = x.astype(BF16)
    return hi, (x - hi.astype(F32)).astype(BF16)


def _dot3_const_lhs(m_ref, x):
    x_hi, x_lo = _split(x)
    return _dot(m_ref[0], x_hi) + (_dot(m_ref[1], x_hi) + _dot(m_ref[0], x_lo))


def _dot3_const_rhs(x, m_ref):
    x_hi, x_lo = _split(x)
    return _dot(x_hi, m_ref[0]) + (_dot(x_hi, m_ref[1]) + _dot(x_lo, m_ref[0]))


def _dot_nt(a, b, precision=None):
    return lax.dot_general(a, b, (((1,), (1,)), ((), ())), preferred_element_type=F32, precision=precision)


def _ada_kernel(c_ref, w_ref, b_ref, o_ref):
    c = c_ref[...]
    o_ref[...] = _dot_hi(c * jax.nn.sigmoid(c), w_ref[...]) + b_ref[...]


def _ada_call(cs, w_ada, b_ada):
    depth, d, six_d = w_ada.shape
    nblk = six_d // d
    return pl.pallas_call(
        _ada_kernel,
        grid=(depth, nblk),
        in_specs=[
            pl.BlockSpec((SUBLANES, d), lambda l, j: (0, 0)),
            pl.BlockSpec((None, d, d), lambda l, j: (l, 0, j)),
            pl.BlockSpec((None, 1, d), lambda l, j: (l, 0, j)),
        ],
        out_specs=pl.BlockSpec((None, SUBLANES, d), lambda l, j: (l, 0, j)),
        out_shape=jax.ShapeDtypeStruct((depth, SUBLANES, six_d), F32),
        compiler_params=_cparams(("parallel", "parallel")),
        name="ada_mod",
    )(cs, w_ada, b_ada.reshape(depth, 1, six_d))


def _shift_rows(u, d, pos, row_w):
    n = u.shape[0]
    if d == 0:
        return u
    rolled = pltpu.roll(u, (-d) % n, axis=0)
    valid = (pos + d >= 0) & (pos + d < row_w)
    return jnp.where(valid, rolled, 0.0)


def _conv_rows(u, w_ref, left, pos, row_w):
    out = None
    for k in range(w_ref.shape[0]):
        term = w_ref[k:k + 1, :] * _shift_rows(u, k - left, pos, row_w)
        out = term if out is None else out + term
    return out


def _inproj_kernel(x_ref, sh_ref, sc_ref, g1_ref, win_ref, cw_ref, cb_ref, wg_ref, bg_ref, lam_ref,
                   scw_ref, gsc_ref, dft_ref,
                   af_ref, bf_ref, ab_ref, bb_ref, lg_ref, ysc_ref, ucs_ref, winb_ref, *, row_w, lru_w, conv_w, sub):
    @pl.when((pl.program_id(0) == 0) & (pl.program_id(1) == 0))
    def _():
        winb_ref[...] = win_ref[...].astype(BF16)

    tm = x_ref.shape[1]
    gmod = g1_ref[...] * (1.0 + sc_ref[0])
    shift = sh_ref[0]
    nl = -lam_ref[...]
    nsp = -LRU_C * (jnp.maximum(nl, 0.0) + jnp.log(1.0 + jnp.exp(-jnp.abs(nl))))
    pos = lax.broadcasted_iota(jnp.int32, (sub, 1), 0) % row_w
    out_refs = ((af_ref, bf_ref), (ab_ref, bb_ref))
    o = 2 * lru_w

    for t in range(tm // sub):
        rs = slice(t * sub, (t + 1) * sub)
        x = x_ref[0, rs, :]
        h = x * lax.rsqrt(jnp.mean(x * x, axis=-1, keepdims=True) + EPS) * gmod + shift
        p = _dot(h.astype(BF16), winb_ref[...])

        u = _conv_rows(p[:, :lru_w], cw_ref, 1, pos, row_w) + cb_ref[...]
        for g in range(lru_w // GATE_GROUP):
            cs = slice(g * GATE_GROUP, (g + 1) * GATE_GROUP)
            ug = u[:, cs]
            z = _dot(ug.astype(BF16), wg_ref[g]) + bg_ref[g]
            for d in range(2):
                r = jax.nn.sigmoid(z[:, (2 * d) * GATE_GROUP:(2 * d + 1) * GATE_GROUP])
                i = jax.nn.sigmoid(z[:, (2 * d + 1) * GATE_GROUP:(2 * d + 2) * GATE_GROUP])
                a = jnp.exp(r * nsp[:, d * lru_w + g * GATE_GROUP: d * lru_w + (g + 1) * GATE_GROUP])
                inp = jnp.sqrt(1.0 - a * a) * (i * ug)
                for k in range(GATE_GROUP // LANES):
                    plane = g * (GATE_GROUP // LANES) + k
                    out_refs[d][0][0, plane, rs, :] = a[:, k * LANES:(k + 1) * LANES]
                    out_refs[d][1][0, plane, rs, :] = inp[:, k * LANES:(k + 1) * LANES]

        lg_ref[0, rs, :] = jax.nn.gelu(p[:, lru_w:o], approximate=True)

        sc_b = p[:, o:o + conv_w]
        sc_c = p[:, o + conv_w:o + 2 * conv_w]
        sc_x = p[:, o + 2 * conv_w:o + 3 * conv_w]
        y_sc = sc_b * _conv_rows(sc_c * sc_x, scw_ref, 1, pos, row_w)
        ysc_ref[0, rs, :] = _rms(y_sc, gsc_ref[...])

        ucs_ref[0, rs, :] = _dot3_const_rhs(p[:, o + 3 * conv_w:], dft_ref)


def _inproj_call(x, shift, scale, g1, w_in, layer, cw, cb, wg, bg, lam, scw, gsc, dftc, *, row_w, tm):
    bt, n, d = x.shape
    in_cols = w_in.shape[2]
    lru_w = cw.shape[1]
    conv_w = scw.shape[1]
    fft_w = in_cols - 2 * lru_w - 3 * conv_w
    ng = lru_w // GATE_GROUP
    tok = lambda w: pl.BlockSpec((1, tm, w), lambda b, i: (b, i, 0))
    vec = lambda w: pl.BlockSpec((1, 1, w), lambda b, i: (b, 0, 0))
    full = lambda *s: pl.BlockSpec(s, lambda b, i: (0,) * len(s))
    shp = lambda w: jax.ShapeDtypeStruct((bt, n, w), F32)
    planes = pl.BlockSpec((1, lru_w // LANES, tm, LANES), lambda b, i: (b, 0, i, 0))
    return pl.pallas_call(
        functools.partial(_inproj_kernel, row_w=row_w, lru_w=lru_w, conv_w=conv_w, sub=min(tm, SUB_ROWS)),
        grid=(bt, n // tm),
        in_specs=[tok(d), vec(d), vec(d), full(1, d),
                  pl.BlockSpec((None, d, in_cols), lambda b, i: (layer, 0, 0), pipeline_mode=pl.Buffered(1)),
                  full(*cw.shape), full(1, lru_w),
                  full(ng, GATE_GROUP, 4 * GATE_GROUP), full(ng, 1, 4 * GATE_GROUP), full(1, 2 * lru_w),
                  full(*scw.shape), full(1, conv_w), full(2, fft_w, 2 * fft_w)],
        out_specs=[planes] * 4 + [tok(lru_w), tok(conv_w), tok(2 * fft_w)],
        out_shape=[jax.ShapeDtypeStruct((bt, lru_w // LANES, n, LANES), F32)] * 4
        + [shp(lru_w), shp(conv_w), shp(2 * fft_w)],
        scratch_shapes=[pltpu.VMEM((d, in_cols), BF16)],
        compiler_params=_cparams(("arbitrary", "arbitrary")),
        name="inproj_local",
    )(x, shift, scale, g1, w_in, cw, cb, wg, bg, lam, scw, gsc, dftc)


def _scan_kernel(*refs, reverse, add_other):
    if add_other:
        a_ref, b_ref, h0_ref, other_ref, h_ref, hl_ref, carry_ref = refs
    else:
        a_ref, b_ref, h0_ref, h_ref, hl_ref, carry_ref = refs
        other_ref = None

    @pl.when(pl.program_id(2) == 0)
    def _():
        carry_ref[...] = jnp.broadcast_to(h0_ref[0], carry_ref.shape)

    tb = a_ref.shape[2]
    group = SUBLANES * SUBLANES
    row = lax.broadcasted_iota(jnp.int32, (SUBLANES, LANES), 0)
    steps = range(SUBLANES - 1, -1, -1) if reverse else range(SUBLANES)
    first = SUBLANES - 1 if reverse else 0
    last = 0 if reverse else SUBLANES - 1
    groups = range(tb // group - 1, -1, -1) if reverse else range(tb // group)

    def shift_chunks(v, k):
        return pltpu.roll(v, (SUBLANES - k) if reverse else k, axis=0)

    carry = carry_ref[...]
    for g in groups:
        rows = [pl.ds(g * group + s, SUBLANES, stride=SUBLANES) for s in range(SUBLANES)]
        hs, ps = {}, {}
        h = p = None
        for s in steps:
            a = a_ref[0, 0, rows[s], :]
            b = b_ref[0, 0, rows[s], :]
            h = b if h is None else a * h + b
            p = a if p is None else p * a
            hs[s], ps[s] = h, p
        pe, he = p, h
        for k in (1, 2, 4):
            valid = (row <= SUBLANES - 1 - k) if reverse else (row >= k)
            he = jnp.where(valid, he + pe * shift_chunks(he, k), he)
            pe = jnp.where(valid, pe * shift_chunks(pe, k), pe)
        end = he + pe * carry
        h_in = jnp.where(row == first, carry, shift_chunks(end, 1))
        carry = jnp.broadcast_to(end[last:last + 1], end.shape)
        for s in range(SUBLANES):
            out = hs[s] + ps[s] * h_in
            if other_ref is not None:
                out = out + other_ref[0, 0, rows[s], :]
            h_ref[0, 0, rows[s], :] = out
    carry_ref[...] = carry
    hl_ref[0] = carry[0:1]


def _scan_call(a, b, h0, other, *, reverse, tb):
    bt, npl, n, _ = a.shape
    nt = n // tb
    blk = (lambda bi, l, i: (bi, l, nt - 1 - i, 0)) if reverse else (lambda bi, l, i: (bi, l, i, 0))
    tok = pl.BlockSpec((1, 1, tb, LANES), blk)
    st = pl.BlockSpec((1, 1, LANES), lambda bi, l, i: (bi, 0, l))
    ins = [a, b, h0] + ([other] if other is not None else [])
    in_specs = [tok, tok, st] + ([tok] if other is not None else [])
    return pl.pallas_call(
        functools.partial(_scan_kernel, reverse=reverse, add_other=other is not None),
        grid=(bt, npl, nt),
        in_specs=in_specs,
        out_specs=[tok, st],
        out_shape=[jax.ShapeDtypeStruct(a.shape, F32), jax.ShapeDtypeStruct((bt, 1, npl * LANES), F32)],
        scratch_shapes=[pltpu.VMEM((SUBLANES, LANES), F32)],
        compiler_params=_cparams(("parallel", "parallel", "arbitrary")),
        name="lru_scan_bwd" if reverse else "lru_scan_fwd",
    )(*ins)


def _cos_sin(n):
    k = np.arange(n, dtype=np.float64)
    ang = 2.0 * np.pi * np.outer(k, k) / n
    return np.cos(ang), np.sin(ang)


def _slab_copies(src_hbm, dst_ref, sem, b, first, count, width, slot):
    return [pltpu.make_async_copy(src_hbm.at[b, :, first + j, :], dst_ref.at[slot, :, pl.ds(j * width, width)],
                                  sem.at[slot]) for j in range(count)]


def _prefetch_slabs(src_hbm, dst_ref, sem, count, width):
    b, i = pl.program_id(0), pl.program_id(1)
    ni = pl.num_programs(1)
    step = b * ni + i
    slot = step % 2

    @pl.when(step == 0)
    def _():
        for cp in _slab_copies(src_hbm, dst_ref, sem, b, i * count, count, width, slot):
            cp.start()

    @pl.when(step + 1 < pl.num_programs(0) * ni)
    def _():
        wrap = i + 1 == ni
        bn = jnp.where(wrap, b + 1, b)
        nxt = jnp.where(wrap, 0, i + 1)
        for cp in _slab_copies(src_hbm, dst_ref, sem, bn, nxt * count, count, width, 1 - slot):
            cp.start()

    for cp in _slab_copies(src_hbm, dst_ref, sem, b, i * count, count, width, slot):
        cp.wait()
    return step, slot


def _dft1_kernel(x_hbm, m_ref, tc_ref, ts_ref, o_ref, xs_ref, sem, *, n2, jn, fw):
    _, slot = _prefetch_slabs(x_hbm, xs_ref, sem, jn, 2 * fw)
    r = _dot(m_ref[0], xs_ref[slot].astype(BF16))
    for j in range(jn):
        base = j * 2 * fw
        c_uc = r[:n2, base:base + fw]
        c_us = r[:n2, base + fw:base + 2 * fw]
        s_uc = r[n2:, base:base + fw]
        s_us = r[n2:, base + fw:base + 2 * fw]
        br = c_uc - s_us
        bi = -(c_us + s_uc)
        tc = tc_ref[0, :, j:j + 1]
        ts = ts_ref[0, :, j:j + 1]
        o_ref[0, j, :, :fw] = br * tc + bi * ts
        o_ref[0, j, :, fw:] = bi * tc - br * ts


def _dft2_kernel(p_hbm, m_ref, g_ref, y_hbm, ps_ref, ys_ref, sem_in, sem_out, *, n1, kn, fw):
    step, slot = _prefetch_slabs(p_hbm, ps_ref, sem_in, kn, 2 * fw)
    b, i = pl.program_id(0), pl.program_id(1)
    last = pl.num_programs(0) * pl.num_programs(1) - 1

    def out_copies(s):
        return [pltpu.make_async_copy(ys_ref.at[s, :, pl.ds(k * fw, fw)], y_hbm.at[b, :, i * kn + k, :],
                                      sem_out.at[s]) for k in range(kn)]

    r = _dot(m_ref[0], ps_ref[slot].astype(BF16))

    @pl.when(step >= 2)
    def _():
        for cp in out_copies(slot):
            cp.wait()

    for k in range(kn):
        base = k * 2 * fw
        y = r[:n1, base:base + fw] + r[n1:, base + fw:base + 2 * fw]
        ys_ref[slot, :, k * fw:(k + 1) * fw] = _rms(y, g_ref[...])
    for cp in out_copies(slot):
        cp.start()

    @pl.when(step == last)
    def _():
        for cp in out_copies(slot):
            cp.wait()

    @pl.when((step == last) & (step >= 1))
    def _():
        for cp in out_copies(1 - slot):
            cp.wait()


def _dft_direct_kernel(x_ref, m_ref, g_ref, o_ref, *, n, fw):
    x = x_ref[0]
    r = _dot3_const_lhs(m_ref, x)
    y = r[:n, :fw] - r[n:, fw:]
    o_ref[0] = _rms(y, g_ref[...])


def _fourier_call(ucs, g_fft):
    bt, n, fw2 = ucs.shape
    fw = fw2 // 2
    gd = fw // FFT_GROUPS
    scale = 1.0 / math.sqrt(n * gd)
    if n <= 512:
        c, s = _cos_sin(n)
        m = _split_const(np.concatenate([c, s], 0) * scale)
        return pl.pallas_call(
            functools.partial(_dft_direct_kernel, n=n, fw=fw),
            grid=(bt,),
            in_specs=[pl.BlockSpec((1, n, fw2), lambda b: (b, 0, 0)),
                      pl.BlockSpec((2, 2 * n, n), lambda b: (0, 0, 0)),
                      pl.BlockSpec((1, fw), lambda b: (0, 0))],
            out_specs=pl.BlockSpec((1, n, fw), lambda b: (b, 0, 0)),
            out_shape=jax.ShapeDtypeStruct((bt, n, fw), F32),
            compiler_params=_cparams(("parallel",)),
            name="dft_direct",
        )(ucs, m, g_fft)

    n1 = LANES
    n2 = n // n1
    jn = 16
    c2, s2 = _cos_sin(n2)
    m1 = _split_const(np.concatenate([c2, s2], 0))
    ang = 2.0 * np.pi * np.outer(np.arange(n2), np.arange(n1)) / n
    tw = lambda f: jnp.asarray(f(ang).reshape(n2, n1 // jn, jn).transpose(1, 0, 2), F32)
    p = pl.pallas_call(
        functools.partial(_dft1_kernel, n2=n2, jn=jn, fw=fw),
        grid=(bt, n1 // jn),
        in_specs=[pl.BlockSpec(memory_space=pl.ANY),
                  pl.BlockSpec((2, 2 * n2, n2), lambda b, i: (0, 0, 0)),
                  pl.BlockSpec((1, n2, jn), lambda b, i: (i, 0, 0)),
                  pl.BlockSpec((1, n2, jn), lambda b, i: (i, 0, 0))],
        out_specs=pl.BlockSpec((1, jn, n2, fw2), lambda b, i: (b, i, 0, 0)),
        out_shape=jax.ShapeDtypeStruct((bt, n1, n2, fw2), F32),
        scratch_shapes=[pltpu.VMEM((2, n2, jn * fw2), F32), pltpu.SemaphoreType.DMA((2,))],
        compiler_params=_cparams(("arbitrary", "arbitrary")),
        name="dft_stage1",
    )(ucs.reshape(bt, n2, n1, fw2), m1, tw(np.cos), tw(np.sin))

    kn = min(8, n2)
    c1, s1 = _cos_sin(n1)
    m2 = _split_const(np.concatenate([c1, s1], 0) * scale)
    y = pl.pallas_call(
        functools.partial(_dft2_kernel, n1=n1, kn=kn, fw=fw),
        grid=(bt, n2 // kn),
        in_specs=[pl.BlockSpec(memory_space=pl.ANY),
                  pl.BlockSpec((2, 2 * n1, n1), lambda b, i: (0, 0, 0)),
                  pl.BlockSpec((1, fw), lambda b, i: (0, 0))],
        out_specs=pl.BlockSpec(memory_space=pl.ANY),
        out_shape=jax.ShapeDtypeStruct((bt, n1, n2, fw), F32),
        scratch_shapes=[pltpu.VMEM((2, n1, kn * fw2), F32), pltpu.VMEM((2, n1, kn * fw), F32),
                        pltpu.SemaphoreType.DMA((2,)), pltpu.SemaphoreType.DMA((2,))],
        compiler_params=_cparams(("arbitrary", "arbitrary")),
        name="dft_stage2",
    )(p, m2, g_fft)
    return y.reshape(bt, n, fw)


def _outproj_kernel(hs_ref, lg_ref, ysc_ref, yfft_ref, x_ref, wout_ref, glru_ref, gate_ref,
                    sh_ref, sc_ref, g2_ref, wr_ref, xn_ref, h2_ref, aff_ref, woutb_ref, *, sub):
    @pl.when((pl.program_id(0) == 0) & (pl.program_id(1) == 0))
    def _():
        woutb_ref[...] = wout_ref[...].astype(BF16)

    tm, d = x_ref.shape[1], x_ref.shape[2]
    tile = d // LANES
    gmod = g2_ref[...] * (1.0 + sc_ref[0])
    for t in range(tm // sub):
        rs = slice(t * sub, (t + 1) * sub)
        hs = jnp.concatenate([hs_ref[0, k, rs, :] for k in range(hs_ref.shape[1])], axis=-1)
        y_lru = _rms(hs * lg_ref[0, rs, :], glru_ref[...])
        y = jnp.concatenate([y_lru.astype(BF16), ysc_ref[0, rs, :].astype(BF16), yfft_ref[0, rs, :].astype(BF16)],
                            axis=-1)
        xn = x_ref[0, rs, :] + gate_ref[0] * _dot(y, woutb_ref[...])
        xn_ref[0, rs, :] = xn
        h2 = xn * lax.rsqrt(jnp.mean(xn * xn, axis=-1, keepdims=True) + EPS) * gmod + sh_ref[0]
        for k in range(tile):
            h2_ref[0, pl.ds(t * sub * tile + k, sub, stride=tile), :] = h2[:, k * LANES:(k + 1) * LANES]
        h_hi, h_lo = _split(h2)
        logits = _dot_nt(wr_ref[0], h_hi) + (_dot_nt(wr_ref[1], h_hi) + _dot_nt(wr_ref[0], h_lo))
        m = jnp.max(logits, axis=0, keepdims=True)
        e = jnp.exp(logits - m)
        aff_ref[0, :, rs] = e / jnp.sum(e, axis=0, keepdims=True)


def _outproj_call(hs, lg, ysc, yfft, x, w_out, layer, g_lru, gate, shift, scale, g2, w_router_t, *, tm):
    bt, n, d = x.shape
    ne = w_router_t.shape[1]
    tok = lambda w: pl.BlockSpec((1, tm, w), lambda b, i: (b, i, 0))
    vec = lambda w: pl.BlockSpec((1, 1, w), lambda b, i: (b, 0, 0))
    full = lambda *s: pl.BlockSpec(s, lambda b, i: (0,) * len(s))
    return pl.pallas_call(
        functools.partial(_outproj_kernel, sub=min(tm, SUB_ROWS)),
        grid=(bt, n // tm),
        in_specs=[pl.BlockSpec((1, hs.shape[1], tm, LANES), lambda b, i: (b, 0, i, 0)),
                  tok(lg.shape[2]), tok(ysc.shape[2]), tok(yfft.shape[2]), tok(d),
                  pl.BlockSpec((None,) + w_out.shape[1:], lambda b, i: (layer, 0, 0), pipeline_mode=pl.Buffered(1)),
                  full(1, lg.shape[2]), vec(d), vec(d), vec(d), full(1, d), full(2, ne, d)],
        out_specs=[tok(d), pl.BlockSpec((1, tm * d // LANES, LANES), lambda b, i: (b, i, 0)),
                   pl.BlockSpec((1, ne, tm), lambda b, i: (b, 0, i))],
        out_shape=[jax.ShapeDtypeStruct((bt, n, d), F32), jax.ShapeDtypeStruct((bt, n * d // LANES, LANES), F32),
                   jax.ShapeDtypeStruct((bt, ne, n), F32)],
        scratch_shapes=[pltpu.VMEM(w_out.shape[1:], BF16)],
        compiler_params=_cparams(("arbitrary", "arbitrary")),
        name="outproj_router",
    )(hs, lg, ysc, yfft, x, w_out, g_lru, gate, shift, scale, g2, w_router_t)


def _select_kernel(aff_ref, idx_ref, g_ref, *, cap):
    aff = aff_ref[0]
    ne, nb, _ = aff.shape

    def count(mask):
        c = jnp.sum(mask.astype(jnp.int32), axis=1, keepdims=True)
        return jnp.sum(c, axis=2, keepdims=True)

    def bit_step(i, t):
        cand = t | (jnp.int32(1) << (30 - i))
        return jnp.where(count(aff >= lax.bitcast_convert_type(cand, F32)) >= cap, cand, t)

    bits = lax.fori_loop(0, 31, bit_step, jnp.zeros((ne, 1, 1), jnp.int32))
    lo = lax.bitcast_convert_type(bits, F32)
    hi = lax.bitcast_convert_type(bits + 1, F32)

    def mid_step(i, lh):
        lo, hi = lh
        mid = (lo + hi) * 0.5
        ok = count(aff >= mid) >= cap
        return jnp.where(ok, mid, lo), jnp.where(ok, hi, mid)

    lo, hi = lax.fori_loop(0, 14, mid_step, (lo, hi))
    gt = aff >= hi
    eq = (aff >= lo) & (aff < hi)
    need = (cap - count(gt)).astype(F32)

    lane = lax.broadcasted_iota(jnp.int32, (LANES, LANES), 0)
    lane_t = lax.broadcasted_iota(jnp.int32, (LANES, LANES), 1)
    tri_incl = (lane <= lane_t).astype(BF16)
    blk = lax.broadcasted_iota(jnp.int32, (nb, nb), 0)
    blk_t = lax.broadcasted_iota(jnp.int32, (nb, nb), 1)
    tri_blk = (blk <= blk_t).astype(BF16)
    ones_row = jnp.ones((SUBLANES, LANES), BF16)
    kvals = lax.broadcasted_iota(jnp.int32, (SUBLANES, nb), 1).astype(BF16)
    slot = lax.broadcasted_iota(jnp.int32, (cap, 1), 0).astype(F32)

    def block_prefix(mask_bf16):
        lc = _dot(mask_bf16, tri_incl)
        cnt_row = _dot_nt(ones_row, mask_bf16)
        inc_row = _dot(cnt_row.astype(BF16), tri_blk)
        return lc, inc_row - cnt_row, inc_row

    for e in range(ne):
        eq_e = eq[e].astype(BF16)
        lc_eq, off_eq, _ = block_prefix(eq_e)
        tri_strict = (blk_t < blk).astype(BF16)
        cnt_col = lc_eq[:, LANES - 1:LANES]
        before = _dot(tri_strict, jnp.broadcast_to(cnt_col, (nb, LANES)).astype(BF16))[:, :1]
        rank = lc_eq - eq[e].astype(F32) + before
        sel = gt[e] | (eq[e] & (rank < need[e]))
        sel_bf = sel.astype(BF16)

        lc, off_row, inc_row = block_prefix(sel_bf)
        off1 = off_row[0:1]
        inc1 = inc_row[0:1]
        onehot = ((slot >= off1) & (slot < inc1))
        oh_bf = onehot.astype(BF16)
        offk = jnp.sum(jnp.where(onehot, off1, 0.0), axis=1, keepdims=True)
        jl = slot - offk
        m = _dot(oh_bf, lc.astype(BF16))
        below = (m <= jl)
        kb_row = _dot_nt(kvals, oh_bf)
        r_row = _dot_nt(ones_row, below.astype(BF16))
        idx_ref[0, e:e + 1, :] = (kb_row[0:1] * float(LANES) + r_row[0:1]).astype(jnp.int32)

        msel = _dot(oh_bf, sel_bf)
        hit = (m == jl + 1.0) & (msel > 0.5)
        a_hi, a_mid = _split(aff[e])
        a_lo = (aff[e] - a_hi.astype(F32) - a_mid.astype(F32)).astype(BF16)
        aff_rows = _dot(oh_bf, a_hi) + (_dot(oh_bf, a_mid) + _dot(oh_bf, a_lo))
        g_ref[0, e] = jnp.sum(jnp.where(hit, aff_rows, 0.0), axis=1, keepdims=True)


def _select_call(aff_t, cap):
    bt, ne, n = aff_t.shape
    nb = n // LANES
    return pl.pallas_call(
        functools.partial(_select_kernel, cap=cap),
        grid=(bt,),
        in_specs=[pl.BlockSpec((1, ne, nb, LANES), lambda b: (b, 0, 0, 0))],
        out_specs=[pl.BlockSpec((1, ne, cap), lambda b: (b, 0, 0)),
                   pl.BlockSpec((1, ne, cap, 1), lambda b: (b, 0, 0, 0))],
        out_shape=[jax.ShapeDtypeStruct((bt, ne, cap), jnp.int32),
                   jax.ShapeDtypeStruct((bt, ne, cap, 1), F32)],
        compiler_params=_cparams(("parallel",)),
        name="expert_select",
    )(aff_t.reshape(bt, ne, nb, LANES))


def _ffn_kernel(idx_ref, idxn_ref, h_hbm, g_ref, gate2_ref, wg_ref, wu_ref, wd_ref, o_ref,
                xs_ref, wgb_ref, wub_ref, wdb_ref, sem, *, cap, chunk, gs, d):
    e = pl.program_id(0)
    bb = pl.program_id(1)
    nbb = pl.num_programs(1)
    step = e * nbb + bb
    slot = step % 2
    rows = gs * cap
    tile = d // LANES

    def row_copy(ids_ref, b0, dst_slot, s, j):
        tok = pl.multiple_of(ids_ref[s, 0, 0, j] * tile, tile)
        return pltpu.make_async_copy(h_hbm.at[b0 + s, pl.ds(tok, tile), :],
                                     xs_ref.at[dst_slot, pl.ds((s * cap + j) * tile, tile), :], sem.at[dst_slot])

    def wait_slot(s):
        pltpu.make_async_copy(h_hbm.at[0, pl.ds(0, rows * tile), :], xs_ref.at[s], sem.at[s]).wait()

    @pl.when(step == 0)
    def _():
        for s in range(gs):
            def start(j, carry):
                row_copy(idx_ref, bb * gs, slot, s, j).start()
                return carry
            lax.fori_loop(0, cap, start, 0, unroll=8)

    @pl.when(bb == 0)
    def _():
        wgb_ref[...] = wg_ref[0, 0].astype(BF16)
        wub_ref[...] = wu_ref[0, 0].astype(BF16)
        wdb_ref[...] = wd_ref[0, 0].astype(BF16)

    wait_slot(slot)

    bn = ((bb + 1) % nbb) * gs
    for c in range(rows // chunk):
        pieces = [xs_ref[slot, pl.ds(c * chunk * tile + k, chunk, stride=tile), :].astype(BF16) for k in range(tile)]
        xb = jnp.concatenate(pieces, axis=-1)
        gate = _dot(xb, wgb_ref[...])
        up = _dot(xb, wub_ref[...])
        for r in range(c * chunk, (c + 1) * chunk):
            row_copy(idxn_ref, bn, 1 - slot, r // cap, r % cap).start()
        hid = (gate * jax.nn.sigmoid(gate)) * up
        y = _dot(hid.astype(BF16), wdb_ref[...])
        s, r0 = (c * chunk) // cap, (c * chunk) % cap
        o_ref[s, 0, r0:r0 + chunk, :] = (y * g_ref[s, 0, r0:r0 + chunk, :]) * gate2_ref[s]

    @pl.when(step == pl.num_programs(0) * nbb - 1)
    def _():
        wait_slot(1 - slot)


def _ffn_call(idx, h2t, g, gate2, wg, wu, wd, layer, *, gs):
    bt, ne, cap = idx.shape
    d = gate2.shape[2]
    ff = wg.shape[3]
    nbb = bt // gs
    chunk = min(cap, 256)
    wspec = lambda s: pl.BlockSpec((1, 1) + s, lambda e, b: (layer, e, 0, 0))
    idx4 = idx.reshape(bt, ne, 1, cap)
    nxt = lambda e, b: ((b + 1) % nbb, jnp.minimum(e + (b + 1) // nbb, ne - 1), 0, 0)
    return pl.pallas_call(
        functools.partial(_ffn_kernel, cap=cap, chunk=chunk, gs=gs, d=d),
        grid=(ne, nbb),
        in_specs=[pl.BlockSpec((gs, 1, 1, cap), lambda e, b: (b, e, 0, 0), memory_space=pltpu.SMEM),
                  pl.BlockSpec((gs, 1, 1, cap), nxt, memory_space=pltpu.SMEM),
                  pl.BlockSpec(memory_space=pl.ANY),
                  pl.BlockSpec((gs, 1, cap, 1), lambda e, b: (b, e, 0, 0)),
                  pl.BlockSpec((gs, 1, d), lambda e, b: (b, 0, 0)),
                  wspec((d, ff)), wspec((d, ff)), wspec((ff, d))],
        out_specs=pl.BlockSpec((gs, 1, cap, d), lambda e, b: (b, e, 0, 0)),
        out_shape=jax.ShapeDtypeStruct((bt, ne, cap, d), F32),
        scratch_shapes=[pltpu.VMEM((2, gs * cap * d // LANES, LANES), F32), pltpu.VMEM((d, ff), BF16),
                        pltpu.VMEM((d, ff), BF16), pltpu.VMEM((ff, d), BF16), pltpu.SemaphoreType.DMA((2,))],
        compiler_params=_cparams(("arbitrary", "arbitrary")),
        name="expert_ffn",
    )(idx4, idx4, h2t, g, gate2, wg, wu, wd)


def _combine_kernel(idx_ref, x_hbm, z_ref, gfin_ref, o_hbm, acc_ref, sem_in, sem_out, *, cap, final_norm, nchunk):
    b = pl.program_id(0)
    e = pl.program_id(1)
    rc = acc_ref.shape[0] // nchunk

    def in_copy(bi, c):
        rs = pl.ds(c * rc, rc)
        return pltpu.make_async_copy(x_hbm.at[bi, rs, :], acc_ref.at[rs, :], sem_in.at[c])

    def out_copy(c):
        rs = pl.ds(c * rc, rc)
        return pltpu.make_async_copy(acc_ref.at[rs, :], o_hbm.at[b, rs, :], sem_out.at[c])

    @pl.when((e == 0) & (b == 0))
    def _():
        for c in range(nchunk):
            in_copy(b, c).start()

    @pl.when(e == 0)
    def _():
        for c in range(nchunk):
            in_copy(b, c).wait()

    group = 4
    for j0 in range(0, cap, group):
        toks = [idx_ref[0, 0, 0, j0 + u] for u in range(group)]
        vals = [acc_ref[pl.ds(toks[u], 1), :] + z_ref[0, 0, j0 + u:j0 + u + 1, :] for u in range(group)]
        for u in range(group):
            acc_ref[pl.ds(toks[u], 1), :] = vals[u]

    @pl.when(e == pl.num_programs(1) - 1)
    def _():
        for c in range(nchunk):
            if final_norm:
                rs = pl.ds(c * rc, rc)
                acc_ref[rs, :] = _rms(acc_ref[rs, :], gfin_ref[...])
            out_copy(c).start()
        for c in range(nchunk):
            out_copy(c).wait()

            @pl.when(b + 1 < pl.num_programs(0))
            def _():
                in_copy(b + 1, c).start()


def _combine_call(idx, x, z, g_final, *, final_norm):
    bt, ne, cap = idx.shape
    _, n, d = x.shape
    nchunk = 32
    return pl.pallas_call(
        functools.partial(_combine_kernel, cap=cap, final_norm=final_norm, nchunk=nchunk),
        grid=(bt, ne),
        in_specs=[pl.BlockSpec((1, 1, 1, cap), lambda b, e: (b, e, 0, 0), memory_space=pltpu.SMEM),
                  pl.BlockSpec(memory_space=pl.ANY),
                  pl.BlockSpec((1, 1, cap, d), lambda b, e: (b, e, 0, 0)),
                  pl.BlockSpec((1, d), lambda b, e: (0, 0))],
        out_specs=pl.BlockSpec(memory_space=pl.ANY),
        out_shape=jax.ShapeDtypeStruct((bt, n, d), F32),
        scratch_shapes=[pltpu.VMEM((n, d), F32), pltpu.SemaphoreType.DMA((nchunk,)),
                        pltpu.SemaphoreType.DMA((nchunk,))],
        compiler_params=_cparams(("arbitrary", "arbitrary")),
        name="expert_combine",
    )(idx.reshape(bt, ne, 1, cap), x, z, g_final)


def _block_diag(w):
    h, hd, _ = w.shape
    eye = jnp.eye(h, dtype=w.dtype)
    return (w[:, :, None, :] * eye[:, None, :, None]).reshape(h * hd, h * hd)


def _gate_weights(wr, br, wi, bi):
    dense = [_block_diag(wr[0]), _block_diag(wi[0]), _block_diag(wr[1]), _block_diag(wi[1])]
    bias = [br[0], bi[0], br[1], bi[1]]
    lru_w = dense[0].shape[0]
    wg, bg = [], []
    for g in range(lru_w // GATE_GROUP):
        cs = slice(g * GATE_GROUP, (g + 1) * GATE_GROUP)
        wg.append(jnp.concatenate([m[cs, cs] for m in dense], axis=1))
        bg.append(jnp.concatenate([v[cs] for v in bias])[None, :])
    return jnp.stack(wg).astype(BF16), jnp.stack(bg)


def _channel_dft(fw):
    gd = fw // FFT_GROUPS
    c, s = _cos_sin(gd)
    eye = np.eye(FFT_GROUPS)
    return _split_const(np.concatenate([np.kron(eye, c), np.kron(eye, s)], axis=1))


def _mixer(x, lp, shift, scale, h0_f, h0_b, *, row_w, tm, tb, need_out):
    af, bf, ab, bb, lg, ysc, ucs = _inproj_call(
        x, shift, scale, lp["g1"], lp["w_in"], lp["layer"], lp["cw"], lp["cb"], lp["wg"], lp["bg"], lp["lam"],
        lp["scw"], lp["g_sc"], lp["dftc"], row_w=row_w, tm=tm)
    hf, sf = _scan_call(af, bf, h0_f, None, reverse=False, tb=tb)
    hs, sb = _scan_call(ab, bb, h0_b, hf, reverse=True, tb=tb)
    if not need_out:
        return sf, sb, None
    yfft = _fourier_call(ucs, lp["g_fft"])
    return sf, sb, (hs, lg, ysc, yfft)


def _moe(x, h2, aff_t, gate2, lp, g_final, *, final_norm, gs):
    bt, n, d = x.shape
    cap = CAPACITY_FACTOR * n // N_EXPERTS
    pad = (-n) % (SUBLANES * LANES)
    if pad:
        aff_t = jnp.pad(aff_t, ((0, 0), (0, 0), (0, pad)), constant_values=-1.0)
    idx, g = _select_call(aff_t, cap)
    z = _ffn_call(idx, h2, g, gate2, lp["wge"], lp["wue"], lp["wde"], lp["layer"], gs=gs)
    return _combine_call(idx, x, z, g_final, final_norm=final_norm)


def kernel(x, c, ctx, c_ctx, w_ada, b_ada, g_norm1, w_in, lru_conv_w, lru_conv_b, lru_wr, lru_br, lru_wi,
           lru_bi, lru_lam, sc_conv_w, g_out, w_out, g_norm2, w_router, w_gate_e, w_up_e, w_down_e, g_final):
    depth = w_ada.shape[0]
    bsz, seq, d = x.shape
    ctx_len = ctx.shape[1]
    lru_w = lru_conv_w.shape[2]
    conv_w = sc_conv_w.shape[2]
    fft_w = w_in.shape[2] - 2 * lru_w - 3 * conv_w

    cs = jnp.concatenate([c, c_ctx[None, :], jnp.zeros((SUBLANES - bsz - 1, d), F32)], axis=0)
    mods = _ada_call(cs, w_ada, b_ada)
    dftc = _channel_dft(fft_w)
    gfin = g_final[None, :]
    zero_state = jnp.zeros((bsz, 1, lru_w), F32)

    for l in range(depth):
        last = l == depth - 1
        mx = [mods[l, :bsz, None, k * d:(k + 1) * d] for k in range(6)]
        mc = [jnp.broadcast_to(mods[l, bsz, k * d:(k + 1) * d], (bsz, 1, d)) for k in range(6)]
        wg, bg = _gate_weights(lru_wr[l], lru_br[l], lru_wi[l], lru_bi[l])
        lp = dict(
            g1=g_norm1[l][None, :], w_in=w_in, cw=lru_conv_w[l], cb=lru_conv_b[l][None, :],
            wg=wg, bg=bg, lam=lru_lam[l].reshape(1, 2 * lru_w), scw=sc_conv_w[l],
            g_sc=g_out[l][None, lru_w:lru_w + conv_w], g_fft=g_out[l][None, lru_w + conv_w:], dftc=dftc,
            wge=w_gate_e, wue=w_up_e, wde=w_down_e, layer=l)
        g_lru = g_out[l][None, :lru_w]
        w_router_t = _split_const(w_router[l].T)

        sf, sb, parts = _mixer(ctx, lp, mc[0], mc[1], zero_state, zero_state,
                               row_w=ctx_len, tm=ctx_len, tb=ctx_len, need_out=not last)
        if not last:
            ctx, hc2, aff_c = _outproj_call(*parts, ctx, w_out, l, g_lru, mc[2], mc[3], mc[4],
                                            g_norm2[l][None, :], w_router_t, tm=ctx_len)
            ctx = _moe(ctx, hc2, aff_c, mc[5], lp, gfin, final_norm=False, gs=bsz)

        _, _, parts = _mixer(x, lp, mx[0], mx[1], sf, sb, row_w=GRID_W, tm=512, tb=min(seq, 2048), need_out=True)
        x, hx2, aff_x = _outproj_call(*parts, x, w_out, l, g_lru, mx[2], mx[3], mx[4],
                                      g_norm2[l][None, :], w_router_t, tm=512)
        x = _moe(x, hx2, aff_x, mx[5], lp, gfin, final_norm=last, gs=1)
    return x
```

```python
import functools
import math

import numpy as np
import jax
import jax.numpy as jnp
from jax import lax
from jax.experimental import pallas as pl
from jax.experimental.pallas import tpu as pltpu

F32 = jnp.float32
BF16 = jnp.bfloat16
HI = lax.Precision.HIGHEST

GRID_W = 64
LRU_HEADS = 8
LRU_C = 8.0
N_EXPERTS = 16
CAPACITY_FACTOR = 2
EPS = 1e-6
FFT_GROUPS = 4

LANES = 128
SUBLANES = 8
GATE_GROUP = 256
SUB_ROWS = 512
VMEM_LIMIT = 56 * 1024 * 1024


def _cparams(sem):
    return pltpu.CompilerParams(dimension_semantics=sem, vmem_limit_bytes=VMEM_LIMIT)


def _rms(x, g):
    return x * lax.rsqrt(jnp.mean(x * x, axis=-1, keepdims=True) + EPS) * g


def _dot(a, b):
    return jnp.dot(a, b, preferred_element_type=F32)


def _dot_hi(a, b):
    return jnp.dot(a, b, preferred_element_type=F32, precision=HI)


def _split_const(m):
    m = jnp.asarray(m, F32)
    hi = m.astype(BF16)
    return jnp.stack([hi, (m - hi.astype(F32)).astype(BF16)])


def _split(x):
    hi = x.astype(BF16)
    return hi, (x - hi.astype(F32)).astype(BF16)


def _dot3_const_lhs(m_ref, x):
    x_hi, x_lo = _split(x)
    return _dot(m_ref[0], x_hi) + (_dot(m_ref[1], x_hi) + _dot(m_ref[0], x_lo))


def _dot3_const_rhs(x, m_ref):
    x_hi, x_lo = _split(x)
    return _dot(x_hi, m_ref[0]) + (_dot(x_hi, m_ref[1]) + _dot(x_lo, m_ref[0]))


def _dot_nt(a, b, precision=None):
    return lax.dot_general(a, b, (((1,), (1,)), ((), ())), preferred_element_type=F32, precision=precision)


def _ada_kernel(c_ref, w_ref, b_ref, o_ref):
    c = c_ref[...]
    o_ref[...] = _dot_hi(c * jax.nn.sigmoid(c), w_ref[...]) + b_ref[...]


def _ada_call(cs, w_ada, b_ada):
    depth, d, six_d = w_ada.shape
    nblk = six_d // d
    return pl.pallas_call(
        _ada_kernel,
        grid=(depth, nblk),
        in_specs=[
            pl.BlockSpec((SUBLANES, d), lambda l, j: (0, 0)),
            pl.BlockSpec((None, d, d), lambda l, j: (l, 0, j)),
            pl.BlockSpec((None, 1, d), lambda l, j: (l, 0, j)),
        ],
        out_specs=pl.BlockSpec((None, SUBLANES, d), lambda l, j: (l, 0, j)),
        out_shape=jax.ShapeDtypeStruct((depth, SUBLANES, six_d), F32),
        compiler_params=_cparams(("parallel", "parallel")),
        name="ada_mod",
    )(cs, w_ada, b_ada.reshape(depth, 1, six_d))


def _shift_rows(u, d, pos, row_w):
    n = u.shape[0]
    if d == 0:
        return u
    rolled = pltpu.roll(u, (-d) % n, axis=0)
    valid = (pos + d >= 0) & (pos + d < row_w)
    return jnp.where(valid, rolled, 0.0)


def _conv_rows(u, w_ref, left, pos, row_w):
    out = None
    for k in range(w_ref.shape[0]):
        term = w_ref[k:k + 1, :] * _shift_rows(u, k - left, pos, row_w)
        out = term if out is None else out + term
    return out


def _inproj_kernel(x_ref, sh_ref, sc_ref, g1_ref, win_ref, cw_ref, cb_ref, wg_ref, bg_ref, lam_ref,
                   scw_ref, gsc_ref, dft_ref,
                   af_ref, bf_ref, ab_ref, bb_ref, lg_ref, ysc_ref, ucs_ref, winb_ref, *, row_w, lru_w, conv_w, sub):
    @pl.when((pl.program_id(0) == 0) & (pl.program_id(1) == 0))
    def _():
        winb_ref[...] = win_ref[...].astype(BF16)

    tm = x_ref.shape[1]
    gmod = g1_ref[...] * (1.0 + sc_ref[0])
    shift = sh_ref[0]
    nl = -lam_ref[...]
    nsp = -LRU_C * (jnp.maximum(nl, 0.0) + jnp.log(1.0 + jnp.exp(-jnp.abs(nl))))
    pos = lax.broadcasted_iota(jnp.int32, (sub, 1), 0) % row_w
    out_refs = ((af_ref, bf_ref), (ab_ref, bb_ref))
    o = 2 * lru_w

    for t in range(tm // sub):
        rs = slice(t * sub, (t + 1) * sub)
        x = x_ref[0, rs, :]
        h = x * lax.rsqrt(jnp.mean(x * x, axis=-1, keepdims=True) + EPS) * gmod + shift
        p = _dot(h.astype(BF16), winb_ref[...])

        u = _conv_rows(p[:, :lru_w], cw_ref, 1, pos, row_w) + cb_ref[...]
        for g in range(lru_w // GATE_GROUP):
            cs = slice(g * GATE_GROUP, (g + 1) * GATE_GROUP)
            ug = u[:, cs]
            z = _dot(ug.astype(BF16), wg_ref[g]) + bg_ref[g]
            for d in range(2):
                r = jax.nn.sigmoid(z[:, (2 * d) * GATE_GROUP:(2 * d + 1) * GATE_GROUP])
                i = jax.nn.sigmoid(z[:, (2 * d + 1) * GATE_GROUP:(2 * d + 2) * GATE_GROUP])
                a = jnp.exp(r * nsp[:, d * lru_w + g * GATE_GROUP: d * lru_w + (g + 1) * GATE_GROUP])
                inp = jnp.sqrt(1.0 - a * a) * (i * ug)
                for k in range(GATE_GROUP // LANES):
                    plane = g * (GATE_GROUP // LANES) + k
                    out_refs[d][0][0, plane, rs, :] = a[:, k * LANES:(k + 1) * LANES]
                    out_refs[d][1][0, plane, rs, :] = inp[:, k * LANES:(k + 1) * LANES]

        lg_ref[0, rs, :] = jax.nn.gelu(p[:, lru_w:o], approximate=True)

        sc_b = p[:, o:o + conv_w]
        sc_c = p[:, o + conv_w:o + 2 * conv_w]
        sc_x = p[:, o + 2 * conv_w:o + 3 * conv_w]
        y_sc = sc_b * _conv_rows(sc_c * sc_x, scw_ref, 1, pos, row_w)
        ysc_ref[0, rs, :] = _rms(y_sc, gsc_ref[...])

        ucs_ref[0, rs, :] = _dot3_const_rhs(p[:, o + 3 * conv_w:], dft_ref)


def _inproj_call(x, shift, scale, g1, w_in, layer, cw, cb, wg, bg, lam, scw, gsc, dftc, *, row_w, tm):
    bt, n, d = x.shape
    in_cols = w_in.shape[2]
    lru_w = cw.shape[1]
    conv_w = scw.shape[1]
    fft_w = in_cols - 2 * lru_w - 3 * conv_w
    ng = lru_w // GATE_GROUP
    tok = lambda w: pl.BlockSpec((1, tm, w), lambda b, i: (b, i, 0))
    vec = lambda w: pl.BlockSpec((1, 1, w), lambda b, i: (b, 0, 0))
    full = lambda *s: pl.BlockSpec(s, lambda b, i: (0,) * len(s))
    shp = lambda w: jax.ShapeDtypeStruct((bt, n, w), F32)
    planes = pl.BlockSpec((1, lru_w // LANES, tm, LANES), lambda b, i: (b, 0, i, 0))
    return pl.pallas_call(
        functools.partial(_inproj_kernel, row_w=row_w, lru_w=lru_w, conv_w=conv_w, sub=min(tm, SUB_ROWS)),
        grid=(bt, n // tm),
        in_specs=[tok(d), vec(d), vec(d), full(1, d),
                  pl.BlockSpec((None, d, in_cols), lambda b, i: (layer, 0, 0), pipeline_mode=pl.Buffered(1)),
                  full(*cw.shape), full(1, lru_w),
                  pl.BlockSpec((None, ng, GATE_GROUP, 4 * GATE_GROUP), lambda b, i: (layer, 0, 0, 0)),
                  pl.BlockSpec((None, ng, 1, 4 * GATE_GROUP), lambda b, i: (layer, 0, 0, 0)), full(1, 2 * lru_w),
                  full(*scw.shape), full(1, conv_w), full(2, fft_w, 2 * fft_w)],
        out_specs=[planes] * 4 + [tok(lru_w), tok(conv_w), tok(2 * fft_w)],
        out_shape=[jax.ShapeDtypeStruct((bt, lru_w // LANES, n, LANES), F32)] * 4
        + [shp(lru_w), shp(conv_w), shp(2 * fft_w)],
        scratch_shapes=[pltpu.VMEM((d, in_cols), BF16)],
        compiler_params=_cparams(("arbitrary", "arbitrary")),
        name="inproj_local",
    )(x, shift, scale, g1, w_in, cw, cb, wg, bg, lam, scw, gsc, dftc)


def _scan_kernel(*refs, reverse, add_other):
    if add_other:
        a_ref, b_ref, h0_ref, other_ref, h_ref, hl_ref, carry_ref = refs
    else:
        a_ref, b_ref, h0_ref, h_ref, hl_ref, carry_ref = refs
        other_ref = None

    @pl.when(pl.program_id(2) == 0)
    def _():
        carry_ref[...] = jnp.broadcast_to(h0_ref[0], carry_ref.shape)

    tb = a_ref.shape[2]
    group = SUBLANES * SUBLANES
    row = lax.broadcasted_iota(jnp.int32, (SUBLANES, LANES), 0)
    steps = range(SUBLANES - 1, -1, -1) if reverse else range(SUBLANES)
    first = SUBLANES - 1 if reverse else 0
    last = 0 if reverse else SUBLANES - 1
    groups = range(tb // group - 1, -1, -1) if reverse else range(tb // group)

    def shift_chunks(v, k):
        return pltpu.roll(v, (SUBLANES - k) if reverse else k, axis=0)

    carry = carry_ref[...]
    for g in groups:
        rows = [pl.ds(g * group + s, SUBLANES, stride=SUBLANES) for s in range(SUBLANES)]
        hs, ps = {}, {}
        h = p = None
        for s in steps:
            a = a_ref[0, 0, rows[s], :]
            b = b_ref[0, 0, rows[s], :]
            h = b if h is None else a * h + b
            p = a if p is None else p * a
            hs[s], ps[s] = h, p
        pe, he = p, h
        for k in (1, 2, 4):
            valid = (row <= SUBLANES - 1 - k) if reverse else (row >= k)
            he = jnp.where(valid, he + pe * shift_chunks(he, k), he)
            pe = jnp.where(valid, pe * shift_chunks(pe, k), pe)
        end = he + pe * carry
        h_in = jnp.where(row == first, carry, shift_chunks(end, 1))
        carry = jnp.broadcast_to(end[last:last + 1], end.shape)
        for s in range(SUBLANES):
            out = hs[s] + ps[s] * h_in
            if other_ref is not None:
                out = out + other_ref[0, 0, rows[s], :]
            h_ref[0, 0, rows[s], :] = out
    carry_ref[...] = carry
    hl_ref[0] = carry[0:1]


def _scan_call(a, b, h0, other, *, reverse, tb):
    bt, npl, n, _ = a.shape
    nt = n // tb
    blk = (lambda bi, l, i: (bi, l, nt - 1 - i, 0)) if reverse else (lambda bi, l, i: (bi, l, i, 0))
    tok = pl.BlockSpec((1, 1, tb, LANES), blk)
    st = pl.BlockSpec((1, 1, LANES), lambda bi, l, i: (bi, 0, l))
    ins = [a, b, h0] + ([other] if other is not None else [])
    in_specs = [tok, tok, st] + ([tok] if other is not None else [])
    return pl.pallas_call(
        functools.partial(_scan_kernel, reverse=reverse, add_other=other is not None),
        grid=(bt, npl, nt),
        in_specs=in_specs,
        out_specs=[tok, st],
        out_shape=[jax.ShapeDtypeStruct(a.shape, F32), jax.ShapeDtypeStruct((bt, 1, npl * LANES), F32)],
        scratch_shapes=[pltpu.VMEM((SUBLANES, LANES), F32)],
        compiler_params=_cparams(("parallel", "parallel", "arbitrary")),
        name="lru_scan_bwd" if reverse else "lru_scan_fwd",
    )(*ins)


def _cos_sin(n):
    k = np.arange(n, dtype=np.float64)
    ang = 2.0 * np.pi * np.outer(k, k) / n
    return np.cos(ang), np.sin(ang)


def _slab_copies(src_hbm, dst_ref, sem, b, first, count, width, slot):
    return [pltpu.make_async_copy(src_hbm.at[b, :, first + j, :], dst_ref.at[slot, :, pl.ds(j * width, width)],
                                  sem.at[slot]) for j in range(count)]


def _prefetch_slabs(src_hbm, dst_ref, sem, count, width):
    b, i = pl.program_id(0), pl.program_id(1)
    ni = pl.num_programs(1)
    step = b * ni + i
    slot = step % 2

    @pl.when(step == 0)
    def _():
        for cp in _slab_copies(src_hbm, dst_ref, sem, b, i * count, count, width, slot):
            cp.start()

    @pl.when(step + 1 < pl.num_programs(0) * ni)
    def _():
        wrap = i + 1 == ni
        bn = jnp.where(wrap, b + 1, b)
        nxt = jnp.where(wrap, 0, i + 1)
        for cp in _slab_copies(src_hbm, dst_ref, sem, bn, nxt * count, count, width, 1 - slot):
            cp.start()

    for cp in _slab_copies(src_hbm, dst_ref, sem, b, i * count, count, width, slot):
        cp.wait()
    return step, slot


def _dft1_kernel(x_hbm, m_ref, tc_ref, ts_ref, o_ref, xs_ref, sem, *, n2, jn, fw):
    _, slot = _prefetch_slabs(x_hbm, xs_ref, sem, jn, 2 * fw)
    r = _dot(m_ref[0], xs_ref[slot].astype(BF16))
    for j in range(jn):
        base = j * 2 * fw
        c_uc = r[:n2, base:base + fw]
        c_us = r[:n2, base + fw:base + 2 * fw]
        s_uc = r[n2:, base:base + fw]
        s_us = r[n2:, base + fw:base + 2 * fw]
        br = c_uc - s_us
        bi = -(c_us + s_uc)
        tc = tc_ref[0, :, j:j + 1]
        ts = ts_ref[0, :, j:j + 1]
        o_ref[0, j, :, :fw] = br * tc + bi * ts
        o_ref[0, j, :, fw:] = bi * tc - br * ts


def _dft2_kernel(p_hbm, m_ref, g_ref, y_hbm, ps_ref, ys_ref, sem_in, sem_out, *, n1, kn, fw):
    step, slot = _prefetch_slabs(p_hbm, ps_ref, sem_in, kn, 2 * fw)
    b, i = pl.program_id(0), pl.program_id(1)
    last = pl.num_programs(0) * pl.num_programs(1) - 1

    def out_copies(s):
        return [pltpu.make_async_copy(ys_ref.at[s, :, pl.ds(k * fw, fw)], y_hbm.at[b, :, i * kn + k, :],
                                      sem_out.at[s]) for k in range(kn)]

    r = _dot(m_ref[0], ps_ref[slot].astype(BF16))

    @pl.when(step >= 2)
    def _():
        for cp in out_copies(slot):
            cp.wait()

    for k in range(kn):
        base = k * 2 * fw
        y = r[:n1, base:base + fw] + r[n1:, base + fw:base + 2 * fw]
        ys_ref[slot, :, k * fw:(k + 1) * fw] = _rms(y, g_ref[...])
    for cp in out_copies(slot):
        cp.start()

    @pl.when(step == last)
    def _():
        for cp in out_copies(slot):
            cp.wait()

    @pl.when((step == last) & (step >= 1))
    def _():
        for cp in out_copies(1 - slot):
            cp.wait()


def _dft_direct_kernel(x_ref, m_ref, g_ref, o_ref, *, n, fw):
    x = x_ref[0]
    r = _dot3_const_lhs(m_ref, x)
    y = r[:n, :fw] - r[n:, fw:]
    o_ref[0] = _rms(y, g_ref[...])


def _fourier_call(ucs, g_fft):
    bt, n, fw2 = ucs.shape
    fw = fw2 // 2
    gd = fw // FFT_GROUPS
    scale = 1.0 / math.sqrt(n * gd)
    if n <= 512:
        c, s = _cos_sin(n)
        m = _split_const(np.concatenate([c, s], 0) * scale)
        return pl.pallas_call(
            functools.partial(_dft_direct_kernel, n=n, fw=fw),
            grid=(bt,),
            in_specs=[pl.BlockSpec((1, n, fw2), lambda b: (b, 0, 0)),
                      pl.BlockSpec((2, 2 * n, n), lambda b: (0, 0, 0)),
                      pl.BlockSpec((1, fw), lambda b: (0, 0))],
            out_specs=pl.BlockSpec((1, n, fw), lambda b: (b, 0, 0)),
            out_shape=jax.ShapeDtypeStruct((bt, n, fw), F32),
            compiler_params=_cparams(("parallel",)),
            name="dft_direct",
        )(ucs, m, g_fft)

    n1 = LANES
    n2 = n // n1
    jn = 16
    c2, s2 = _cos_sin(n2)
    m1 = _split_const(np.concatenate([c2, s2], 0))
    ang = 2.0 * np.pi * np.outer(np.arange(n2), np.arange(n1)) / n
    tw = lambda f: jnp.asarray(f(ang).reshape(n2, n1 // jn, jn).transpose(1, 0, 2), F32)
    p = pl.pallas_call(
        functools.partial(_dft1_kernel, n2=n2, jn=jn, fw=fw),
        grid=(bt, n1 // jn),
        in_specs=[pl.BlockSpec(memory_space=pl.ANY),
                  pl.BlockSpec((2, 2 * n2, n2), lambda b, i: (0, 0, 0)),
                  pl.BlockSpec((1, n2, jn), lambda b, i: (i, 0, 0)),
                  pl.BlockSpec((1, n2, jn), lambda b, i: (i, 0, 0))],
        out_specs=pl.BlockSpec((1, jn, n2, fw2), lambda b, i: (b, i, 0, 0)),
        out_shape=jax.ShapeDtypeStruct((bt, n1, n2, fw2), F32),
        scratch_shapes=[pltpu.VMEM((2, n2, jn * fw2), F32), pltpu.SemaphoreType.DMA((2,))],
        compiler_params=_cparams(("arbitrary", "arbitrary")),
        name="dft_stage1",
    )(ucs.reshape(bt, n2, n1, fw2), m1, tw(np.cos), tw(np.sin))

    kn = min(8, n2)
    c1, s1 = _cos_sin(n1)
    m2 = _split_const(np.concatenate([c1, s1], 0) * scale)
    y = pl.pallas_call(
        functools.partial(_dft2_kernel, n1=n1, kn=kn, fw=fw),
        grid=(bt, n2 // kn),
        in_specs=[pl.BlockSpec(memory_space=pl.ANY),
                  pl.BlockSpec((2, 2 * n1, n1), lambda b, i: (0, 0, 0)),
                  pl.BlockSpec((1, fw), lambda b, i: (0, 0))],
        out_specs=pl.BlockSpec(memory_space=pl.ANY),
        out_shape=jax.ShapeDtypeStruct((bt, n1, n2, fw), F32),
        scratch_shapes=[pltpu.VMEM((2, n1, kn * fw2), F32), pltpu.VMEM((2, n1, kn * fw), F32),
                        pltpu.SemaphoreType.DMA((2,)), pltpu.SemaphoreType.DMA((2,))],
        compiler_params=_cparams(("arbitrary", "arbitrary")),
        name="dft_stage2",
    )(p, m2, g_fft)
    return y.reshape(bt, n, fw)


def _outproj_kernel(hs_ref, lg_ref, ysc_ref, yfft_ref, x_ref, wout_ref, glru_ref, gate_ref,
                    sh_ref, sc_ref, g2_ref, wr_ref, xn_ref, h2_ref, aff_ref, woutb_ref, *, sub):
    @pl.when((pl.program_id(0) == 0) & (pl.program_id(1) == 0))
    def _():
        woutb_ref[...] = wout_ref[...].astype(BF16)

    tm, d = x_ref.shape[1], x_ref.shape[2]
    tile = d // LANES
    gmod = g2_ref[...] * (1.0 + sc_ref[0])
    for t in range(tm // sub):
        rs = slice(t * sub, (t + 1) * sub)
        hs = jnp.concatenate([hs_ref[0, k, rs, :] for k in range(hs_ref.shape[1])], axis=-1)
        y_lru = _rms(hs * lg_ref[0, rs, :], glru_ref[...])
        y = jnp.concatenate([y_lru.astype(BF16), ysc_ref[0, rs, :].astype(BF16), yfft_ref[0, rs, :].astype(BF16)],
                            axis=-1)
        xn = x_ref[0, rs, :] + gate_ref[0] * _dot(y, woutb_ref[...])
        xn_ref[0, rs, :] = xn
        h2 = xn * lax.rsqrt(jnp.mean(xn * xn, axis=-1, keepdims=True) + EPS) * gmod + sh_ref[0]
        for k in range(tile):
            h2_ref[0, pl.ds(t * sub * tile + k, sub, stride=tile), :] = h2[:, k * LANES:(k + 1) * LANES]
        h_hi, h_lo = _split(h2)
        logits = _dot_nt(wr_ref[0], h_hi) + (_dot_nt(wr_ref[1], h_hi) + _dot_nt(wr_ref[0], h_lo))
        m = jnp.max(logits, axis=0, keepdims=True)
        e = jnp.exp(logits - m)
        aff_ref[0, :, rs] = e / jnp.sum(e, axis=0, keepdims=True)


def _outproj_call(hs, lg, ysc, yfft, x, w_out, layer, g_lru, gate, shift, scale, g2, w_router_t, *, tm):
    bt, n, d = x.shape
    ne = w_router_t.shape[2]
    tok = lambda w: pl.BlockSpec((1, tm, w), lambda b, i: (b, i, 0))
    vec = lambda w: pl.BlockSpec((1, 1, w), lambda b, i: (b, 0, 0))
    full = lambda *s: pl.BlockSpec(s, lambda b, i: (0,) * len(s))
    return pl.pallas_call(
        functools.partial(_outproj_kernel, sub=min(tm, SUB_ROWS)),
        grid=(bt, n // tm),
        in_specs=[pl.BlockSpec((1, hs.shape[1], tm, LANES), lambda b, i: (b, 0, i, 0)),
                  tok(lg.shape[2]), tok(ysc.shape[2]), tok(yfft.shape[2]), tok(d),
                  pl.BlockSpec((None,) + w_out.shape[1:], lambda b, i: (layer, 0, 0), pipeline_mode=pl.Buffered(1)),
                  full(1, lg.shape[2]), vec(d), vec(d), vec(d), full(1, d),
                  pl.BlockSpec((2, None, ne, d), lambda b, i: (0, layer, 0, 0))],
        out_specs=[tok(d), pl.BlockSpec((1, tm * d // LANES, LANES), lambda b, i: (b, i, 0)),
                   pl.BlockSpec((1, ne, tm), lambda b, i: (b, 0, i))],
        out_shape=[jax.ShapeDtypeStruct((bt, n, d), F32), jax.ShapeDtypeStruct((bt, n * d // LANES, LANES), F32),
                   jax.ShapeDtypeStruct((bt, ne, n), F32)],
        scratch_shapes=[pltpu.VMEM(w_out.shape[1:], BF16)],
        compiler_params=_cparams(("arbitrary", "arbitrary")),
        name="outproj_router",
    )(hs, lg, ysc, yfft, x, w_out, g_lru, gate, shift, scale, g2, w_router_t)


def _select_kernel(aff_ref, idx_ref, g_ref, *, cap):
    aff = aff_ref[0]
    ne, nb, _ = aff.shape

    def count(mask):
        c = jnp.sum(mask.astype(jnp.int32), axis=1, keepdims=True)
        return jnp.sum(c, axis=2, keepdims=True)

    def bit_step(i, t):
        cand = t | (jnp.int32(1) << (30 - i))
        return jnp.where(count(aff >= lax.bitcast_convert_type(cand, F32)) >= cap, cand, t)

    bits = lax.fori_loop(0, 31, bit_step, jnp.zeros((ne, 1, 1), jnp.int32))
    lo = lax.bitcast_convert_type(bits, F32)
    hi = lax.bitcast_convert_type(bits + 1, F32)

    def mid_step(i, lh):
        lo, hi = lh
        mid = (lo + hi) * 0.5
        ok = count(aff >= mid) >= cap
        return jnp.where(ok, mid, lo), jnp.where(ok, hi, mid)

    lo, hi = lax.fori_loop(0, 14, mid_step, (lo, hi))
    gt = aff >= hi
    eq = (aff >= lo) & (aff < hi)
    need = (cap - count(gt)).astype(F32)

    lane = lax.broadcasted_iota(jnp.int32, (LANES, LANES), 0)
    lane_t = lax.broadcasted_iota(jnp.int32, (LANES, LANES), 1)
    tri_incl = (lane <= lane_t).astype(BF16)
    blk = lax.broadcasted_iota(jnp.int32, (nb, nb), 0)
    blk_t = lax.broadcasted_iota(jnp.int32, (nb, nb), 1)
    tri_blk = (blk <= blk_t).astype(BF16)
    ones_row = jnp.ones((SUBLANES, LANES), BF16)
    kvals = lax.broadcasted_iota(jnp.int32, (SUBLANES, nb), 1).astype(BF16)
    slot = lax.broadcasted_iota(jnp.int32, (cap, 1), 0).astype(F32)

    def block_prefix(mask_bf16):
        lc = _dot(mask_bf16, tri_incl)
        cnt_row = _dot_nt(ones_row, mask_bf16)
        inc_row = _dot(cnt_row.astype(BF16), tri_blk)
        return lc, inc_row - cnt_row, inc_row

    for e in range(ne):
        eq_e = eq[e].astype(BF16)
        lc_eq, off_eq, _ = block_prefix(eq_e)
        tri_strict = (blk_t < blk).astype(BF16)
        cnt_col = lc_eq[:, LANES - 1:LANES]
        before = _dot(tri_strict, jnp.broadcast_to(cnt_col, (nb, LANES)).astype(BF16))[:, :1]
        rank = lc_eq - eq[e].astype(F32) + before
        sel = gt[e] | (eq[e] & (rank < need[e]))
        sel_bf = sel.astype(BF16)

        lc, off_row, inc_row = block_prefix(sel_bf)
        off1 = off_row[0:1]
        inc1 = inc_row[0:1]
        onehot = ((slot >= off1) & (slot < inc1))
        oh_bf = onehot.astype(BF16)
        offk = jnp.sum(jnp.where(onehot, off1, 0.0), axis=1, keepdims=True)
        jl = slot - offk
        m = _dot(oh_bf, lc.astype(BF16))
        below = (m <= jl)
        kb_row = _dot_nt(kvals, oh_bf)
        r_row = _dot_nt(ones_row, below.astype(BF16))
        idx_ref[0, e:e + 1, :] = (kb_row[0:1] * float(LANES) + r_row[0:1]).astype(jnp.int32)

        msel = _dot(oh_bf, sel_bf)
        hit = (m == jl + 1.0) & (msel > 0.5)
        a_hi, a_mid = _split(aff[e])
        a_lo = (aff[e] - a_hi.astype(F32) - a_mid.astype(F32)).astype(BF16)
        aff_rows = _dot(oh_bf, a_hi) + (_dot(oh_bf, a_mid) + _dot(oh_bf, a_lo))
        g_ref[0, e] = jnp.sum(jnp.where(hit, aff_rows, 0.0), axis=1, keepdims=True)


def _select_call(aff_t, cap):
    bt, ne, n = aff_t.shape
    nb = n // LANES
    return pl.pallas_call(
        functools.partial(_select_kernel, cap=cap),
        grid=(bt,),
        in_specs=[pl.BlockSpec((1, ne, nb, LANES), lambda b: (b, 0, 0, 0))],
        out_specs=[pl.BlockSpec((1, ne, cap), lambda b: (b, 0, 0)),
                   pl.BlockSpec((1, ne, cap, 1), lambda b: (b, 0, 0, 0))],
        out_shape=[jax.ShapeDtypeStruct((bt, ne, cap), jnp.int32),
                   jax.ShapeDtypeStruct((bt, ne, cap, 1), F32)],
        compiler_params=_cparams(("parallel",)),
        name="expert_select",
    )(aff_t.reshape(bt, ne, nb, LANES))


def _ffn_kernel(idx_ref, idxn_ref, h_hbm, g_ref, gate2_ref, wg_ref, wu_ref, wd_ref, o_ref,
                xs_ref, wgb_ref, wub_ref, wdb_ref, sem, *, cap, chunk, gs, d):
    e = pl.program_id(0)
    bb = pl.program_id(1)
    nbb = pl.num_programs(1)
    step = e * nbb + bb
    slot = step % 2
    rows = gs * cap
    tile = d // LANES

    def row_copy(ids_ref, b0, dst_slot, s, j):
        tok = pl.multiple_of(ids_ref[s, 0, 0, j] * tile, tile)
        return pltpu.make_async_copy(h_hbm.at[b0 + s, pl.ds(tok, tile), :],
                                     xs_ref.at[dst_slot, pl.ds((s * cap + j) * tile, tile), :], sem.at[dst_slot])

    def wait_slot(s):
        pltpu.make_async_copy(h_hbm.at[0, pl.ds(0, rows * tile), :], xs_ref.at[s], sem.at[s]).wait()

    @pl.when(step == 0)
    def _():
        for s in range(gs):
            def start(j, carry):
                row_copy(idx_ref, bb * gs, slot, s, j).start()
                return carry
            lax.fori_loop(0, cap, start, 0, unroll=8)

    @pl.when(bb == 0)
    def _():
        wgb_ref[...] = wg_ref[0, 0].astype(BF16)
        wub_ref[...] = wu_ref[0, 0].astype(BF16)
        wdb_ref[...] = wd_ref[0, 0].astype(BF16)

    wait_slot(slot)

    bn = ((bb + 1) % nbb) * gs
    for c in range(rows // chunk):
        for r in range(c * chunk, (c + 1) * chunk):
            row_copy(idxn_ref, bn, 1 - slot, r // cap, r % cap).start()
        pieces = [xs_ref[slot, pl.ds(c * chunk * tile + k, chunk, stride=tile), :].astype(BF16) for k in range(tile)]
        xb = jnp.concatenate(pieces, axis=-1)
        gate = _dot(xb, wgb_ref[...])
        up = _dot(xb, wub_ref[...])
        hid = (gate * jax.nn.sigmoid(gate)) * up
        y = _dot(hid.astype(BF16), wdb_ref[...])
        s, r0 = (c * chunk) // cap, (c * chunk) % cap
        o_ref[s, 0, r0:r0 + chunk, :] = (y * g_ref[s, 0, r0:r0 + chunk, :]) * gate2_ref[s]

    @pl.when(step == pl.num_programs(0) * nbb - 1)
    def _():
        wait_slot(1 - slot)


def _ffn_call(idx, h2t, g, gate2, wg, wu, wd, layer, *, gs):
    bt, ne, cap = idx.shape
    d = gate2.shape[2]
    ff = wg.shape[3]
    nbb = bt // gs
    chunk = min(cap, 256)
    wspec = lambda s: pl.BlockSpec((1, 1) + s, lambda e, b: (layer, e, 0, 0))
    idx4 = idx.reshape(bt, ne, 1, cap)
    nxt = lambda e, b: ((b + 1) % nbb, jnp.minimum(e + (b + 1) // nbb, ne - 1), 0, 0)
    return pl.pallas_call(
        functools.partial(_ffn_kernel, cap=cap, chunk=chunk, gs=gs, d=d),
        grid=(ne, nbb),
        in_specs=[pl.BlockSpec((gs, 1, 1, cap), lambda e, b: (b, e, 0, 0), memory_space=pltpu.SMEM),
                  pl.BlockSpec((gs, 1, 1, cap), nxt, memory_space=pltpu.SMEM),
                  pl.BlockSpec(memory_space=pl.ANY),
                  pl.BlockSpec((gs, 1, cap, 1), lambda e, b: (b, e, 0, 0)),
                  pl.BlockSpec((gs, 1, d), lambda e, b: (b, 0, 0)),
                  wspec((d, ff)), wspec((d, ff)), wspec((ff, d))],
        out_specs=pl.BlockSpec((gs, 1, cap, d), lambda e, b: (b, e, 0, 0)),
        out_shape=jax.ShapeDtypeStruct((bt, ne, cap, d), F32),
        scratch_shapes=[pltpu.VMEM((2, gs * cap * d // LANES, LANES), F32), pltpu.VMEM((d, ff), BF16),
                        pltpu.VMEM((d, ff), BF16), pltpu.VMEM((ff, d), BF16), pltpu.SemaphoreType.DMA((2,))],
        compiler_params=_cparams(("arbitrary", "arbitrary")),
        name="expert_ffn",
    )(idx4, idx4, h2t, g, gate2, wg, wu, wd)


def _combine_kernel(idx_ref, x_hbm, z_ref, gfin_ref, o_hbm, acc_ref, sem_in, sem_out, *, cap, final_norm, nchunk):
    b = pl.program_id(0)
    e = pl.program_id(1)
    rc = acc_ref.shape[0] // nchunk

    def in_copy(bi, c):
        rs = pl.ds(c * rc, rc)
        return pltpu.make_async_copy(x_hbm.at[bi, rs, :], acc_ref.at[rs, :], sem_in.at[c])

    def out_copy(c):
        rs = pl.ds(c * rc, rc)
        return pltpu.make_async_copy(acc_ref.at[rs, :], o_hbm.at[b, rs, :], sem_out.at[c])

    @pl.when((e == 0) & (b == 0))
    def _():
        for c in range(nchunk):
            in_copy(b, c).start()

    @pl.when(e == 0)
    def _():
        for c in range(nchunk):
            in_copy(b, c).wait()

    group = 4
    for j0 in range(0, cap, group):
        toks = [idx_ref[0, 0, 0, j0 + u] for u in range(group)]
        vals = [acc_ref[pl.ds(toks[u], 1), :] + z_ref[0, 0, j0 + u:j0 + u + 1, :] for u in range(group)]
        for u in range(group):
            acc_ref[pl.ds(toks[u], 1), :] = vals[u]

    @pl.when(e == pl.num_programs(1) - 1)
    def _():
        for c in range(nchunk):
            if final_norm:
                rs = pl.ds(c * rc, rc)
                acc_ref[rs, :] = _rms(acc_ref[rs, :], gfin_ref[...])
            out_copy(c).start()
        for c in range(nchunk):
            out_copy(c).wait()

            @pl.when(b + 1 < pl.num_programs(0))
            def _():
                in_copy(b + 1, c).start()


def _combine_call(idx, x, z, g_final, *, final_norm):
    bt, ne, cap = idx.shape
    _, n, d = x.shape
    nchunk = 32
    return pl.pallas_call(
        functools.partial(_combine_kernel, cap=cap, final_norm=final_norm, nchunk=nchunk),
        grid=(bt, ne),
        in_specs=[pl.BlockSpec((1, 1, 1, cap), lambda b, e: (b, e, 0, 0), memory_space=pltpu.SMEM),
                  pl.BlockSpec(memory_space=pl.ANY),
                  pl.BlockSpec((1, 1, cap, d), lambda b, e: (b, e, 0, 0)),
                  pl.BlockSpec((1, d), lambda b, e: (0, 0))],
        out_specs=pl.BlockSpec(memory_space=pl.ANY),
        out_shape=jax.ShapeDtypeStruct((bt, n, d), F32),
        scratch_shapes=[pltpu.VMEM((n, d), F32), pltpu.SemaphoreType.DMA((nchunk,)),
                        pltpu.SemaphoreType.DMA((nchunk,))],
        compiler_params=_cparams(("arbitrary", "arbitrary")),
        name="expert_combine",
    )(idx.reshape(bt, ne, 1, cap), x, z, g_final)


def _gate_weights(wr, br, wi, bi):
    depth, _, h, hd, _ = wr.shape
    w = jnp.stack([wr, wi], axis=2)
    eye = jnp.eye(h, dtype=w.dtype)
    dense = (w[:, :, :, :, :, None, :] * eye[:, None, :, None]).reshape(depth, 2, 2, h * hd, h * hd)
    ng = h * hd // GATE_GROUP
    blocks = dense.reshape(depth, 2, 2, ng, GATE_GROUP, ng, GATE_GROUP)
    diag = jnp.stack([blocks[:, :, :, g, :, g, :] for g in range(ng)], axis=1)
    wg = diag.transpose(0, 1, 4, 2, 3, 5).reshape(depth, ng, GATE_GROUP, 4 * GATE_GROUP)
    b = jnp.stack([br, bi], axis=2).reshape(depth, 2, 2, ng, GATE_GROUP)
    bg = b.transpose(0, 3, 1, 2, 4).reshape(depth, ng, 1, 4 * GATE_GROUP)
    return wg.astype(BF16), bg


def _channel_dft(fw):
    gd = fw // FFT_GROUPS
    c, s = _cos_sin(gd)
    eye = np.eye(FFT_GROUPS)
    return _split_const(np.concatenate([np.kron(eye, c), np.kron(eye, s)], axis=1))


def _mixer(x, lp, shift, scale, h0_f, h0_b, *, row_w, tm, tb, need_out):
    af, bf, ab, bb, lg, ysc, ucs = _inproj_call(
        x, shift, scale, lp["g1"], lp["w_in"], lp["layer"], lp["cw"], lp["cb"], lp["wg"], lp["bg"], lp["lam"],
        lp["scw"], lp["g_sc"], lp["dftc"], row_w=row_w, tm=tm)
    hf, sf = _scan_call(af, bf, h0_f, None, reverse=False, tb=tb)
    hs, sb = _scan_call(ab, bb, h0_b, hf, reverse=True, tb=tb)
    if not need_out:
        return sf, sb, None
    yfft = _fourier_call(ucs, lp["g_fft"])
    return sf, sb, (hs, lg, ysc, yfft)


def _moe(x, h2, aff_t, gate2, lp, g_final, *, final_norm, gs):
    bt, n, d = x.shape
    cap = CAPACITY_FACTOR * n // N_EXPERTS
    pad = (-n) % (SUBLANES * LANES)
    if pad:
        aff_t = jnp.pad(aff_t, ((0, 0), (0, 0), (0, pad)), constant_values=-1.0)
    idx, g = _select_call(aff_t, cap)
    z = _ffn_call(idx, h2, g, gate2, lp["wge"], lp["wue"], lp["wde"], lp["layer"], gs=gs)
    return _combine_call(idx, x, z, g_final, final_norm=final_norm)


def kernel(x, c, ctx, c_ctx, w_ada, b_ada, g_norm1, w_in, lru_conv_w, lru_conv_b, lru_wr, lru_br, lru_wi,
           lru_bi, lru_lam, sc_conv_w, g_out, w_out, g_norm2, w_router, w_gate_e, w_up_e, w_down_e, g_final):
    depth = w_ada.shape[0]
    bsz, seq, d = x.shape
    ctx_len = ctx.shape[1]
    lru_w = lru_conv_w.shape[2]
    conv_w = sc_conv_w.shape[2]
    fft_w = w_in.shape[2] - 2 * lru_w - 3 * conv_w

    cs = jnp.concatenate([c, c_ctx[None, :], jnp.zeros((SUBLANES - bsz - 1, d), F32)], axis=0)
    mods = _ada_call(cs, w_ada, b_ada)
    dftc = _channel_dft(fft_w)
    gfin = g_final[None, :]
    zero_state = jnp.zeros((bsz, 1, lru_w), F32)
    wg, bg = _gate_weights(lru_wr, lru_br, lru_wi, lru_bi)
    w_router_t = _split_const(jnp.swapaxes(w_router, 1, 2))

    for l in range(depth):
        last = l == depth - 1
        mx = [mods[l, :bsz, None, k * d:(k + 1) * d] for k in range(6)]
        mc = [jnp.broadcast_to(mods[l, bsz, k * d:(k + 1) * d], (bsz, 1, d)) for k in range(6)]
        lp = dict(
            g1=g_norm1[l][None, :], w_in=w_in, cw=lru_conv_w[l], cb=lru_conv_b[l][None, :],
            wg=wg, bg=bg, lam=lru_lam[l].reshape(1, 2 * lru_w), scw=sc_conv_w[l],
            g_sc=g_out[l][None, lru_w:lru_w + conv_w], g_fft=g_out[l][None, lru_w + conv_w:], dftc=dftc,
            wge=w_gate_e, wue=w_up_e, wde=w_down_e, layer=l)
        g_lru = g_out[l][None, :lru_w]

        sf, sb, parts = _mixer(ctx, lp, mc[0], mc[1], zero_state, zero_state,
                               row_w=ctx_len, tm=ctx_len, tb=ctx_len, need_out=not last)
        if not last:
            ctx, hc2, aff_c = _outproj_call(*parts, ctx, w_out, l, g_lru, mc[2], mc[3], mc[4],
                                            g_norm2[l][None, :], w_router_t, tm=ctx_len)
            ctx = _moe(ctx, hc2, aff_c, mc[5], lp, gfin, final_norm=False, gs=bsz)

        _, _, parts = _mixer(x, lp, mx[0], mx[1], sf, sb, row_w=GRID_W, tm=min(seq, 1024), tb=min(seq, 2048), need_out=True)
        x, hx2, aff_x = _outproj_call(*parts, x, w_out, l, g_lru, mx[2], mx[3], mx[4],
                                      g_norm2[l][None, :], w_router_t, tm=min(seq, 1024))
        x = _moe(x, hx2, aff_x, mx[5], lp, gfin, final_norm=last, gs=1)
    return x
```

```python
import functools
import math

import numpy as np
import jax
import jax.numpy as jnp
from jax import lax
from jax.experimental import pallas as pl
from jax.experimental.pallas import tpu as pltpu

F32 = jnp.float32
BF16 = jnp.bfloat16
HI = lax.Precision.HIGHEST

GRID_W = 64
LRU_HEADS = 8
LRU_C = 8.0
N_EXPERTS = 16
CAPACITY_FACTOR = 2
EPS = 1e-6
FFT_GROUPS = 4

LANES = 128
SUBLANES = 8
GATE_GROUP = 256
SUB_ROWS = 512
VMEM_LIMIT = 56 * 1024 * 1024


def _cparams(sem):
    return pltpu.CompilerParams(dimension_semantics=sem, vmem_limit_bytes=VMEM_LIMIT)


def _rms(x, g):
    return x * lax.rsqrt(jnp.mean(x * x, axis=-1, keepdims=True) + EPS) * g


def _dot(a, b):
    return jnp.dot(a, b, preferred_element_type=F32)


def _dot_hi(a, b):
    return jnp.dot(a, b, preferred_element_type=F32, precision=HI)


def _split_const(m):
    m = jnp.asarray(m, F32)
    hi = m.astype(BF16)
    return jnp.stack([hi, (m - hi.astype(F32)).astype(BF16)])


def _split(x):
    hi = x.astype(BF16)
    return hi, (x - hi.astype(F32)).astype(BF16)


def _dot3_const_lhs(m_ref, x):
    x_hi, x_lo = _split(x)
    return _dot(m_ref[0], x_hi) + (_dot(m_ref[1], x_hi) + _dot(m_ref[0], x_lo))


def _dot3_const_rhs(x, m_ref):
    x_hi, x_lo = _split(x)
    return _dot(x_hi, m_ref[0]) + (_dot(x_hi, m_ref[1]) + _dot(x_lo, m_ref[0]))


def _dot_nt(a, b, precision=None):
    return lax.dot_general(a, b, (((1,), (1,)), ((), ())), preferred_element_type=F32, precision=precision)


def _ada_kernel(c_ref, w_ref, b_ref, o_ref):
    c = c_ref[...]
    o_ref[...] = _dot_hi(c * jax.nn.sigmoid(c), w_ref[...]) + b_ref[...]


def _ada_call(cs, w_ada, b_ada):
    depth, d, six_d = w_ada.shape
    nblk = six_d // d
    return pl.pallas_call(
        _ada_kernel,
        grid=(depth, nblk),
        in_specs=[
            pl.BlockSpec((SUBLANES, d), lambda l, j: (0, 0)),
            pl.BlockSpec((None, d, d), lambda l, j: (l, 0, j)),
            pl.BlockSpec((None, 1, d), lambda l, j: (l, 0, j)),
        ],
        out_specs=pl.BlockSpec((None, SUBLANES, d), lambda l, j: (l, 0, j)),
        out_shape=jax.ShapeDtypeStruct((depth, SUBLANES, six_d), F32),
        compiler_params=_cparams(("parallel", "parallel")),
        name="ada_mod",
    )(cs, w_ada, b_ada.reshape(depth, 1, six_d))


def _shift_rows(u, d, pos, row_w):
    n = u.shape[0]
    if d == 0:
        return u
    rolled = pltpu.roll(u, (-d) % n, axis=0)
    valid = (pos + d >= 0) & (pos + d < row_w)
    return jnp.where(valid, rolled, 0.0)


def _conv_rows(u, w_ref, left, pos, row_w):
    out = None
    for k in range(w_ref.shape[0]):
        term = w_ref[k:k + 1, :] * _shift_rows(u, k - left, pos, row_w)
        out = term if out is None else out + term
    return out


def _inproj_kernel(x_ref, sh_ref, sc_ref, g1_ref, win_ref, cw_ref, cb_ref, wg_ref, bg_ref, lam_ref,
                   scw_ref, gsc_ref, dft_ref,
                   af_ref, bf_ref, ab_ref, bb_ref, lg_ref, ysc_ref, ucs_ref, winb_ref, *, row_w, lru_w, conv_w, sub):
    @pl.when((pl.program_id(0) == 0) & (pl.program_id(1) == 0))
    def _():
        winb_ref[...] = win_ref[...].astype(BF16)

    tm = x_ref.shape[1]
    gmod = g1_ref[...] * (1.0 + sc_ref[0])
    shift = sh_ref[0]
    nl = -lam_ref[...]
    nsp = -LRU_C * (jnp.maximum(nl, 0.0) + jnp.log(1.0 + jnp.exp(-jnp.abs(nl))))
    pos = lax.broadcasted_iota(jnp.int32, (sub, 1), 0) % row_w
    out_refs = ((af_ref, bf_ref), (ab_ref, bb_ref))
    o = 2 * lru_w

    for t in range(tm // sub):
        rs = slice(t * sub, (t + 1) * sub)
        x = x_ref[0, rs, :]
        h = x * lax.rsqrt(jnp.mean(x * x, axis=-1, keepdims=True) + EPS) * gmod + shift
        p = _dot(h.astype(BF16), winb_ref[...])

        u = _conv_rows(p[:, :lru_w], cw_ref, 1, pos, row_w) + cb_ref[...]
        for g in range(lru_w // GATE_GROUP):
            cs = slice(g * GATE_GROUP, (g + 1) * GATE_GROUP)
            ug = u[:, cs]
            z = _dot(ug.astype(BF16), wg_ref[g]) + bg_ref[g]
            for d in range(2):
                r = jax.nn.sigmoid(z[:, (2 * d) * GATE_GROUP:(2 * d + 1) * GATE_GROUP])
                i = jax.nn.sigmoid(z[:, (2 * d + 1) * GATE_GROUP:(2 * d + 2) * GATE_GROUP])
                a = jnp.exp(r * nsp[:, d * lru_w + g * GATE_GROUP: d * lru_w + (g + 1) * GATE_GROUP])
                s = 1.0 - a * a
                inp = jnp.where(s > 0.0, s * lax.rsqrt(s), 0.0) * (i * ug)
                for k in range(GATE_GROUP // LANES):
                    plane = g * (GATE_GROUP // LANES) + k
                    out_refs[d][0][0, plane, rs, :] = a[:, k * LANES:(k + 1) * LANES]
                    out_refs[d][1][0, plane, rs, :] = inp[:, k * LANES:(k + 1) * LANES]

        lg_ref[0, rs, :] = jax.nn.gelu(p[:, lru_w:o], approximate=True)

        sc_b = p[:, o:o + conv_w]
        sc_c = p[:, o + conv_w:o + 2 * conv_w]
        sc_x = p[:, o + 2 * conv_w:o + 3 * conv_w]
        y_sc = sc_b * _conv_rows(sc_c * sc_x, scw_ref, 1, pos, row_w)
        ysc_ref[0, rs, :] = _rms(y_sc, gsc_ref[...])

        ucs_ref[0, rs, :] = _dot3_const_rhs(p[:, o + 3 * conv_w:], dft_ref)


def _inproj_call(x, shift, scale, g1, w_in, layer, cw, cb, wg, bg, lam, scw, gsc, dftc, *, row_w, tm):
    bt, n, d = x.shape
    in_cols = w_in.shape[2]
    lru_w = cw.shape[1]
    conv_w = scw.shape[1]
    fft_w = in_cols - 2 * lru_w - 3 * conv_w
    ng = lru_w // GATE_GROUP
    tok = lambda w: pl.BlockSpec((1, tm, w), lambda b, i: (b, i, 0))
    vec = lambda w: pl.BlockSpec((1, 1, w), lambda b, i: (b, 0, 0))
    full = lambda *s: pl.BlockSpec(s, lambda b, i: (0,) * len(s))
    shp = lambda w: jax.ShapeDtypeStruct((bt, n, w), F32)
    planes = pl.BlockSpec((1, lru_w // LANES, tm, LANES), lambda b, i: (b, 0, i, 0))
    return pl.pallas_call(
        functools.partial(_inproj_kernel, row_w=row_w, lru_w=lru_w, conv_w=conv_w, sub=min(tm, SUB_ROWS)),
        grid=(bt, n // tm),
        in_specs=[tok(d), vec(d), vec(d), full(1, d),
                  pl.BlockSpec((None, d, in_cols), lambda b, i: (layer, 0, 0), pipeline_mode=pl.Buffered(1)),
                  full(*cw.shape), full(1, lru_w),
                  pl.BlockSpec((None, ng, GATE_GROUP, 4 * GATE_GROUP), lambda b, i: (layer, 0, 0, 0)),
                  pl.BlockSpec((None, ng, 1, 4 * GATE_GROUP), lambda b, i: (layer, 0, 0, 0)), full(1, 2 * lru_w),
                  full(*scw.shape), full(1, conv_w), full(2, fft_w, 2 * fft_w)],
        out_specs=[planes] * 4 + [tok(lru_w), tok(conv_w), tok(2 * fft_w)],
        out_shape=[jax.ShapeDtypeStruct((bt, lru_w // LANES, n, LANES), F32)] * 4
        + [shp(lru_w), shp(conv_w), shp(2 * fft_w)],
        scratch_shapes=[pltpu.VMEM((d, in_cols), BF16)],
        compiler_params=_cparams(("arbitrary", "arbitrary")),
        name="inproj_local",
    )(x, shift, scale, g1, w_in, cw, cb, wg, bg, lam, scw, gsc, dftc)


def _scan_kernel(*refs, reverse, add_other):
    if add_other:
        a_ref, b_ref, h0_ref, other_ref, h_ref, hl_ref, carry_ref = refs
    else:
        a_ref, b_ref, h0_ref, h_ref, hl_ref, carry_ref = refs
        other_ref = None

    @pl.when(pl.program_id(2) == 0)
    def _():
        carry_ref[...] = jnp.broadcast_to(h0_ref[0], carry_ref.shape)

    tb = a_ref.shape[2]
    group = SUBLANES * SUBLANES
    row = lax.broadcasted_iota(jnp.int32, (SUBLANES, LANES), 0)
    steps = range(SUBLANES - 1, -1, -1) if reverse else range(SUBLANES)
    first = SUBLANES - 1 if reverse else 0
    last = 0 if reverse else SUBLANES - 1
    groups = range(tb // group - 1, -1, -1) if reverse else range(tb // group)

    def shift_chunks(v, k):
        return pltpu.roll(v, (SUBLANES - k) if reverse else k, axis=0)

    carry = carry_ref[...]
    for g in groups:
        rows = [pl.ds(g * group + s, SUBLANES, stride=SUBLANES) for s in range(SUBLANES)]
        hs, ps = {}, {}
        h = p = None
        for s in steps:
            a = a_ref[0, 0, rows[s], :]
            b = b_ref[0, 0, rows[s], :]
            h = b if h is None else a * h + b
            p = a if p is None else p * a
            hs[s], ps[s] = h, p
        pe, he = p, h
        for k in (1, 2, 4):
            valid = (row <= SUBLANES - 1 - k) if reverse else (row >= k)
            he = jnp.where(valid, he + pe * shift_chunks(he, k), he)
            pe = jnp.where(valid, pe * shift_chunks(pe, k), pe)
        end = he + pe * carry
        h_in = jnp.where(row == first, carry, shift_chunks(end, 1))
        carry = jnp.broadcast_to(end[last:last + 1], end.shape)
        for s in range(SUBLANES):
            out = hs[s] + ps[s] * h_in
            if other_ref is not None:
                out = out + other_ref[0, 0, rows[s], :]
            h_ref[0, 0, rows[s], :] = out
    carry_ref[...] = carry
    hl_ref[0] = carry[0:1]


def _scan_call(a, b, h0, other, *, reverse, tb):
    bt, npl, n, _ = a.shape
    nt = n // tb
    blk = (lambda bi, l, i: (bi, l, nt - 1 - i, 0)) if reverse else (lambda bi, l, i: (bi, l, i, 0))
    tok = pl.BlockSpec((1, 1, tb, LANES), blk)
    st = pl.BlockSpec((1, 1, LANES), lambda bi, l, i: (bi, 0, l))
    ins = [a, b, h0] + ([other] if other is not None else [])
    in_specs = [tok, tok, st] + ([tok] if other is not None else [])
    return pl.pallas_call(
        functools.partial(_scan_kernel, reverse=reverse, add_other=other is not None),
        grid=(bt, npl, nt),
        in_specs=in_specs,
        out_specs=[tok, st],
        out_shape=[jax.ShapeDtypeStruct(a.shape, F32), jax.ShapeDtypeStruct((bt, 1, npl * LANES), F32)],
        scratch_shapes=[pltpu.VMEM((SUBLANES, LANES), F32)],
        compiler_params=_cparams(("parallel", "parallel", "arbitrary")),
        name="lru_scan_bwd" if reverse else "lru_scan_fwd",
    )(*ins)


def _cos_sin(n):
    k = np.arange(n, dtype=np.float64)
    ang = 2.0 * np.pi * np.outer(k, k) / n
    return np.cos(ang), np.sin(ang)


def _slab_copies(src_hbm, dst_ref, sem, b, first, count, width, slot):
    return [pltpu.make_async_copy(src_hbm.at[b, :, first + j, :], dst_ref.at[slot, :, pl.ds(j * width, width)],
                                  sem.at[slot]) for j in range(count)]


def _prefetch_slabs(src_hbm, dst_ref, sem, count, width):
    b, i = pl.program_id(0), pl.program_id(1)
    ni = pl.num_programs(1)
    step = b * ni + i
    slot = step % 2

    @pl.when(step == 0)
    def _():
        for cp in _slab_copies(src_hbm, dst_ref, sem, b, i * count, count, width, slot):
            cp.start()

    @pl.when(step + 1 < pl.num_programs(0) * ni)
    def _():
        wrap = i + 1 == ni
        bn = jnp.where(wrap, b + 1, b)
        nxt = jnp.where(wrap, 0, i + 1)
        for cp in _slab_copies(src_hbm, dst_ref, sem, bn, nxt * count, count, width, 1 - slot):
            cp.start()

    for cp in _slab_copies(src_hbm, dst_ref, sem, b, i * count, count, width, slot):
        cp.wait()
    return step, slot


def _dft1_kernel(x_hbm, m_ref, tc_ref, ts_ref, o_ref, xs_ref, sem, *, n2, jn, fw):
    _, slot = _prefetch_slabs(x_hbm, xs_ref, sem, jn, 2 * fw)
    r = _dot(m_ref[0], xs_ref[slot].astype(BF16))
    for j in range(jn):
        base = j * 2 * fw
        c_uc = r[:n2, base:base + fw]
        c_us = r[:n2, base + fw:base + 2 * fw]
        s_uc = r[n2:, base:base + fw]
        s_us = r[n2:, base + fw:base + 2 * fw]
        br = c_uc - s_us
        bi = -(c_us + s_uc)
        tc = tc_ref[0, :, j:j + 1]
        ts = ts_ref[0, :, j:j + 1]
        o_ref[0, j, :, :fw] = br * tc + bi * ts
        o_ref[0, j, :, fw:] = bi * tc - br * ts


def _dft2_kernel(p_hbm, m_ref, g_ref, y_hbm, ps_ref, ys_ref, sem_in, sem_out, *, n1, kn, fw):
    step, slot = _prefetch_slabs(p_hbm, ps_ref, sem_in, kn, 2 * fw)
    b, i = pl.program_id(0), pl.program_id(1)
    last = pl.num_programs(0) * pl.num_programs(1) - 1

    def out_copies(s):
        return [pltpu.make_async_copy(ys_ref.at[s, :, pl.ds(k * fw, fw)], y_hbm.at[b, :, i * kn + k, :],
                                      sem_out.at[s]) for k in range(kn)]

    r = _dot(m_ref[0], ps_ref[slot].astype(BF16))

    @pl.when(step >= 2)
    def _():
        for cp in out_copies(slot):
            cp.wait()

    for k in range(kn):
        base = k * 2 * fw
        y = r[:n1, base:base + fw] + r[n1:, base + fw:base + 2 * fw]
        ys_ref[slot, :, k * fw:(k + 1) * fw] = _rms(y, g_ref[...])
    for cp in out_copies(slot):
        cp.start()

    @pl.when(step == last)
    def _():
        for cp in out_copies(slot):
            cp.wait()

    @pl.when((step == last) & (step >= 1))
    def _():
        for cp in out_copies(1 - slot):
            cp.wait()


def _dft_direct_kernel(x_ref, m_ref, g_ref, o_ref, *, n, fw):
    x = x_ref[0]
    r = _dot3_const_lhs(m_ref, x)
    y = r[:n, :fw] - r[n:, fw:]
    o_ref[0] = _rms(y, g_ref[...])


def _fourier_call(ucs, g_fft):
    bt, n, fw2 = ucs.shape
    fw = fw2 // 2
    gd = fw // FFT_GROUPS
    scale = 1.0 / math.sqrt(n * gd)
    if n <= 512:
        c, s = _cos_sin(n)
        m = _split_const(np.concatenate([c, s], 0) * scale)
        return pl.pallas_call(
            functools.partial(_dft_direct_kernel, n=n, fw=fw),
            grid=(bt,),
            in_specs=[pl.BlockSpec((1, n, fw2), lambda b: (b, 0, 0)),
                      pl.BlockSpec((2, 2 * n, n), lambda b: (0, 0, 0)),
                      pl.BlockSpec((1, fw), lambda b: (0, 0))],
            out_specs=pl.BlockSpec((1, n, fw), lambda b: (b, 0, 0)),
            out_shape=jax.ShapeDtypeStruct((bt, n, fw), F32),
            compiler_params=_cparams(("parallel",)),
            name="dft_direct",
        )(ucs, m, g_fft)

    n1 = LANES
    n2 = n // n1
    jn = 32
    c2, s2 = _cos_sin(n2)
    m1 = _split_const(np.concatenate([c2, s2], 0))
    ang = 2.0 * np.pi * np.outer(np.arange(n2), np.arange(n1)) / n
    tw = lambda f: jnp.asarray(f(ang).reshape(n2, n1 // jn, jn).transpose(1, 0, 2), F32)
    p = pl.pallas_call(
        functools.partial(_dft1_kernel, n2=n2, jn=jn, fw=fw),
        grid=(bt, n1 // jn),
        in_specs=[pl.BlockSpec(memory_space=pl.ANY),
                  pl.BlockSpec((2, 2 * n2, n2), lambda b, i: (0, 0, 0)),
                  pl.BlockSpec((1, n2, jn), lambda b, i: (i, 0, 0)),
                  pl.BlockSpec((1, n2, jn), lambda b, i: (i, 0, 0))],
        out_specs=pl.BlockSpec((1, jn, n2, fw2), lambda b, i: (b, i, 0, 0)),
        out_shape=jax.ShapeDtypeStruct((bt, n1, n2, fw2), F32),
        scratch_shapes=[pltpu.VMEM((2, n2, jn * fw2), F32), pltpu.SemaphoreType.DMA((2,))],
        compiler_params=_cparams(("arbitrary", "arbitrary")),
        name="dft_stage1",
    )(ucs.reshape(bt, n2, n1, fw2), m1, tw(np.cos), tw(np.sin))

    kn = min(16, n2)
    c1, s1 = _cos_sin(n1)
    m2 = _split_const(np.concatenate([c1, s1], 0) * scale)
    y = pl.pallas_call(
        functools.partial(_dft2_kernel, n1=n1, kn=kn, fw=fw),
        grid=(bt, n2 // kn),
        in_specs=[pl.BlockSpec(memory_space=pl.ANY),
                  pl.BlockSpec((2, 2 * n1, n1), lambda b, i: (0, 0, 0)),
                  pl.BlockSpec((1, fw), lambda b, i: (0, 0))],
        out_specs=pl.BlockSpec(memory_space=pl.ANY),
        out_shape=jax.ShapeDtypeStruct((bt, n1, n2, fw), F32),
        scratch_shapes=[pltpu.VMEM((2, n1, kn * fw2), F32), pltpu.VMEM((2, n1, kn * fw), F32),
                        pltpu.SemaphoreType.DMA((2,)), pltpu.SemaphoreType.DMA((2,))],
        compiler_params=_cparams(("arbitrary", "arbitrary")),
        name="dft_stage2",
    )(p, m2, g_fft)
    return y.reshape(bt, n, fw)


def _outproj_kernel(hs_ref, lg_ref, ysc_ref, yfft_ref, x_ref, wout_ref, glru_ref, gate_ref,
                    sh_ref, sc_ref, g2_ref, wr_ref, xn_ref, h2_ref, aff_ref, woutb_ref, *, sub):
    @pl.when((pl.program_id(0) == 0) & (pl.program_id(1) == 0))
    def _():
        woutb_ref[...] = wout_ref[...].astype(BF16)

    tm, d = x_ref.shape[1], x_ref.shape[2]
    tile = d // LANES
    gmod = g2_ref[...] * (1.0 + sc_ref[0])
    for t in range(tm // sub):
        rs = slice(t * sub, (t + 1) * sub)
        hs = jnp.concatenate([hs_ref[0, k, rs, :] for k in range(hs_ref.shape[1])], axis=-1)
        y_lru = _rms(hs * lg_ref[0, rs, :], glru_ref[...])
        y = jnp.concatenate([y_lru.astype(BF16), ysc_ref[0, rs, :].astype(BF16), yfft_ref[0, rs, :].astype(BF16)],
                            axis=-1)
        xn = x_ref[0, rs, :] + gate_ref[0] * _dot(y, woutb_ref[...])
        xn_ref[0, rs, :] = xn
        h2 = xn * lax.rsqrt(jnp.mean(xn * xn, axis=-1, keepdims=True) + EPS) * gmod + sh_ref[0]
        for k in range(tile):
            h2_ref[0, pl.ds(t * sub * tile + k, sub, stride=tile), :] = h2[:, k * LANES:(k + 1) * LANES]
        h_hi, h_lo = _split(h2)
        logits = _dot_nt(wr_ref[0], h_hi) + (_dot_nt(wr_ref[1], h_hi) + _dot_nt(wr_ref[0], h_lo))
        m = jnp.max(logits, axis=0, keepdims=True)
        e = jnp.exp(logits - m)
        aff_ref[0, :, rs] = e / jnp.sum(e, axis=0, keepdims=True)


def _outproj_call(hs, lg, ysc, yfft, x, w_out, layer, g_lru, gate, shift, scale, g2, w_router_t, *, tm):
    bt, n, d = x.shape
    ne = w_router_t.shape[2]
    tok = lambda w: pl.BlockSpec((1, tm, w), lambda b, i: (b, i, 0))
    vec = lambda w: pl.BlockSpec((1, 1, w), lambda b, i: (b, 0, 0))
    full = lambda *s: pl.BlockSpec(s, lambda b, i: (0,) * len(s))
    return pl.pallas_call(
        functools.partial(_outproj_kernel, sub=min(tm, SUB_ROWS)),
        grid=(bt, n // tm),
        in_specs=[pl.BlockSpec((1, hs.shape[1], tm, LANES), lambda b, i: (b, 0, i, 0)),
                  tok(lg.shape[2]), tok(ysc.shape[2]), tok(yfft.shape[2]), tok(d),
                  pl.BlockSpec((None,) + w_out.shape[1:], lambda b, i: (layer, 0, 0), pipeline_mode=pl.Buffered(1)),
                  full(1, lg.shape[2]), vec(d), vec(d), vec(d), full(1, d),
                  pl.BlockSpec((2, None, ne, d), lambda b, i: (0, layer, 0, 0))],
        out_specs=[tok(d), pl.BlockSpec((1, tm * d // LANES, LANES), lambda b, i: (b, i, 0)),
                   pl.BlockSpec((1, ne, tm), lambda b, i: (b, 0, i))],
        out_shape=[jax.ShapeDtypeStruct((bt, n, d), F32), jax.ShapeDtypeStruct((bt, n * d // LANES, LANES), F32),
                   jax.ShapeDtypeStruct((bt, ne, n), F32)],
        scratch_shapes=[pltpu.VMEM(w_out.shape[1:], BF16)],
        compiler_params=_cparams(("arbitrary", "arbitrary")),
        name="outproj_router",
    )(hs, lg, ysc, yfft, x, w_out, g_lru, gate, shift, scale, g2, w_router_t)


def _select_kernel(aff_ref, idx_ref, g_ref, *, cap):
    aff = aff_ref[0]
    ne, nb, _ = aff.shape

    def count(mask):
        c = jnp.sum(mask.astype(jnp.int32), axis=1, keepdims=True)
        return jnp.sum(c, axis=2, keepdims=True)

    def bit_step(i, t):
        cand = t | (jnp.int32(1) << (30 - i))
        return jnp.where(count(aff >= lax.bitcast_convert_type(cand, F32)) >= cap, cand, t)

    bits = lax.fori_loop(0, 31, bit_step, jnp.zeros((ne, 1, 1), jnp.int32))
    lo = lax.bitcast_convert_type(bits, F32)
    hi = lax.bitcast_convert_type(bits + 1, F32)

    def mid_step(i, lh):
        lo, hi = lh
        mid = (lo + hi) * 0.5
        ok = count(aff >= mid) >= cap
        return jnp.where(ok, mid, lo), jnp.where(ok, hi, mid)

    lo, hi = lax.fori_loop(0, 14, mid_step, (lo, hi))
    gt = aff >= hi
    eq = (aff >= lo) & (aff < hi)
    need = (cap - count(gt)).astype(F32)

    lane = lax.broadcasted_iota(jnp.int32, (LANES, LANES), 0)
    lane_t = lax.broadcasted_iota(jnp.int32, (LANES, LANES), 1)
    tri_incl = (lane <= lane_t).astype(BF16)
    blk = lax.broadcasted_iota(jnp.int32, (nb, nb), 0)
    blk_t = lax.broadcasted_iota(jnp.int32, (nb, nb), 1)
    tri_blk = (blk <= blk_t).astype(BF16)
    ones_row = jnp.ones((SUBLANES, LANES), BF16)
    kvals = lax.broadcasted_iota(jnp.int32, (SUBLANES, nb), 1).astype(BF16)
    slot = lax.broadcasted_iota(jnp.int32, (cap, 1), 0).astype(F32)

    def block_prefix(mask_bf16):
        lc = _dot(mask_bf16, tri_incl)
        cnt_row = _dot_nt(ones_row, mask_bf16)
        inc_row = _dot(cnt_row.astype(BF16), tri_blk)
        return lc, inc_row - cnt_row, inc_row

    for e in range(ne):
        eq_e = eq[e].astype(BF16)
        lc_eq, off_eq, _ = block_prefix(eq_e)
        tri_strict = (blk_t < blk).astype(BF16)
        cnt_col = lc_eq[:, LANES - 1:LANES]
        before = _dot(tri_strict, jnp.broadcast_to(cnt_col, (nb, LANES)).astype(BF16))[:, :1]
        rank = lc_eq - eq[e].astype(F32) + before
        sel = gt[e] | (eq[e] & (rank < need[e]))
        sel_bf = sel.astype(BF16)

        lc, off_row, inc_row = block_prefix(sel_bf)
        off1 = off_row[0:1]
        inc1 = inc_row[0:1]
        onehot = ((slot >= off1) & (slot < inc1))
        oh_bf = onehot.astype(BF16)
        offk = jnp.sum(jnp.where(onehot, off1, 0.0), axis=1, keepdims=True)
        jl = slot - offk
        m = _dot(oh_bf, lc.astype(BF16))
        below = (m <= jl)
        kb_row = _dot_nt(kvals, oh_bf)
        r_row = _dot_nt(ones_row, below.astype(BF16))
        idx_ref[0, e:e + 1, :] = (kb_row[0:1] * float(LANES) + r_row[0:1]).astype(jnp.int32)

        msel = _dot(oh_bf, sel_bf)
        hit = (m == jl + 1.0) & (msel > 0.5)
        a_hi, a_mid = _split(aff[e])
        a_lo = (aff[e] - a_hi.astype(F32) - a_mid.astype(F32)).astype(BF16)
        aff_rows = _dot(oh_bf, a_hi) + (_dot(oh_bf, a_mid) + _dot(oh_bf, a_lo))
        g_ref[0, e] = jnp.sum(jnp.where(hit, aff_rows, 0.0), axis=1, keepdims=True)


def _select_call(aff_t, cap):
    bt, ne, n = aff_t.shape
    nb = n // LANES
    return pl.pallas_call(
        functools.partial(_select_kernel, cap=cap),
        grid=(bt,),
        in_specs=[pl.BlockSpec((1, ne, nb, LANES), lambda b: (b, 0, 0, 0))],
        out_specs=[pl.BlockSpec((1, ne, cap), lambda b: (b, 0, 0)),
                   pl.BlockSpec((1, ne, cap, 1), lambda b: (b, 0, 0, 0))],
        out_shape=[jax.ShapeDtypeStruct((bt, ne, cap), jnp.int32),
                   jax.ShapeDtypeStruct((bt, ne, cap, 1), F32)],
        compiler_params=_cparams(("parallel",)),
        name="expert_select",
    )(aff_t.reshape(bt, ne, nb, LANES))


def _ffn_kernel(idx_ref, idxn_ref, h_hbm, g_ref, gate2_ref, wg_ref, wu_ref, wd_ref, o_ref,
                xs_ref, wgb_ref, wub_ref, wdb_ref, sem, *, cap, chunk, gs, d):
    e = pl.program_id(0)
    bb = pl.program_id(1)
    nbb = pl.num_programs(1)
    step = e * nbb + bb
    slot = step % 2
    rows = gs * cap
    tile = d // LANES

    def row_copy(ids_ref, b0, dst_slot, s, j):
        tok = pl.multiple_of(ids_ref[s, 0, 0, j] * tile, tile)
        return pltpu.make_async_copy(h_hbm.at[b0 + s, pl.ds(tok, tile), :],
                                     xs_ref.at[dst_slot, pl.ds((s * cap + j) * tile, tile), :], sem.at[dst_slot])

    def wait_slot(s):
        pltpu.make_async_copy(h_hbm.at[0, pl.ds(0, rows * tile), :], xs_ref.at[s], sem.at[s]).wait()

    @pl.when(step == 0)
    def _():
        for s in range(gs):
            def start(j, carry):
                row_copy(idx_ref, bb * gs, slot, s, j).start()
                return carry
            lax.fori_loop(0, cap, start, 0, unroll=8)

    @pl.when(bb == 0)
    def _():
        wgb_ref[...] = wg_ref[0, 0].astype(BF16)
        wub_ref[...] = wu_ref[0, 0].astype(BF16)
        wdb_ref[...] = wd_ref[0, 0].astype(BF16)

    wait_slot(slot)

    bn = ((bb + 1) % nbb) * gs
    for c in range(rows // chunk):
        for r in range(c * chunk, (c + 1) * chunk):
            row_copy(idxn_ref, bn, 1 - slot, r // cap, r % cap).start()
        pieces = [xs_ref[slot, pl.ds(c * chunk * tile + k, chunk, stride=tile), :].astype(BF16) for k in range(tile)]
        xb = jnp.concatenate(pieces, axis=-1)
        gate = _dot(xb, wgb_ref[...])
        up = _dot(xb, wub_ref[...])
        hid = (gate * jax.nn.sigmoid(gate)) * up
        y = _dot(hid.astype(BF16), wdb_ref[...])
        s, r0 = (c * chunk) // cap, (c * chunk) % cap
        o_ref[s, 0, r0:r0 + chunk, :] = (y * g_ref[s, 0, r0:r0 + chunk, :]) * gate2_ref[s]

    @pl.when(step == pl.num_programs(0) * nbb - 1)
    def _():
        wait_slot(1 - slot)


def _ffn_call(idx, h2t, g, gate2, wg, wu, wd, layer, *, gs):
    bt, ne, cap = idx.shape
    d = gate2.shape[2]
    ff = wg.shape[3]
    nbb = bt // gs
    chunk = min(cap, 256)
    wspec = lambda s: pl.BlockSpec((1, 1) + s, lambda e, b: (layer, e, 0, 0))
    idx4 = idx.reshape(bt, ne, 1, cap)
    nxt = lambda e, b: ((b + 1) % nbb, jnp.minimum(e + (b + 1) // nbb, ne - 1), 0, 0)
    return pl.pallas_call(
        functools.partial(_ffn_kernel, cap=cap, chunk=chunk, gs=gs, d=d),
        grid=(ne, nbb),
        in_specs=[pl.BlockSpec((gs, 1, 1, cap), lambda e, b: (b, e, 0, 0), memory_space=pltpu.SMEM),
                  pl.BlockSpec((gs, 1, 1, cap), nxt, memory_space=pltpu.SMEM),
                  pl.BlockSpec(memory_space=pl.ANY),
                  pl.BlockSpec((gs, 1, cap, 1), lambda e, b: (b, e, 0, 0)),
                  pl.BlockSpec((gs, 1, d), lambda e, b: (b, 0, 0)),
                  wspec((d, ff)), wspec((d, ff)), wspec((ff, d))],
        out_specs=pl.BlockSpec((gs, 1, cap, d), lambda e, b: (b, e, 0, 0)),
        out_shape=jax.ShapeDtypeStruct((bt, ne, cap, d), F32),
        scratch_shapes=[pltpu.VMEM((2, gs * cap * d // LANES, LANES), F32), pltpu.VMEM((d, ff), BF16),
                        pltpu.VMEM((d, ff), BF16), pltpu.VMEM((ff, d), BF16), pltpu.SemaphoreType.DMA((2,))],
        compiler_params=_cparams(("arbitrary", "arbitrary")),
        name="expert_ffn",
    )(idx4, idx4, h2t, g, gate2, wg, wu, wd)


def _combine_kernel(idx_ref, x_hbm, z_ref, gfin_ref, o_hbm, acc_ref, sem_in, sem_out, *, cap, final_norm, nchunk):
    b = pl.program_id(0)
    e = pl.program_id(1)
    rc = acc_ref.shape[0] // nchunk

    def in_copy(bi, c):
        rs = pl.ds(c * rc, rc)
        return pltpu.make_async_copy(x_hbm.at[bi, rs, :], acc_ref.at[rs, :], sem_in.at[c])

    def out_copy(c):
        rs = pl.ds(c * rc, rc)
        return pltpu.make_async_copy(acc_ref.at[rs, :], o_hbm.at[b, rs, :], sem_out.at[c])

    @pl.when((e == 0) & (b == 0))
    def _():
        for c in range(nchunk):
            in_copy(b, c).start()

    @pl.when(e == 0)
    def _():
        for c in range(nchunk):
            in_copy(b, c).wait()

    group = 16
    for j0 in range(0, cap, group):
        toks = [idx_ref[0, 0, 0, j0 + u] for u in range(group)]
        vals = [acc_ref[pl.ds(toks[u], 1), :] + z_ref[0, 0, j0 + u:j0 + u + 1, :] for u in range(group)]
        for u in range(group):
            acc_ref[pl.ds(toks[u], 1), :] = vals[u]

    @pl.when(e == pl.num_programs(1) - 1)
    def _():
        for c in range(nchunk):
            if final_norm:
                rs = pl.ds(c * rc, rc)
                acc_ref[rs, :] = _rms(acc_ref[rs, :], gfin_ref[...])
            out_copy(c).start()
        for c in range(nchunk):
            out_copy(c).wait()

            @pl.when(b + 1 < pl.num_programs(0))
            def _():
                in_copy(b + 1, c).start()


def _combine_call(idx, x, z, g_final, *, final_norm):
    bt, ne, cap = idx.shape
    _, n, d = x.shape
    nchunk = 32
    return pl.pallas_call(
        functools.partial(_combine_kernel, cap=cap, final_norm=final_norm, nchunk=nchunk),
        grid=(bt, ne),
        in_specs=[pl.BlockSpec((1, 1, 1, cap), lambda b, e: (b, e, 0, 0), memory_space=pltpu.SMEM),
                  pl.BlockSpec(memory_space=pl.ANY),
                  pl.BlockSpec((1, 1, cap, d), lambda b, e: (b, e, 0, 0)),
                  pl.BlockSpec((1, d), lambda b, e: (0, 0))],
        out_specs=pl.BlockSpec(memory_space=pl.ANY),
        out_shape=jax.ShapeDtypeStruct((bt, n, d), F32),
        scratch_shapes=[pltpu.VMEM((n, d), F32), pltpu.SemaphoreType.DMA((nchunk,)),
                        pltpu.SemaphoreType.DMA((nchunk,))],
        compiler_params=_cparams(("arbitrary", "arbitrary")),
        name="expert_combine",
    )(idx.reshape(bt, ne, 1, cap), x, z, g_final)


def _gate_weights(wr, br, wi, bi):
    depth, _, h, hd, _ = wr.shape
    w = jnp.stack([wr, wi], axis=2)
    eye = jnp.eye(h, dtype=w.dtype)
    dense = (w[:, :, :, :, :, None, :] * eye[:, None, :, None]).reshape(depth, 2, 2, h * hd, h * hd)
    ng = h * hd // GATE_GROUP
    blocks = dense.reshape(depth, 2, 2, ng, GATE_GROUP, ng, GATE_GROUP)
    diag = jnp.stack([blocks[:, :, :, g, :, g, :] for g in range(ng)], axis=1)
    wg = diag.transpose(0, 1, 4, 2, 3, 5).reshape(depth, ng, GATE_GROUP, 4 * GATE_GROUP)
    b = jnp.stack([br, bi], axis=2).reshape(depth, 2, 2, ng, GATE_GROUP)
    bg = b.transpose(0, 3, 1, 2, 4).reshape(depth, ng, 1, 4 * GATE_GROUP)
    return wg.astype(BF16), bg


def _channel_dft(fw):
    gd = fw // FFT_GROUPS
    c, s = _cos_sin(gd)
    eye = np.eye(FFT_GROUPS)
    return _split_const(np.concatenate([np.kron(eye, c), np.kron(eye, s)], axis=1))


def _mixer(x, lp, shift, scale, h0_f, h0_b, *, row_w, tm, tb, need_out):
    af, bf, ab, bb, lg, ysc, ucs = _inproj_call(
        x, shift, scale, lp["g1"], lp["w_in"], lp["layer"], lp["cw"], lp["cb"], lp["wg"], lp["bg"], lp["lam"],
        lp["scw"], lp["g_sc"], lp["dftc"], row_w=row_w, tm=tm)
    hf, sf = _scan_call(af, bf, h0_f, None, reverse=False, tb=tb)
    hs, sb = _scan_call(ab, bb, h0_b, hf, reverse=True, tb=tb)
    if not need_out:
        return sf, sb, None
    yfft = _fourier_call(ucs, lp["g_fft"])
    return sf, sb, (hs, lg, ysc, yfft)


def _moe(x, h2, aff_t, gate2, lp, g_final, *, final_norm, gs):
    bt, n, d = x.shape
    cap = CAPACITY_FACTOR * n // N_EXPERTS
    pad = (-n) % (SUBLANES * LANES)
    if pad:
        aff_t = jnp.pad(aff_t, ((0, 0), (0, 0), (0, pad)), constant_values=-1.0)
    idx, g = _select_call(aff_t, cap)
    z = _ffn_call(idx, h2, g, gate2, lp["wge"], lp["wue"], lp["wde"], lp["layer"], gs=gs)
    return _combine_call(idx, x, z, g_final, final_norm=final_norm)


def kernel(x, c, ctx, c_ctx, w_ada, b_ada, g_norm1, w_in, lru_conv_w, lru_conv_b, lru_wr, lru_br, lru_wi,
           lru_bi, lru_lam, sc_conv_w, g_out, w_out, g_norm2, w_router, w_gate_e, w_up_e, w_down_e, g_final):
    depth = w_ada.shape[0]
    bsz, seq, d = x.shape
    ctx_len = ctx.shape[1]
    lru_w = lru_conv_w.shape[2]
    conv_w = sc_conv_w.shape[2]
    fft_w = w_in.shape[2] - 2 * lru_w - 3 * conv_w

    cs = jnp.concatenate([c, c_ctx[None, :], jnp.zeros((SUBLANES - bsz - 1, d), F32)], axis=0)
    mods = _ada_call(cs, w_ada, b_ada)
    dftc = _channel_dft(fft_w)
    gfin = g_final[None, :]
    zero_state = jnp.zeros((bsz, 1, lru_w), F32)
    wg, bg = _gate_weights(lru_wr, lru_br, lru_wi, lru_bi)
    w_router_t = _split_const(jnp.swapaxes(w_router, 1, 2))

    for l in range(depth):
        last = l == depth - 1
        mx = [mods[l, :bsz, None, k * d:(k + 1) * d] for k in range(6)]
        mc = [jnp.broadcast_to(mods[l, bsz, k * d:(k + 1) * d], (bsz, 1, d)) for k in range(6)]
        lp = dict(
            g1=g_norm1[l][None, :], w_in=w_in, cw=lru_conv_w[l], cb=lru_conv_b[l][None, :],
            wg=wg, bg=bg, lam=lru_lam[l].reshape(1, 2 * lru_w), scw=sc_conv_w[l],
            g_sc=g_out[l][None, lru_w:lru_w + conv_w], g_fft=g_out[l][None, lru_w + conv_w:], dftc=dftc,
            wge=w_gate_e, wue=w_up_e, wde=w_down_e, layer=l)
        g_lru = g_out[l][None, :lru_w]

        sf, sb, parts = _mixer(ctx, lp, mc[0], mc[1], zero_state, zero_state,
                               row_w=ctx_len, tm=ctx_len, tb=ctx_len, need_out=not last)
        if not last:
            ctx, hc2, aff_c = _outproj_call(*parts, ctx, w_out, l, g_lru, mc[2], mc[3], mc[4],
                                            g_norm2[l][None, :], w_router_t, tm=ctx_len)
            ctx = _moe(ctx, hc2, aff_c, mc[5], lp, gfin, final_norm=False, gs=bsz)

        _, _, parts = _mixer(x, lp, mx[0], mx[1], sf, sb, row_w=GRID_W, tm=min(seq, 1024), tb=min(seq, 2048), need_out=True)
        x, hx2, aff_x = _outproj_call(*parts, x, w_out, l, g_lru, mx[2], mx[3], mx[4],
                                      g_norm2[l][None, :], w_router_t, tm=min(seq, 1024))
        x = _moe(x, hx2, aff_x, mx[5], lp, gfin, final_norm=last, gs=1)
    return x
```

```python
import functools
import math

import numpy as np
import jax
import jax.numpy as jnp
from jax import lax
from jax.experimental import pallas as pl
from jax.experimental.pallas import tpu as pltpu

F32 = jnp.float32
BF16 = jnp.bfloat16
HI = lax.Precision.HIGHEST

GRID_W = 64
LRU_HEADS = 8
LRU_C = 8.0
N_EXPERTS = 16
CAPACITY_FACTOR = 2
EPS = 1e-6
FFT_GROUPS = 4

LANES = 128
SUBLANES = 8
GATE_GROUP = 256
SUB_ROWS = 512
VMEM_LIMIT = 56 * 1024 * 1024


def _cparams(sem):
    return pltpu.CompilerParams(dimension_semantics=sem, vmem_limit_bytes=VMEM_LIMIT)


def _rms(x, g):
    return x * lax.rsqrt(jnp.mean(x * x, axis=-1, keepdims=True) + EPS) * g


def _dot(a, b):
    return jnp.dot(a, b, preferred_element_type=F32)


def _dot_hi(a, b):
    return jnp.dot(a, b, preferred_element_type=F32, precision=HI)


def _split_const(m):
    m = jnp.asarray(m, F32)
    hi = m.astype(BF16)
    return jnp.stack([hi, (m - hi.astype(F32)).astype(BF16)])


def _split(x):
    hi = x.astype(BF16)
    return hi, (x - hi.astype(F32)).astype(BF16)


def _dot3_const_lhs(m_ref, x):
    x_hi, x_lo = _split(x)
    return _dot(m_ref[0], x_hi) + (_dot(m_ref[1], x_hi) + _dot(m_ref[0], x_lo))


def _dot3_const_rhs(x, m_ref):
    x_hi, x_lo = _split(x)
    return _dot(x_hi, m_ref[0]) + (_dot(x_hi, m_ref[1]) + _dot(x_lo, m_ref[0]))


def _dot_nt(a, b, precision=None):
    return lax.dot_general(a, b, (((1,), (1,)), ((), ())), preferred_element_type=F32, precision=precision)


def _ada_kernel(c_ref, w_ref, b_ref, o_ref):
    c = c_ref[...]
    o_ref[...] = _dot_hi(c * jax.nn.sigmoid(c), w_ref[...]) + b_ref[...]


def _ada_call(cs, w_ada, b_ada):
    depth, d, six_d = w_ada.shape
    nblk = six_d // d
    return pl.pallas_call(
        _ada_kernel,
        grid=(depth, nblk),
        in_specs=[
            pl.BlockSpec((SUBLANES, d), lambda l, j: (0, 0)),
            pl.BlockSpec((None, d, d), lambda l, j: (l, 0, j)),
            pl.BlockSpec((None, 1, d), lambda l, j: (l, 0, j)),
        ],
        out_specs=pl.BlockSpec((None, SUBLANES, d), lambda l, j: (l, 0, j)),
        out_shape=jax.ShapeDtypeStruct((depth, SUBLANES, six_d), F32),
        compiler_params=_cparams(("parallel", "parallel")),
        name="ada_mod",
    )(cs, w_ada, b_ada.reshape(depth, 1, six_d))


def _shift_rows(u, d, pos, row_w):
    n = u.shape[0]
    if d == 0:
        return u
    rolled = pltpu.roll(u, (-d) % n, axis=0)
    valid = (pos + d >= 0) & (pos + d < row_w)
    return jnp.where(valid, rolled, 0.0)


def _conv_rows(u, w_ref, left, pos, row_w):
    out = None
    for k in range(w_ref.shape[0]):
        term = w_ref[k:k + 1, :] * _shift_rows(u, k - left, pos, row_w)
        out = term if out is None else out + term
    return out


def _inproj_kernel(x_ref, sh_ref, sc_ref, g1_ref, win_ref, cw_ref, cb_ref, wg_ref, bg_ref, lam_ref,
                   scw_ref, gsc_ref, dft_ref,
                   af_ref, bf_ref, ab_ref, bb_ref, lg_ref, ysc_ref, ucs_ref, winb_ref, *, row_w, lru_w, conv_w, sub):
    @pl.when((pl.program_id(0) == 0) & (pl.program_id(1) == 0))
    def _():
        winb_ref[...] = win_ref[...].astype(BF16)

    tm = x_ref.shape[1]
    gmod = g1_ref[...] * (1.0 + sc_ref[0])
    shift = sh_ref[0]
    nl = -lam_ref[...]
    nsp = -LRU_C * (jnp.maximum(nl, 0.0) + jnp.log(1.0 + jnp.exp(-jnp.abs(nl))))
    pos = lax.broadcasted_iota(jnp.int32, (sub, 1), 0) % row_w
    out_refs = ((af_ref, bf_ref), (ab_ref, bb_ref))
    o = 2 * lru_w

    for t in range(tm // sub):
        rs = slice(t * sub, (t + 1) * sub)
        x = x_ref[0, rs, :]
        h = x * lax.rsqrt(jnp.mean(x * x, axis=-1, keepdims=True) + EPS) * gmod + shift
        p = _dot(h.astype(BF16), winb_ref[...])

        u = _conv_rows(p[:, :lru_w], cw_ref, 1, pos, row_w) + cb_ref[...]
        for g in range(lru_w // GATE_GROUP):
            cs = slice(g * GATE_GROUP, (g + 1) * GATE_GROUP)
            ug = u[:, cs]
            z = _dot(ug.astype(BF16), wg_ref[g]) + bg_ref[g]
            for d in range(2):
                r = jax.nn.sigmoid(z[:, (2 * d) * GATE_GROUP:(2 * d + 1) * GATE_GROUP])
                i = jax.nn.sigmoid(z[:, (2 * d + 1) * GATE_GROUP:(2 * d + 2) * GATE_GROUP])
                a = jnp.exp(r * nsp[:, d * lru_w + g * GATE_GROUP: d * lru_w + (g + 1) * GATE_GROUP])
                s = 1.0 - a * a
                inp = jnp.where(s > 0.0, s * lax.rsqrt(s), 0.0) * (i * ug)
                for k in range(GATE_GROUP // LANES):
                    plane = g * (GATE_GROUP // LANES) + k
                    out_refs[d][0][0, plane, rs, :] = a[:, k * LANES:(k + 1) * LANES]
                    out_refs[d][1][0, plane, rs, :] = inp[:, k * LANES:(k + 1) * LANES]

        lg_ref[0, rs, :] = jax.nn.gelu(p[:, lru_w:o], approximate=True)

        sc_b = p[:, o:o + conv_w]
        sc_c = p[:, o + conv_w:o + 2 * conv_w]
        sc_x = p[:, o + 2 * conv_w:o + 3 * conv_w]
        y_sc = sc_b * _conv_rows(sc_c * sc_x, scw_ref, 1, pos, row_w)
        ysc_ref[0, rs, :] = _rms(y_sc, gsc_ref[...])

        ucs_ref[0, rs, :] = _dot3_const_rhs(p[:, o + 3 * conv_w:], dft_ref)


def _inproj_call(x, shift, scale, g1, w_in, layer, cw, cb, wg, bg, lam, scw, gsc, dftc, *, row_w, tm):
    bt, n, d = x.shape
    in_cols = w_in.shape[2]
    lru_w = cw.shape[1]
    conv_w = scw.shape[1]
    fft_w = in_cols - 2 * lru_w - 3 * conv_w
    ng = lru_w // GATE_GROUP
    tok = lambda w: pl.BlockSpec((1, tm, w), lambda b, i: (b, i, 0))
    vec = lambda w: pl.BlockSpec((1, 1, w), lambda b, i: (b, 0, 0))
    full = lambda *s: pl.BlockSpec(s, lambda b, i: (0,) * len(s))
    shp = lambda w: jax.ShapeDtypeStruct((bt, n, w), F32)
    planes = pl.BlockSpec((1, lru_w // LANES, tm, LANES), lambda b, i: (b, 0, i, 0))
    return pl.pallas_call(
        functools.partial(_inproj_kernel, row_w=row_w, lru_w=lru_w, conv_w=conv_w, sub=min(tm, SUB_ROWS)),
        grid=(bt, n // tm),
        in_specs=[tok(d), vec(d), vec(d), full(1, d),
                  pl.BlockSpec((None, d, in_cols), lambda b, i: (layer, 0, 0), pipeline_mode=pl.Buffered(1)),
                  full(*cw.shape), full(1, lru_w),
                  pl.BlockSpec((None, ng, GATE_GROUP, 4 * GATE_GROUP), lambda b, i: (layer, 0, 0, 0)),
                  pl.BlockSpec((None, ng, 1, 4 * GATE_GROUP), lambda b, i: (layer, 0, 0, 0)), full(1, 2 * lru_w),
                  full(*scw.shape), full(1, conv_w), full(2, fft_w, 2 * fft_w)],
        out_specs=[planes] * 4 + [tok(lru_w), tok(conv_w), tok(2 * fft_w)],
        out_shape=[jax.ShapeDtypeStruct((bt, lru_w // LANES, n, LANES), F32)] * 4
        + [shp(lru_w), shp(conv_w), shp(2 * fft_w)],
        scratch_shapes=[pltpu.VMEM((d, in_cols), BF16)],
        compiler_params=_cparams(("arbitrary", "arbitrary")),
        name="inproj_local",
    )(x, shift, scale, g1, w_in, cw, cb, wg, bg, lam, scw, gsc, dftc)


def _scan_kernel(*refs, reverse, add_other):
    if add_other:
        a_ref, b_ref, h0_ref, other_ref, h_ref, hl_ref, carry_ref = refs
    else:
        a_ref, b_ref, h0_ref, h_ref, hl_ref, carry_ref = refs
        other_ref = None

    @pl.when(pl.program_id(2) == 0)
    def _():
        carry_ref[...] = jnp.broadcast_to(h0_ref[0], carry_ref.shape)

    tb = a_ref.shape[2]
    group = SUBLANES * SUBLANES
    row = lax.broadcasted_iota(jnp.int32, (SUBLANES, LANES), 0)
    steps = range(SUBLANES - 1, -1, -1) if reverse else range(SUBLANES)
    first = SUBLANES - 1 if reverse else 0
    last = 0 if reverse else SUBLANES - 1
    groups = range(tb // group - 1, -1, -1) if reverse else range(tb // group)

    def shift_chunks(v, k):
        return pltpu.roll(v, (SUBLANES - k) if reverse else k, axis=0)

    carry = carry_ref[...]
    for g in groups:
        rows = [pl.ds(g * group + s, SUBLANES, stride=SUBLANES) for s in range(SUBLANES)]
        hs, ps = {}, {}
        h = p = None
        for s in steps:
            a = a_ref[0, 0, rows[s], :]
            b = b_ref[0, 0, rows[s], :]
            h = b if h is None else a * h + b
            p = a if p is None else p * a
            hs[s], ps[s] = h, p
        pe, he = p, h
        for k in (1, 2, 4):
            valid = (row <= SUBLANES - 1 - k) if reverse else (row >= k)
            he = jnp.where(valid, he + pe * shift_chunks(he, k), he)
            pe = jnp.where(valid, pe * shift_chunks(pe, k), pe)
        end = he + pe * carry
        h_in = jnp.where(row == first, carry, shift_chunks(end, 1))
        carry = jnp.broadcast_to(end[last:last + 1], end.shape)
        for s in range(SUBLANES):
            out = hs[s] + ps[s] * h_in
            if other_ref is not None:
                out = out + other_ref[0, 0, rows[s], :]
            h_ref[0, 0, rows[s], :] = out
    carry_ref[...] = carry
    hl_ref[0] = carry[0:1]


def _scan_call(a, b, h0, other, *, reverse, tb):
    bt, npl, n, _ = a.shape
    nt = n // tb
    blk = (lambda bi, l, i: (bi, l, nt - 1 - i, 0)) if reverse else (lambda bi, l, i: (bi, l, i, 0))
    tok = pl.BlockSpec((1, 1, tb, LANES), blk)
    st = pl.BlockSpec((1, 1, LANES), lambda bi, l, i: (bi, 0, l))
    ins = [a, b, h0] + ([other] if other is not None else [])
    in_specs = [tok, tok, st] + ([tok] if other is not None else [])
    return pl.pallas_call(
        functools.partial(_scan_kernel, reverse=reverse, add_other=other is not None),
        grid=(bt, npl, nt),
        in_specs=in_specs,
        out_specs=[tok, st],
        out_shape=[jax.ShapeDtypeStruct(a.shape, F32), jax.ShapeDtypeStruct((bt, 1, npl * LANES), F32)],
        scratch_shapes=[pltpu.VMEM((SUBLANES, LANES), F32)],
        compiler_params=_cparams(("parallel", "parallel", "arbitrary")),
        name="lru_scan_bwd" if reverse else "lru_scan_fwd",
    )(*ins)


def _cos_sin(n):
    k = np.arange(n, dtype=np.float64)
    ang = 2.0 * np.pi * np.outer(k, k) / n
    return np.cos(ang), np.sin(ang)


def _slab_copies(src_hbm, dst_ref, sem, b, first, count, width, slot):
    return [pltpu.make_async_copy(src_hbm.at[b, :, first + j, :], dst_ref.at[slot, :, pl.ds(j * width, width)],
                                  sem.at[slot]) for j in range(count)]


def _prefetch_slabs(src_hbm, dst_ref, sem, count, width):
    b, i = pl.program_id(0), pl.program_id(1)
    ni = pl.num_programs(1)
    step = b * ni + i
    slot = step % 2

    @pl.when(step == 0)
    def _():
        for cp in _slab_copies(src_hbm, dst_ref, sem, b, i * count, count, width, slot):
            cp.start()

    @pl.when(step + 1 < pl.num_programs(0) * ni)
    def _():
        wrap = i + 1 == ni
        bn = jnp.where(wrap, b + 1, b)
        nxt = jnp.where(wrap, 0, i + 1)
        for cp in _slab_copies(src_hbm, dst_ref, sem, bn, nxt * count, count, width, 1 - slot):
            cp.start()

    for cp in _slab_copies(src_hbm, dst_ref, sem, b, i * count, count, width, slot):
        cp.wait()
    return step, slot


def _dft1_kernel(x_hbm, m_ref, tc_ref, ts_ref, o_ref, xs_ref, sem, *, n2, jn, fw):
    _, slot = _prefetch_slabs(x_hbm, xs_ref, sem, jn, 2 * fw)
    r = _dot(m_ref[0], xs_ref[slot].astype(BF16))
    for j in range(jn):
        base = j * 2 * fw
        c_uc = r[:n2, base:base + fw]
        c_us = r[:n2, base + fw:base + 2 * fw]
        s_uc = r[n2:, base:base + fw]
        s_us = r[n2:, base + fw:base + 2 * fw]
        br = c_uc - s_us
        bi = -(c_us + s_uc)
        tc = tc_ref[0, :, j:j + 1]
        ts = ts_ref[0, :, j:j + 1]
        o_ref[0, j, :, :fw] = br * tc + bi * ts
        o_ref[0, j, :, fw:] = bi * tc - br * ts


def _dft2_kernel(p_hbm, m_ref, g_ref, y_hbm, ps_ref, ys_ref, sem_in, sem_out, *, n1, kn, fw):
    step, slot = _prefetch_slabs(p_hbm, ps_ref, sem_in, kn, 2 * fw)
    b, i = pl.program_id(0), pl.program_id(1)
    last = pl.num_programs(0) * pl.num_programs(1) - 1

    def out_copies(s):
        return [pltpu.make_async_copy(ys_ref.at[s, :, pl.ds(k * fw, fw)], y_hbm.at[b, :, i * kn + k, :],
                                      sem_out.at[s]) for k in range(kn)]

    r = _dot(m_ref[0], ps_ref[slot].astype(BF16))

    @pl.when(step >= 2)
    def _():
        for cp in out_copies(slot):
            cp.wait()

    for k in range(kn):
        base = k * 2 * fw
        y = r[:n1, base:base + fw] + r[n1:, base + fw:base + 2 * fw]
        ys_ref[slot, :, k * fw:(k + 1) * fw] = _rms(y, g_ref[...])
    for cp in out_copies(slot):
        cp.start()

    @pl.when(step == last)
    def _():
        for cp in out_copies(slot):
            cp.wait()

    @pl.when((step == last) & (step >= 1))
    def _():
        for cp in out_copies(1 - slot):
            cp.wait()


def _dft_direct_kernel(x_ref, m_ref, g_ref, o_ref, *, n, fw):
    x = x_ref[0]
    r = _dot3_const_lhs(m_ref, x)
    y = r[:n, :fw] - r[n:, fw:]
    o_ref[0] = _rms(y, g_ref[...])


def _fourier_call(ucs, g_fft):
    bt, n, fw2 = ucs.shape
    fw = fw2 // 2
    gd = fw // FFT_GROUPS
    scale = 1.0 / math.sqrt(n * gd)
    if n <= 512:
        c, s = _cos_sin(n)
        m = _split_const(np.concatenate([c, s], 0) * scale)
        return pl.pallas_call(
            functools.partial(_dft_direct_kernel, n=n, fw=fw),
            grid=(bt,),
            in_specs=[pl.BlockSpec((1, n, fw2), lambda b: (b, 0, 0)),
                      pl.BlockSpec((2, 2 * n, n), lambda b: (0, 0, 0)),
                      pl.BlockSpec((1, fw), lambda b: (0, 0))],
            out_specs=pl.BlockSpec((1, n, fw), lambda b: (b, 0, 0)),
            out_shape=jax.ShapeDtypeStruct((bt, n, fw), F32),
            compiler_params=_cparams(("parallel",)),
            name="dft_direct",
        )(ucs, m, g_fft)

    n1 = LANES
    n2 = n // n1
    jn = 32
    c2, s2 = _cos_sin(n2)
    m1 = _split_const(np.concatenate([c2, s2], 0))
    ang = 2.0 * np.pi * np.outer(np.arange(n2), np.arange(n1)) / n
    tw = lambda f: jnp.asarray(f(ang).reshape(n2, n1 // jn, jn).transpose(1, 0, 2), F32)
    p = pl.pallas_call(
        functools.partial(_dft1_kernel, n2=n2, jn=jn, fw=fw),
        grid=(bt, n1 // jn),
        in_specs=[pl.BlockSpec(memory_space=pl.ANY),
                  pl.BlockSpec((2, 2 * n2, n2), lambda b, i: (0, 0, 0)),
                  pl.BlockSpec((1, n2, jn), lambda b, i: (i, 0, 0)),
                  pl.BlockSpec((1, n2, jn), lambda b, i: (i, 0, 0))],
        out_specs=pl.BlockSpec((1, jn, n2, fw2), lambda b, i: (b, i, 0, 0)),
        out_shape=jax.ShapeDtypeStruct((bt, n1, n2, fw2), F32),
        scratch_shapes=[pltpu.VMEM((2, n2, jn * fw2), F32), pltpu.SemaphoreType.DMA((2,))],
        compiler_params=_cparams(("arbitrary", "arbitrary")),
        name="dft_stage1",
    )(ucs.reshape(bt, n2, n1, fw2), m1, tw(np.cos), tw(np.sin))

    kn = min(16, n2)
    c1, s1 = _cos_sin(n1)
    m2 = _split_const(np.concatenate([c1, s1], 0) * scale)
    y = pl.pallas_call(
        functools.partial(_dft2_kernel, n1=n1, kn=kn, fw=fw),
        grid=(bt, n2 // kn),
        in_specs=[pl.BlockSpec(memory_space=pl.ANY),
                  pl.BlockSpec((2, 2 * n1, n1), lambda b, i: (0, 0, 0)),
                  pl.BlockSpec((1, fw), lambda b, i: (0, 0))],
        out_specs=pl.BlockSpec(memory_space=pl.ANY),
        out_shape=jax.ShapeDtypeStruct((bt, n1, n2, fw), F32),
        scratch_shapes=[pltpu.VMEM((2, n1, kn * fw2), F32), pltpu.VMEM((2, n1, kn * fw), F32),
                        pltpu.SemaphoreType.DMA((2,)), pltpu.SemaphoreType.DMA((2,))],
        compiler_params=_cparams(("arbitrary", "arbitrary")),
        name="dft_stage2",
    )(p, m2, g_fft)
    return y.reshape(bt, n, fw)


def _outproj_kernel(hs_ref, lg_ref, ysc_ref, yfft_ref, x_ref, wout_ref, glru_ref, gate_ref,
                    sh_ref, sc_ref, g2_ref, wr_ref, xn_ref, h2_ref, aff_ref, woutb_ref, *, sub):
    @pl.when((pl.program_id(0) == 0) & (pl.program_id(1) == 0))
    def _():
        woutb_ref[...] = wout_ref[...].astype(BF16)

    tm, d = x_ref.shape[1], x_ref.shape[2]
    tile = d // LANES
    gmod = g2_ref[...] * (1.0 + sc_ref[0])
    for t in range(tm // sub):
        rs = slice(t * sub, (t + 1) * sub)
        hs = jnp.concatenate([hs_ref[0, k, rs, :] for k in range(hs_ref.shape[1])], axis=-1)
        y_lru = _rms(hs * lg_ref[0, rs, :], glru_ref[...])
        y = jnp.concatenate([y_lru.astype(BF16), ysc_ref[0, rs, :].astype(BF16), yfft_ref[0, rs, :].astype(BF16)],
                            axis=-1)
        xn = x_ref[0, rs, :] + gate_ref[0] * _dot(y, woutb_ref[...])
        xn_ref[0, rs, :] = xn
        h2 = xn * lax.rsqrt(jnp.mean(xn * xn, axis=-1, keepdims=True) + EPS) * gmod + sh_ref[0]
        for k in range(tile):
            h2_ref[0, pl.ds(t * sub * tile + k, sub, stride=tile), :] = h2[:, k * LANES:(k + 1) * LANES]
        h_hi, h_lo = _split(h2)
        logits = _dot_nt(wr_ref[0], h_hi) + (_dot_nt(wr_ref[1], h_hi) + _dot_nt(wr_ref[0], h_lo))
        m = jnp.max(logits, axis=0, keepdims=True)
        e = jnp.exp(logits - m)
        aff_ref[0, :, rs] = e / jnp.sum(e, axis=0, keepdims=True)


def _outproj_call(hs, lg, ysc, yfft, x, w_out, layer, g_lru, gate, shift, scale, g2, w_router_t, *, tm):
    bt, n, d = x.shape
    ne = w_router_t.shape[2]
    tok = lambda w: pl.BlockSpec((1, tm, w), lambda b, i: (b, i, 0))
    vec = lambda w: pl.BlockSpec((1, 1, w), lambda b, i: (b, 0, 0))
    full = lambda *s: pl.BlockSpec(s, lambda b, i: (0,) * len(s))
    return pl.pallas_call(
        functools.partial(_outproj_kernel, sub=min(tm, SUB_ROWS)),
        grid=(bt, n // tm),
        in_specs=[pl.BlockSpec((1, hs.shape[1], tm, LANES), lambda b, i: (b, 0, i, 0)),
                  tok(lg.shape[2]), tok(ysc.shape[2]), tok(yfft.shape[2]), tok(d),
                  pl.BlockSpec((None,) + w_out.shape[1:], lambda b, i: (layer, 0, 0), pipeline_mode=pl.Buffered(1)),
                  full(1, lg.shape[2]), vec(d), vec(d), vec(d), full(1, d),
                  pl.BlockSpec((2, None, ne, d), lambda b, i: (0, layer, 0, 0))],
        out_specs=[tok(d), pl.BlockSpec((1, tm * d // LANES, LANES), lambda b, i: (b, i, 0)),
                   pl.BlockSpec((1, ne, tm), lambda b, i: (b, 0, i))],
        out_shape=[jax.ShapeDtypeStruct((bt, n, d), F32), jax.ShapeDtypeStruct((bt, n * d // LANES, LANES), F32),
                   jax.ShapeDtypeStruct((bt, ne, n), F32)],
        scratch_shapes=[pltpu.VMEM(w_out.shape[1:], BF16)],
        compiler_params=_cparams(("arbitrary", "arbitrary")),
        name="outproj_router",
    )(hs, lg, ysc, yfft, x, w_out, g_lru, gate, shift, scale, g2, w_router_t)


def _select_kernel(aff_ref, idx_ref, g_ref, *, cap):
    aff = aff_ref[0]
    ne, nb, _ = aff.shape

    def count(mask):
        c = jnp.sum(mask.astype(jnp.int32), axis=1, keepdims=True)
        return jnp.sum(c, axis=2, keepdims=True)

    def bit_step(i, t):
        cand = t | (jnp.int32(1) << (30 - i))
        return jnp.where(count(aff >= lax.bitcast_convert_type(cand, F32)) >= cap, cand, t)

    bits = lax.fori_loop(0, 31, bit_step, jnp.zeros((ne, 1, 1), jnp.int32))
    lo = lax.bitcast_convert_type(bits, F32)
    hi = lax.bitcast_convert_type(bits + 1, F32)

    def mid_step(i, lh):
        lo, hi = lh
        mid = (lo + hi) * 0.5
        ok = count(aff >= mid) >= cap
        return jnp.where(ok, mid, lo), jnp.where(ok, hi, mid)

    lo, hi = lax.fori_loop(0, 14, mid_step, (lo, hi))
    gt = aff >= hi
    eq = (aff >= lo) & (aff < hi)
    need = (cap - count(gt)).astype(F32)

    lane = lax.broadcasted_iota(jnp.int32, (LANES, LANES), 0)
    lane_t = lax.broadcasted_iota(jnp.int32, (LANES, LANES), 1)
    tri_incl = (lane <= lane_t).astype(BF16)
    blk = lax.broadcasted_iota(jnp.int32, (nb, nb), 0)
    blk_t = lax.broadcasted_iota(jnp.int32, (nb, nb), 1)
    tri_blk = (blk <= blk_t).astype(BF16)
    ones_row = jnp.ones((SUBLANES, LANES), BF16)
    kvals = lax.broadcasted_iota(jnp.int32, (SUBLANES, nb), 1).astype(BF16)
    slot = lax.broadcasted_iota(jnp.int32, (cap, 1), 0).astype(F32)

    def block_prefix(mask_bf16):
        lc = _dot(mask_bf16, tri_incl)
        cnt_row = _dot_nt(ones_row, mask_bf16)
        inc_row = _dot(cnt_row.astype(BF16), tri_blk)
        return lc, inc_row - cnt_row, inc_row

    for e in range(ne):
        eq_e = eq[e].astype(BF16)
        lc_eq, off_eq, _ = block_prefix(eq_e)
        tri_strict = (blk_t < blk).astype(BF16)
        cnt_col = lc_eq[:, LANES - 1:LANES]
        before = _dot(tri_strict, jnp.broadcast_to(cnt_col, (nb, LANES)).astype(BF16))[:, :1]
        rank = lc_eq - eq[e].astype(F32) + before
        sel = gt[e] | (eq[e] & (rank < need[e]))
        sel_bf = sel.astype(BF16)

        lc, off_row, inc_row = block_prefix(sel_bf)
        off1 = off_row[0:1]
        inc1 = inc_row[0:1]
        onehot = ((slot >= off1) & (slot < inc1))
        oh_bf = onehot.astype(BF16)
        offk = jnp.sum(jnp.where(onehot, off1, 0.0), axis=1, keepdims=True)
        jl = slot - offk
        m = _dot(oh_bf, lc.astype(BF16))
        below = (m <= jl)
        kb_row = _dot_nt(kvals, oh_bf)
        r_row = _dot_nt(ones_row, below.astype(BF16))
        idx_ref[0, e:e + 1, :] = (kb_row[0:1] * float(LANES) + r_row[0:1]).astype(jnp.int32)

        msel = _dot(oh_bf, sel_bf)
        hit = (m == jl + 1.0) & (msel > 0.5)
        a_hi, a_mid = _split(aff[e])
        a_lo = (aff[e] - a_hi.astype(F32) - a_mid.astype(F32)).astype(BF16)
        aff_rows = _dot(oh_bf, a_hi) + (_dot(oh_bf, a_mid) + _dot(oh_bf, a_lo))
        g_ref[0, e] = jnp.sum(jnp.where(hit, aff_rows, 0.0), axis=1, keepdims=True)


def _select_call(aff_t, cap):
    bt, ne, n = aff_t.shape
    nb = n // LANES
    return pl.pallas_call(
        functools.partial(_select_kernel, cap=cap),
        grid=(bt,),
        in_specs=[pl.BlockSpec((1, ne, nb, LANES), lambda b: (b, 0, 0, 0))],
        out_specs=[pl.BlockSpec((1, ne, cap), lambda b: (b, 0, 0)),
                   pl.BlockSpec((1, ne, cap, 1), lambda b: (b, 0, 0, 0))],
        out_shape=[jax.ShapeDtypeStruct((bt, ne, cap), jnp.int32),
                   jax.ShapeDtypeStruct((bt, ne, cap, 1), F32)],
        compiler_params=_cparams(("parallel",)),
        name="expert_select",
    )(aff_t.reshape(bt, ne, nb, LANES))


def _ffn_kernel(idx_ref, idxn_ref, h_hbm, g_ref, gate2_ref, wg_ref, wu_ref, wd_ref, o_ref,
                xs_ref, wgb_ref, wub_ref, wdb_ref, sem, *, cap, chunk, gs, d):
    e = pl.program_id(0)
    bb = pl.program_id(1)
    nbb = pl.num_programs(1)
    step = e * nbb + bb
    slot = step % 2
    rows = gs * cap
    tile = d // LANES

    def row_copy(ids_ref, b0, dst_slot, s, j):
        tok = pl.multiple_of(ids_ref[s, 0, 0, j] * tile, tile)
        return pltpu.make_async_copy(h_hbm.at[b0 + s, pl.ds(tok, tile), :],
                                     xs_ref.at[dst_slot, pl.ds((s * cap + j) * tile, tile), :], sem.at[dst_slot])

    def wait_slot(s):
        pltpu.make_async_copy(h_hbm.at[0, pl.ds(0, rows * tile), :], xs_ref.at[s], sem.at[s]).wait()

    @pl.when(step == 0)
    def _():
        for s in range(gs):
            def start(j, carry):
                row_copy(idx_ref, bb * gs, slot, s, j).start()
                return carry
            lax.fori_loop(0, cap, start, 0, unroll=8)

    @pl.when(bb == 0)
    def _():
        wgb_ref[...] = wg_ref[0, 0].astype(BF16)
        wub_ref[...] = wu_ref[0, 0].astype(BF16)
        wdb_ref[...] = wd_ref[0, 0].astype(BF16)

    wait_slot(slot)

    bn = ((bb + 1) % nbb) * gs
    for c in range(rows // chunk):
        for r in range(c * chunk, (c + 1) * chunk):
            row_copy(idxn_ref, bn, 1 - slot, r // cap, r % cap).start()
        pieces = [xs_ref[slot, pl.ds(c * chunk * tile + k, chunk, stride=tile), :].astype(BF16) for k in range(tile)]
        xb = jnp.concatenate(pieces, axis=-1)
        gate = _dot(xb, wgb_ref[...])
        up = _dot(xb, wub_ref[...])
        hid = (gate * jax.nn.sigmoid(gate)) * up
        y = _dot(hid.astype(BF16), wdb_ref[...])
        s, r0 = (c * chunk) // cap, (c * chunk) % cap
        o_ref[s, 0, r0:r0 + chunk, :] = (y * g_ref[s, 0, r0:r0 + chunk, :]) * gate2_ref[s]

    @pl.when(step == pl.num_programs(0) * nbb - 1)
    def _():
        wait_slot(1 - slot)


def _ffn_call(idx, h2t, g, gate2, wg, wu, wd, layer, *, gs):
    bt, ne, cap = idx.shape
    d = gate2.shape[2]
    ff = wg.shape[3]
    nbb = bt // gs
    chunk = min(cap, 256)
    wspec = lambda s: pl.BlockSpec((1, 1) + s, lambda e, b: (layer, e, 0, 0))
    idx4 = idx.reshape(bt, ne, 1, cap)
    nxt = lambda e, b: ((b + 1) % nbb, jnp.minimum(e + (b + 1) // nbb, ne - 1), 0, 0)
    return pl.pallas_call(
        functools.partial(_ffn_kernel, cap=cap, chunk=chunk, gs=gs, d=d),
        grid=(ne, nbb),
        in_specs=[pl.BlockSpec((gs, 1, 1, cap), lambda e, b: (b, e, 0, 0), memory_space=pltpu.SMEM),
                  pl.BlockSpec((gs, 1, 1, cap), nxt, memory_space=pltpu.SMEM),
                  pl.BlockSpec(memory_space=pl.ANY),
                  pl.BlockSpec((gs, 1, cap, 1), lambda e, b: (b, e, 0, 0)),
                  pl.BlockSpec((gs, 1, d), lambda e, b: (b, 0, 0)),
                  wspec((d, ff)), wspec((d, ff)), wspec((ff, d))],
        out_specs=pl.BlockSpec((gs, 1, cap, d), lambda e, b: (b, e, 0, 0)),
        out_shape=jax.ShapeDtypeStruct((bt, ne, cap, d), F32),
        scratch_shapes=[pltpu.VMEM((2, gs * cap * d // LANES, LANES), F32), pltpu.VMEM((d, ff), BF16),
                        pltpu.VMEM((d, ff), BF16), pltpu.VMEM((ff, d), BF16), pltpu.SemaphoreType.DMA((2,))],
        compiler_params=_cparams(("arbitrary", "arbitrary")),
        name="expert_ffn",
    )(idx4, idx4, h2t, g, gate2, wg, wu, wd)


def _combine_kernel(idx_ref, x_hbm, z_ref, gfin_ref, o_hbm, acc_ref, sem_in, sem_out, *, cap, final_norm, nchunk):
    b = pl.program_id(0)
    e = pl.program_id(1)
    rc = acc_ref.shape[0] // nchunk

    def in_copy(bi, c):
        rs = pl.ds(c * rc, rc)
        return pltpu.make_async_copy(x_hbm.at[bi, rs, :], acc_ref.at[rs, :], sem_in.at[c])

    def out_copy(c):
        rs = pl.ds(c * rc, rc)
        return pltpu.make_async_copy(acc_ref.at[rs, :], o_hbm.at[b, rs, :], sem_out.at[c])

    @pl.when((e == 0) & (b == 0))
    def _():
        for c in range(nchunk):
            in_copy(b, c).start()

    @pl.when(e == 0)
    def _():
        for c in range(nchunk):
            in_copy(b, c).wait()

    group = 16
    for j0 in range(0, cap, group):
        toks = [idx_ref[0, 0, 0, j0 + u] for u in range(group)]
        vals = [acc_ref[pl.ds(toks[u], 1), :] + z_ref[0, 0, j0 + u:j0 + u + 1, :] for u in range(group)]
        for u in range(group):
            acc_ref[pl.ds(toks[u], 1), :] = vals[u]

    @pl.when(e == pl.num_programs(1) - 1)
    def _():
        for c in range(nchunk):
            if final_norm:
                rs = pl.ds(c * rc, rc)
                acc_ref[rs, :] = _rms(acc_ref[rs, :], gfin_ref[...])
            out_copy(c).start()
        for c in range(nchunk):
            out_copy(c).wait()

            @pl.when(b + 1 < pl.num_programs(0))
            def _():
                in_copy(b + 1, c).start()


def _combine_call(idx, x, z, g_final, *, final_norm):
    bt, ne, cap = idx.shape
    _, n, d = x.shape
    nchunk = 32
    return pl.pallas_call(
        functools.partial(_combine_kernel, cap=cap, final_norm=final_norm, nchunk=nchunk),
        grid=(bt, ne),
        in_specs=[pl.BlockSpec((1, 1, 1, cap), lambda b, e: (b, e, 0, 0), memory_space=pltpu.SMEM),
                  pl.BlockSpec(memory_space=pl.ANY),
                  pl.BlockSpec((1, 1, cap, d), lambda b, e: (b, e, 0, 0)),
                  pl.BlockSpec((1, d), lambda b, e: (0, 0))],
        out_specs=pl.BlockSpec(memory_space=pl.ANY),
        out_shape=jax.ShapeDtypeStruct((bt, n, d), F32),
        scratch_shapes=[pltpu.VMEM((n, d), F32), pltpu.SemaphoreType.DMA((nchunk,)),
                        pltpu.SemaphoreType.DMA((nchunk,))],
        compiler_params=_cparams(("arbitrary", "arbitrary")),
        name="expert_combine",
    )(idx.reshape(bt, ne, 1, cap), x, z, g_final)


def _gate_weights(wr, br, wi, bi):
    depth, _, h, hd, _ = wr.shape
    w = jnp.stack([wr, wi], axis=2)
    eye = jnp.eye(h, dtype=w.dtype)
    dense = (w[:, :, :, :, :, None, :] * eye[:, None, :, None]).reshape(depth, 2, 2, h * hd, h * hd)
    ng = h * hd // GATE_GROUP
    blocks = dense.reshape(depth, 2, 2, ng, GATE_GROUP, ng, GATE_GROUP)
    diag = jnp.stack([blocks[:, :, :, g, :, g, :] for g in range(ng)], axis=1)
    wg = diag.transpose(0, 1, 4, 2, 3, 5).reshape(depth, ng, GATE_GROUP, 4 * GATE_GROUP)
    b = jnp.stack([br, bi], axis=2).reshape(depth, 2, 2, ng, GATE_GROUP)
    bg = b.transpose(0, 3, 1, 2, 4).reshape(depth, ng, 1, 4 * GATE_GROUP)
    return wg.astype(BF16), bg


def _channel_dft(fw):
    gd = fw // FFT_GROUPS
    c, s = _cos_sin(gd)
    eye = np.eye(FFT_GROUPS)
    return _split_const(np.concatenate([np.kron(eye, c), np.kron(eye, s)], axis=1))


def _mixer(x, lp, shift, scale, h0_f, h0_b, *, row_w, tm, tb, need_out):
    af, bf, ab, bb, lg, ysc, ucs = _inproj_call(
        x, shift, scale, lp["g1"], lp["w_in"], lp["layer"], lp["cw"], lp["cb"], lp["wg"], lp["bg"], lp["lam"],
        lp["scw"], lp["g_sc"], lp["dftc"], row_w=row_w, tm=tm)
    hf, sf = _scan_call(af, bf, h0_f, None, reverse=False, tb=tb)
    hs, sb = _scan_call(ab, bb, h0_b, hf, reverse=True, tb=tb)
    if not need_out:
        return sf, sb, None
    yfft = _fourier_call(ucs, lp["g_fft"])
    return sf, sb, (hs, lg, ysc, yfft)


def _moe(x, h2, aff_t, gate2, lp, g_final, *, final_norm, gs):
    bt, n, d = x.shape
    cap = CAPACITY_FACTOR * n // N_EXPERTS
    pad = (-n) % (SUBLANES * LANES)
    if pad:
        aff_t = jnp.pad(aff_t, ((0, 0), (0, 0), (0, pad)), constant_values=-1.0)
    idx, g = _select_call(aff_t, cap)
    z = _ffn_call(idx, h2, g, gate2, lp["wge"], lp["wue"], lp["wde"], lp["layer"], gs=gs)
    return _combine_call(idx, x, z, g_final, final_norm=final_norm)


def kernel(x, c, ctx, c_ctx, w_ada, b_ada, g_norm1, w_in, lru_conv_w, lru_conv_b, lru_wr, lru_br, lru_wi,
           lru_bi, lru_lam, sc_conv_w, g_out, w_out, g_norm2, w_router, w_gate_e, w_up_e, w_down_e, g_final):
    depth = w_ada.shape[0]
    bsz, seq, d = x.shape
    ctx_len = ctx.shape[1]
    lru_w = lru_conv_w.shape[2]
    conv_w = sc_conv_w.shape[2]
    fft_w = w_in.shape[2] - 2 * lru_w - 3 * conv_w

    cs = jnp.concatenate([c, c_ctx[None, :], jnp.zeros((SUBLANES - bsz - 1, d), F32)], axis=0)
    mods = _ada_call(cs, w_ada, b_ada)
    dftc = _channel_dft(fft_w)
    gfin = g_final[None, :]
    zero_state = jnp.zeros((bsz, 1, lru_w), F32)
    wg, bg = _gate_weights(lru_wr, lru_br, lru_wi, lru_bi)
    w_router_t = _split_const(jnp.swapaxes(w_router, 1, 2))

    for l in range(depth):
        last = l == depth - 1
        mx = [mods[l, :bsz, None, k * d:(k + 1) * d] for k in range(6)]
        mc = [jnp.broadcast_to(mods[l, bsz, k * d:(k + 1) * d], (bsz, 1, d)) for k in range(6)]
        lp = dict(
            g1=g_norm1[l][None, :], w_in=w_in, cw=lru_conv_w[l], cb=lru_conv_b[l][None, :],
            wg=wg, bg=bg, lam=lru_lam[l].reshape(1, 2 * lru_w), scw=sc_conv_w[l],
            g_sc=g_out[l][None, lru_w:lru_w + conv_w], g_fft=g_out[l][None, lru_w + conv_w:], dftc=dftc,
            wge=w_gate_e, wue=w_up_e, wde=w_down_e, layer=l)
        g_lru = g_out[l][None, :lru_w]

        sf, sb, parts = _mixer(ctx, lp, mc[0], mc[1], zero_state, zero_state,
                               row_w=ctx_len, tm=ctx_len, tb=ctx_len, need_out=not last)
        if not last:
            ctx, hc2, aff_c = _outproj_call(*parts, ctx, w_out, l, g_lru, mc[2], mc[3], mc[4],
                                            g_norm2[l][None, :], w_router_t, tm=ctx_len)
            ctx = _moe(ctx, hc2, aff_c, mc[5], lp, gfin, final_norm=False, gs=bsz)

        _, _, parts = _mixer(x, lp, mx[0], mx[1], sf, sb, row_w=GRID_W, tm=min(seq, 1024), tb=min(seq, 8192), need_out=True)
        x, hx2, aff_x = _outproj_call(*parts, x, w_out, l, g_lru, mx[2], mx[3], mx[4],
                                      g_norm2[l][None, :], w_router_t, tm=min(seq, 1024))
        x = _moe(x, hx2, aff_x, mx[5], lp, gfin, final_norm=last, gs=1)
    return x
```
